```python
import math
import jax, jax.numpy as jnp
from jax import lax
import numpy as np

D_MODEL = 1024
BATCH = 16
SEQ = 2048
DEPTH = 4
DEC_BATCH = 8
DEC_SEQ = 2048
PAST_LEN = 128

HEAD_DIM = 64
N_MIXERS = 3
A_HEADS = 16
A_KV_HEADS = 4
A_RADIUS = 128
B_CHUNK = 128
B_HIDDEN = 2 * D_MODEL
B_GROUPS = 8
C_GROUPS = ((128, 1), (512, 4), (2048, 16))
C_HEADS = 16
C_KV_HEADS = 4
D_FF = 4 * D_MODEL
NUM_BUCKETS = 32
REL_MAX_DISTANCE = 1024
BIAS_HEADS = 16
RMS_EPS = 1e-6
LN_EPS = 1e-5
N_A = len(range(0, DEPTH, N_MIXERS))
N_B = len(range(1, DEPTH, N_MIXERS))
N_C = len(range(2, DEPTH, N_MIXERS))
A_QKV = (A_HEADS + 2 * A_KV_HEADS) * HEAD_DIM
C_GROUP_QKV = (C_HEADS + 2 * C_KV_HEADS) * HEAD_DIM
C_QKV = len(C_GROUPS) * C_GROUP_QKV

kernel_name = "hybrid_bidir_encoder_window_gmlp_dilated"


def rmsnorm(x, g):
    xf = x.astype(jnp.float32)
    y = xf * lax.rsqrt(jnp.mean(xf * xf, axis=-1, keepdims=True) + RMS_EPS)
    return (y * g.astype(jnp.float32)).astype(x.dtype)


def _rel_bucket(rel):
    half = NUM_BUCKETS // 2
    max_exact = half // 2
    n = np.abs(rel)
    large = max_exact + (np.log(np.maximum(n, 1) / max_exact) / np.log(REL_MAX_DISTANCE / max_exact) * (half - max_exact)).astype(np.int32)
    large = np.minimum(large, half - 1)
    return (rel > 0).astype(np.int32) * half + np.where(n < max_exact, n, large)


def _band_bias(rel_bias, blk, radius, dilation):
    width = blk + 2 * radius
    rel = (np.arange(width)[None, :] - radius - np.arange(blk)[:, None]) * dilation
    return jnp.transpose(rel_bias[_rel_bucket(rel)], (2, 0, 1))


def banded_attention(q, k, v, bias, radius, sink=None):
    n, L, hq, hd = q.shape
    hkv = k.shape[2]
    rep = hq // hkv
    blk = math.gcd(radius, L)
    nb = L // blk
    width = blk + 2 * radius
    idx = np.arange(nb)[:, None] * blk + np.arange(width)[None, :]
    key_pos = idx - radius
    rel = np.arange(width)[None, :] - radius - np.arange(blk)[:, None]
    mask = ((key_pos >= 0) & (key_pos < L))[:, None, :] & (np.abs(rel) <= radius)[None]
    pad = ((0, 0), (radius, radius), (0, 0), (0, 0))
    kb = jnp.take(jnp.pad(k, pad), idx.reshape(-1), axis=1).reshape(n, nb, width, hkv, hd)
    vb = jnp.take(jnp.pad(v, pad), idx.reshape(-1), axis=1).reshape(n, nb, width, hkv, hd)
    qb = q.reshape(n, nb, blk, hkv, rep, hd)
    s = jnp.einsum('bnqgrd,bnkgd->bngrqk', qb, kb, preferred_element_type=jnp.float32) * (hd ** -0.5)
    s = s + bias.reshape(hkv, rep, blk, width).astype(jnp.float32)
    s = jnp.where(mask[None, :, None, None], s, -jnp.inf)
    m = jnp.max(s, axis=-1, keepdims=True)
    if sink is not None:
        sk = sink.reshape(hkv, rep, 1, 1).astype(jnp.float32)
        m = jnp.maximum(m, sk)
    p = jnp.exp(s - m)
    denom = jnp.sum(p, axis=-1, keepdims=True)
    if sink is not None:
        denom = denom + jnp.exp(sk - m)
    o = jnp.einsum('bngrqk,bnkgd->bngrqd', p.astype(v.dtype), vb, preferred_element_type=jnp.float32) / denom
    o = jnp.transpose(o, (0, 1, 4, 2, 3, 5)).reshape(n, L, hq, hd).astype(q.dtype)
    lse = jnp.transpose((m + jnp.log(denom))[..., 0], (0, 1, 4, 2, 3)).reshape(n, L, hq)
    return o, lse


def windowed_sink_gqa(h, w_qkv, sink, w_o, rel_bias):
    B, S, _ = h.shape
    q, k, v = jnp.split(h @ w_qkv, [A_HEADS * HEAD_DIM, (A_HEADS + A_KV_HEADS) * HEAD_DIM], axis=-1)
    q = q.reshape(B, S, A_HEADS, HEAD_DIM)
    k = k.reshape(B, S, A_KV_HEADS, HEAD_DIM)
    v = v.reshape(B, S, A_KV_HEADS, HEAD_DIM)
    bias = _band_bias(rel_bias, math.gcd(A_RADIUS, S), A_RADIUS, 1)
    o, _ = banded_attention(q, k, v, bias, A_RADIUS, sink)
    return o.reshape(B, S, A_HEADS * HEAD_DIM) @ w_o


def spatial_gating_mlp(h, w_in, ln_g, ln_b, w_s, b_s, w_out):
    B, S, _ = h.shape
    z = jax.nn.gelu(h @ w_in, approximate=False)
    u, v = jnp.split(z, 2, axis=-1)
    vf = v.astype(jnp.float32)
    mu = jnp.mean(vf, axis=-1, keepdims=True)
    var = jnp.mean(jnp.square(vf - mu), axis=-1, keepdims=True)
    v = ((vf - mu) * lax.rsqrt(var + LN_EPS) * ln_g.astype(jnp.float32) + ln_b.astype(jnp.float32)).astype(h.dtype)
    vc = v.reshape(B, S // B_CHUNK, B_CHUNK, B_GROUPS, B_HIDDEN // B_GROUPS)
    mixed = jnp.einsum('gpq,bcqge->bcpge', w_s, vc) + jnp.transpose(b_s)[:, :, None]
    return (u * mixed.reshape(B, S, B_HIDDEN)) @ w_out


def dilated_mixture_attention(h, w_qkv, w_o, rel_bias):
    B, S, _ = h.shape
    groups = jnp.split(h @ w_qkv, len(C_GROUPS), axis=-1)
    outs, lses = [], []
    for (window, dil), pg in zip(C_GROUPS, groups):
        q, k, v = jnp.split(pg, [C_HEADS * HEAD_DIM, (C_HEADS + C_KV_HEADS) * HEAD_DIM], axis=-1)
        radius = window // (2 * dil)
        L = S // dil

        def to_sub(t, heads):
            t = t.reshape(B, L, dil, heads, HEAD_DIM)
            return jnp.transpose(t, (0, 2, 1, 3, 4)).reshape(B * dil, L, heads, HEAD_DIM)

        bias = _band_bias(rel_bias, math.gcd(radius, L), radius, dil)
        o, lse = banded_attention(to_sub(q, C_HEADS), to_sub(k, C_KV_HEADS), to_sub(v, C_KV_HEADS), bias, radius)
        outs.append(jnp.transpose(o.reshape(B, dil, L, C_HEADS, HEAD_DIM), (0, 2, 1, 3, 4)).reshape(B, S, C_HEADS, HEAD_DIM))
        lses.append(jnp.transpose(lse.reshape(B, dil, L, C_HEADS), (0, 2, 1, 3)).reshape(B, S, C_HEADS))
    wts = jax.nn.softmax(jnp.stack(lses, axis=0), axis=0)
    o = jnp.einsum('gbsh,gbshd->bshd', wts, jnp.stack(outs, axis=0).astype(jnp.float32)).astype(h.dtype)
    return o.reshape(B, S, C_HEADS * HEAD_DIM) @ w_o


def squared_relu_mlp(h, w1, w2):
    return jnp.square(jax.nn.relu(h @ w1)) @ w2


def trunk(x, rel_bias, norm_mix_g, norm_ffn_g, final_g, ffn_w1, ffn_w2, a_wqkv, a_sink, a_wo,
          b_win, b_ln_g, b_ln_b, b_ws, b_bs, b_wo, c_wqkv, c_wo):
    for i in range(DEPTH):
        kind, j = i % N_MIXERS, i // N_MIXERS
        h = rmsnorm(x, norm_mix_g[i])
        if kind == 0:
            m = windowed_sink_gqa(h, a_wqkv[j], a_sink[j], a_wo[j], rel_bias)
        elif kind == 1:
            m = spatial_gating_mlp(h, b_win[j], b_ln_g[j], b_ln_b[j], b_ws[j], b_bs[j], b_wo[j])
        else:
            m = dilated_mixture_attention(h, c_wqkv[j], c_wo[j], rel_bias)
        x = x + m
        x = x + squared_relu_mlp(rmsnorm(x, norm_ffn_g[i]), ffn_w1[i], ffn_w2[i])
    return rmsnorm(x, final_g)


def setup_inputs(seed: int = 0) -> dict:
    key = jax.random.key(seed)
    ks = jax.random.split(key, 20)

    def nrm(k, shape, scale):
        return jax.random.normal(k, shape, jnp.float32) * scale

    return {
        "x_prompt": nrm(ks[0], (BATCH, SEQ, D_MODEL), 1.0),
        "x_sample": nrm(ks[1], (DEC_BATCH, DEC_SEQ, D_MODEL), 1.0),
        "rel_bias": nrm(ks[2], (NUM_BUCKETS, BIAS_HEADS), 0.5),
        "norm_mix_g": 1.0 + nrm(ks[3], (DEPTH, D_MODEL), 0.05),
        "norm_ffn_g": 1.0 + nrm(ks[4], (DEPTH, D_MODEL), 0.05),
        "final_g": 1.0 + nrm(ks[5], (D_MODEL,), 0.05),
        "ffn_w1": nrm(ks[6], (DEPTH, D_MODEL, D_FF), D_MODEL ** -0.5),
        "ffn_w2": nrm(ks[7], (DEPTH, D_FF, D_MODEL), 0.5 * D_FF ** -0.5),
        "a_wqkv": nrm(ks[8], (N_A, D_MODEL, A_QKV), D_MODEL ** -0.5),
        "a_sink": nrm(ks[9], (N_A, A_HEADS), 0.5),
        "a_wo": nrm(ks[10], (N_A, A_HEADS * HEAD_DIM, D_MODEL), (A_HEADS * HEAD_DIM) ** -0.5),
        "b_win": nrm(ks[11], (N_B, D_MODEL, 2 * B_HIDDEN), D_MODEL ** -0.5),
        "b_ln_g": 1.0 + nrm(ks[12], (N_B, B_HIDDEN), 0.05),
        "b_ln_b": nrm(ks[13], (N_B, B_HIDDEN), 0.02),
        "b_ws": nrm(ks[14], (N_B, B_GROUPS, B_CHUNK, B_CHUNK), B_CHUNK ** -0.5),
        "b_bs": 1.0 + nrm(ks[15], (N_B, B_GROUPS, B_CHUNK), 0.1),
        "b_wo": nrm(ks[16], (N_B, B_HIDDEN, D_MODEL), B_HIDDEN ** -0.5),
        "c_wqkv": nrm(ks[17], (N_C, D_MODEL, C_QKV), D_MODEL ** -0.5),
        "c_wo": nrm(ks[18], (N_C, C_HEADS * HEAD_DIM, D_MODEL), (C_HEADS * HEAD_DIM) ** -0.5),
    }


def reference(x_prompt, x_sample, rel_bias, norm_mix_g, norm_ffn_g, final_g, ffn_w1, ffn_w2,
              a_wqkv, a_sink, a_wo, b_win, b_ln_g, b_ln_b, b_ws, b_bs, b_wo, c_wqkv, c_wo):
    y_prompt = trunk(x_prompt, rel_bias, norm_mix_g, norm_ffn_g, final_g, ffn_w1, ffn_w2, a_wqkv, a_sink, a_wo,
                     b_win, b_ln_g, b_ln_b, b_ws, b_bs, b_wo, c_wqkv, c_wo)
    y_sample = trunk(x_sample, rel_bias, norm_mix_g, norm_ffn_g, final_g, ffn_w1, ffn_w2, a_wqkv, a_sink, a_wo,
                     b_win, b_ln_g, b_ln_b, b_ws, b_bs, b_wo, c_wqkv, c_wo)
    return (y_prompt, y_sample)
```

```python
import functools
import math

import numpy as np
import jax
import jax.numpy as jnp
from jax import lax
from jax.experimental import pallas as pl
from jax.experimental.pallas import tpu as pltpu

HEAD_DIM = 64
N_MIXERS = 3
N_HEADS = 16
N_KV_HEADS = 4
A_RADIUS = 128
B_CHUNK = 128
B_GROUPS = 8
C_GROUPS = ((128, 1), (512, 4), (2048, 16))
NUM_BUCKETS = 32
REL_MAX_DISTANCE = 1024
RMS_EPS = 1e-6
LN_EPS = 1e-5

Q_COLS = N_HEADS * HEAD_DIM
KV_COLS = N_KV_HEADS * HEAD_DIM
QKV_COLS = Q_COLS + 2 * KV_COLS
LSE_LANES = 128
MASK_VALUE = -1e30

V7X_VMEM_BYTES = 64 * 1024 * 1024
VMEM_LIMIT_BYTES = V7X_VMEM_BYTES - 8 * 1024 * 1024

ATTN_TQ = 128
FFN_TM = 512
FFN_CHUNK = 1024
PROJ_TM = 512
GMLP_TM = 256


def _const_spec(shape):
    zeros = (0,) * len(shape)
    return pl.BlockSpec(shape, lambda *_: zeros, pipeline_mode=pl.Buffered(1))


def _params(n_grid_dims):
    return pltpu.CompilerParams(
        dimension_semantics=("arbitrary",) * n_grid_dims,
        vmem_limit_bytes=VMEM_LIMIT_BYTES,
    )


def _rmsnorm_f32(x, g):
    ms = jnp.mean(x * x, axis=-1, keepdims=True)
    return x * lax.rsqrt(ms + RMS_EPS) * g


def _proj_kernel(x_ref, g_ref, w_ref, o_ref):
    h = _rmsnorm_f32(x_ref[...], g_ref[...]).astype(jnp.bfloat16)
    y = jnp.dot(h, w_ref[...], preferred_element_type=jnp.float32)
    o_ref[:, :Q_COLS] = (y[:, :Q_COLS] * (HEAD_DIM ** -0.5)).astype(o_ref.dtype)
    o_ref[:, Q_COLS:] = y[:, Q_COLS:].astype(o_ref.dtype)


def _qkv_projection(x, g, w, dil):
    B, S, D = x.shape
    L = S // dil
    tl = min(L, PROJ_TM)
    xv = x.reshape(B, L, dil * D)
    return pl.pallas_call(
        _proj_kernel,
        grid=(B, dil, L // tl),
        in_specs=[
            pl.BlockSpec((None, tl, D), lambda b, r, i: (b, i, r)),
            _const_spec((1, D)),
            _const_spec((D, QKV_COLS)),
        ],
        out_specs=pl.BlockSpec((None, None, tl, QKV_COLS), lambda b, r, i: (b, r, i, 0)),
        out_shape=jax.ShapeDtypeStruct((B, dil, L, QKV_COLS), jnp.bfloat16),
        compiler_params=_params(3),
        name=f"qkv_proj_d{dil}",
    )(xv, g.reshape(1, D), w)


def _rel_bucket(rel):
    half = NUM_BUCKETS // 2
    max_exact = half // 2
    n = np.abs(rel)
    large = max_exact + (np.log(np.maximum(n, 1) / max_exact) / np.log(REL_MAX_DISTANCE / max_exact)
                         * (half - max_exact)).astype(np.int32)
    large = np.minimum(large, half - 1)
    return (rel > 0).astype(np.int32) * half + np.where(n < max_exact, n, large)


def _band_bias_table(rel_bias, tq, radius, dil):
    width = tq + 2 * radius
    rel = np.arange(width)[None, :] - radius - np.arange(tq)[:, None]
    bias = jnp.transpose(rel_bias[_rel_bucket(rel * dil)], (2, 0, 1)).astype(jnp.float32)
    bias = jnp.where(jnp.asarray(np.abs(rel) <= radius)[None], bias, MASK_VALUE)
    return bias.reshape(N_HEADS * tq, width)


def _attn_kernel(*refs, radius, seq_len, has_sink, has_prev, final):
    refs = list(refs)
    q_ref, k_ref, v_ref, bias_ref = refs[:4]
    pos = 4
    sink_ref = None
    if has_sink:
        sink_ref = refs[pos]
        pos += 1
    if has_prev:
        oprev_ref, lprev_ref = refs[pos:pos + 2]
        pos += 2
    if final:
        x_ref, wo_ref, out_ref = refs[pos:pos + 3]
        pos += 3
    else:
        o_ref, lse_ref = refs[pos:pos + 2]
        pos += 2
    kpad_ref, vpad_ref = refs[pos:pos + 2]
    pos += 2
    oscr_ref = refs[pos] if final else None

    tq = q_ref.shape[0]
    width = tq + 2 * radius
    i = pl.program_id(2)

    @pl.when(i == 0)
    def _():
        for src, dst in ((k_ref, kpad_ref), (v_ref, vpad_ref)):
            dst[0:radius, :] = jnp.zeros((radius, KV_COLS), dst.dtype)
            dst[radius:radius + seq_len, :] = src[...]
            dst[radius + seq_len:, :] = jnp.zeros((radius, KV_COLS), dst.dtype)

    q_start = pl.multiple_of(i * tq, tq)
    k_win = kpad_ref[pl.ds(q_start, width), :]
    v_win = vpad_ref[pl.ds(q_start, width), :]
    key_pos = q_start - radius + lax.broadcasted_iota(jnp.int32, (1, width), 1)
    edge = jnp.where((key_pos >= 0) & (key_pos < seq_len), 0.0, MASK_VALUE).astype(jnp.float32)

    if not final:
        lse_ref[...] = jnp.zeros(lse_ref.shape, lse_ref.dtype)

    rep = N_HEADS // N_KV_HEADS
    for h in range(N_HEADS):
        g = h // rep
        cols = slice(h * HEAD_DIM, (h + 1) * HEAD_DIM)
        kv_cols = slice(g * HEAD_DIM, (g + 1) * HEAD_DIM)
        s = lax.dot_general(q_ref[:, cols], k_win[:, kv_cols], (((1,), (1,)), ((), ())),
                            preferred_element_type=jnp.float32)
        s = s + bias_ref[h * tq:(h + 1) * tq, :] + edge
        m = jnp.max(s, axis=-1, keepdims=True)
        if has_sink:
            m = jnp.maximum(m, sink_ref[h])
        p = jnp.exp(s - m)
        denom = jnp.sum(p, axis=-1, keepdims=True)
        if has_sink:
            denom = denom + jnp.exp(sink_ref[h] - m)
        o = jnp.dot(p.astype(jnp.bfloat16), v_win[:, kv_cols], preferred_element_type=jnp.float32)
        o = o / denom
        if has_prev or not final:
            lse = m + jnp.log(denom)
        if has_prev:
            lse_p = lprev_ref[:, h:h + 1]
            top = jnp.maximum(lse_p, lse)
            w_p = jnp.exp(lse_p - top)
            w_c = jnp.exp(lse - top)
            tot = w_p + w_c
            o = (oprev_ref[:, cols] * w_p + o * w_c) / tot
            lse = top + jnp.log(tot)
        if final:
            oscr_ref[:, cols] = o.astype(oscr_ref.dtype)
        else:
            o_ref[:, cols] = o
            lse_ref[:, h:h + 1] = lse

    if final:
        out_ref[...] = x_ref[...] + jnp.dot(oscr_ref[...], wo_ref[...],
                                            preferred_element_type=jnp.float32)


def _band_attention(qkv, bias, radius, *, sink=None, prev=None, final=None):
    B, dil, L, _ = qkv.shape
    S = L * dil
    tq = ATTN_TQ
    width = tq + 2 * radius
    D = Q_COLS

    view_spec = lambda c: pl.BlockSpec((None, tq, c), lambda b, r, i: (b, i, r))
    in_specs = [
        pl.BlockSpec((None, None, tq, Q_COLS), lambda b, r, i: (b, r, i, 0)),
        pl.BlockSpec((None, None, L, KV_COLS), lambda b, r, i: (b, r, 0, Q_COLS // KV_COLS)),
        pl.BlockSpec((None, None, L, KV_COLS), lambda b, r, i: (b, r, 0, Q_COLS // KV_COLS + 1)),
        _const_spec((N_HEADS * tq, width)),
    ]
    args = [qkv, qkv, qkv, bias]
    if sink is not None:
        in_specs.append(pl.BlockSpec(memory_space=pltpu.SMEM))
        args.append(sink.astype(jnp.float32))
    if prev is not None:
        o_prev, lse_prev = prev
        in_specs += [view_spec(D), view_spec(LSE_LANES)]
        args += [o_prev.reshape(B, L, dil * D), lse_prev.reshape(B, L, dil * LSE_LANES)]
    scratch = [pltpu.VMEM((L + 2 * radius, KV_COLS), jnp.bfloat16),
               pltpu.VMEM((L + 2 * radius, KV_COLS), jnp.bfloat16)]
    if final is not None:
        x, w_o = final
        in_specs += [view_spec(D), _const_spec((D, D))]
        args += [x.reshape(B, L, dil * D), w_o]
        out_specs = view_spec(D)
        out_shape = jax.ShapeDtypeStruct((B, L, dil * D), jnp.float32)
        scratch.append(pltpu.VMEM((tq, D), jnp.bfloat16))
    else:
        out_specs = [view_spec(D), view_spec(LSE_LANES)]
        out_shape = [jax.ShapeDtypeStruct((B, L, dil * D), jnp.float32),
                     jax.ShapeDtypeStruct((B, L, dil * LSE_LANES), jnp.float32)]

    kernel = functools.partial(_attn_kernel, radius=radius, seq_len=L, has_sink=sink is not None,
                               has_prev=prev is not None, final=final is not None)
    out = pl.pallas_call(
        kernel,
        grid=(B, dil, L // tq),
        in_specs=in_specs,
        out_specs=out_specs,
        out_shape=out_shape,
        scratch_shapes=scratch,
        compiler_params=_params(3),
        name=f"band_attn_r{radius}_d{dil}",
    )(*args)
    if final is not None:
        return out.reshape(B, S, D)
    o, lse = out
    return o.reshape(B, S, D), lse.reshape(B, S, LSE_LANES)


def _ffn_kernel(*refs, final):
    if final:
        x_ref, g_ref, w1_ref, w2_ref, fg_ref, o_ref = refs
    else:
        x_ref, g_ref, w1_ref, w2_ref, o_ref = refs
    x = x_ref[...]
    h = _rmsnorm_f32(x, g_ref[...]).astype(jnp.bfloat16)
    acc = x
    d_ff = w1_ref.shape[1]
    for c in range(d_ff // FFN_CHUNK):
        cols = slice(c * FFN_CHUNK, (c + 1) * FFN_CHUNK)
        a = jnp.dot(h, w1_ref[:, cols], preferred_element_type=jnp.float32)
        a = jnp.square(jnp.maximum(a, 0.0)).astype(jnp.bfloat16)
        acc = acc + jnp.dot(a, w2_ref[cols, :], preferred_element_type=jnp.float32)
    if final:
        acc = _rmsnorm_f32(acc, fg_ref[...])
    o_ref[...] = acc


def _ffn(x, g, w1, w2, final_g=None):
    B, S, D = x.shape
    d_ff = w1.shape[1]
    T = B * S
    tm = FFN_TM
    row_spec = pl.BlockSpec((tm, D), lambda i: (i, 0))
    in_specs = [row_spec, _const_spec((1, D)), _const_spec((D, d_ff)), _const_spec((d_ff, D))]
    args = [x.reshape(T, D), g.reshape(1, D), w1, w2]
    if final_g is not None:
        in_specs.append(_const_spec((1, D)))
        args.append(final_g.reshape(1, D))
    out = pl.pallas_call(
        functools.partial(_ffn_kernel, final=final_g is not None),
        grid=(T // tm,),
        in_specs=in_specs,
        out_specs=row_spec,
        out_shape=jax.ShapeDtypeStruct((T, D), jnp.float32),
        compiler_params=_params(1),
        name="ffn_final" if final_g is not None else "ffn",
    )(*args)
    return out.reshape(B, S, D)


def _gmlp_kernel(x_ref, g_ref, win_ref, lng_ref, lnb_ref, ws_ref, bs_ref, wout_ref, o_ref):
    x = x_ref[...]
    tm = x.shape[0]
    hidden = wout_ref.shape[0]
    gcols = hidden // B_GROUPS
    h = _rmsnorm_f32(x, g_ref[...]).astype(jnp.bfloat16)
    z = jnp.dot(h, win_ref[...], preferred_element_type=jnp.float32)
    z = 0.5 * z * (1.0 + lax.erf(z * (2.0 ** -0.5)))
    u = z[:, :hidden]
    v = z[:, hidden:]
    mu = jnp.mean(v, axis=-1, keepdims=True)
    vc = v - mu
    var = jnp.mean(vc * vc, axis=-1, keepdims=True)
    vn = (vc * lax.rsqrt(var + LN_EPS) * lng_ref[...] + lnb_ref[...]).astype(jnp.bfloat16)
    rows = []
    for c in range(tm // B_CHUNK):
        r0 = c * B_CHUNK
        parts = []
        for grp in range(B_GROUPS):
            parts.append(jnp.dot(ws_ref[grp], vn[r0:r0 + B_CHUNK, grp * gcols:(grp + 1) * gcols],
                                 preferred_element_type=jnp.float32))
        mixed = jnp.concatenate(parts, axis=-1) + bs_ref[...]
        rows.append((u[r0:r0 + B_CHUNK, :] * mixed).astype(jnp.bfloat16))
    t = jnp.concatenate(rows, axis=0)
    o_ref[...] = x + jnp.dot(t, wout_ref[...], preferred_element_type=jnp.float32)


def _gmlp(x, g, w_in, ln_g, ln_b, w_s, b_s, w_out):
    B, S, D = x.shape
    hidden = w_out.shape[0]
    T = B * S
    tm = GMLP_TM
    bs_full = jnp.repeat(jnp.transpose(b_s).astype(jnp.float32), hidden // B_GROUPS, axis=1)
    row_spec = pl.BlockSpec((tm, D), lambda i: (i, 0))
    out = pl.pallas_call(
        _gmlp_kernel,
        grid=(T // tm,),
        in_specs=[row_spec, _const_spec((1, D)), _const_spec((D, 2 * hidden)),
                  _const_spec((1, hidden)), _const_spec((1, hidden)),
                  _const_spec((B_GROUPS, B_CHUNK, B_CHUNK)), _const_spec((B_CHUNK, hidden)),
                  _const_spec((hidden, D))],
        out_specs=row_spec,
        out_shape=jax.ShapeDtypeStruct((T, D), jnp.float32),
        compiler_params=_params(1),
        name="gmlp",
    )(x.reshape(T, D), g.reshape(1, D), w_in, ln_g.reshape(1, hidden).astype(jnp.float32),
      ln_b.reshape(1, hidden).astype(jnp.float32), w_s, bs_full, w_out)
    return out.reshape(B, S, D)


def _trunk(x, p):
    depth = p["norm_mix_g"].shape[0]
    for i in range(depth):
        kind, j = i % N_MIXERS, i // N_MIXERS
        g_mix = p["norm_mix_g"][i]
        if kind == 0:
            qkv = _qkv_projection(x, g_mix, p["a_wqkv"][j], 1)
            x = _band_attention(qkv, p["a_bias"], A_RADIUS, sink=p["a_sink"][j],
                                final=(x, p["a_wo"][j]))
        elif kind == 1:
            x = _gmlp(x, g_mix, p["b_win"][j], p["b_ln_g"][j], p["b_ln_b"][j], p["b_ws"][j],
                      p["b_bs"][j], p["b_wo"][j])
        else:
            prev = None
            n_groups = len(C_GROUPS)
            for gi, (window, dil) in enumerate(C_GROUPS):
                radius = window // (2 * dil)
                w = p["c_wqkv"][j][:, gi * QKV_COLS:(gi + 1) * QKV_COLS]
                qkv = _qkv_projection(x, g_mix, w, dil)
                if gi < n_groups - 1:
                    prev = _band_attention(qkv, p["c_bias"][gi], radius, prev=prev)
                else:
                    x = _band_attention(qkv, p["c_bias"][gi], radius, prev=prev,
                                        final=(x, p["c_wo"][j]))
        final_g = p["final_g"] if i == depth - 1 else None
        x = _ffn(x, p["norm_ffn_g"][i], p["ffn_w1"][i], p["ffn_w2"][i], final_g)
    return x


def kernel(x_prompt, x_sample, rel_bias, norm_mix_g, norm_ffn_g, final_g, ffn_w1, ffn_w2, a_wqkv,
           a_sink, a_wo, b_win, b_ln_g, b_ln_b, b_ws, b_bs, b_wo, c_wqkv, c_wo):
    bf16 = jnp.bfloat16
    f32 = jnp.float32
    p = {
        "norm_mix_g": norm_mix_g.astype(f32), "norm_ffn_g": norm_ffn_g.astype(f32),
        "final_g": final_g.astype(f32),
        "ffn_w1": ffn_w1.astype(bf16), "ffn_w2": ffn_w2.astype(bf16),
        "a_wqkv": a_wqkv.astype(bf16), "a_sink": a_sink, "a_wo": a_wo.astype(bf16),
        "b_win": b_win.astype(bf16), "b_ln_g": b_ln_g, "b_ln_b": b_ln_b,
        "b_ws": b_ws.astype(bf16), "b_bs": b_bs, "b_wo": b_wo.astype(bf16),
        "c_wqkv": c_wqkv.astype(bf16), "c_wo": c_wo.astype(bf16),
        "a_bias": _band_bias_table(rel_bias, ATTN_TQ, A_RADIUS, 1),
        "c_bias": [_band_bias_table(rel_bias, ATTN_TQ, window // (2 * dil), dil)
                   for window, dil in C_GROUPS],
    }
    return _trunk(x_prompt, p), _trunk(x_sample, p)
```

```python
import functools

import numpy as np
import jax
import jax.numpy as jnp
from jax import lax
from jax.experimental import pallas as pl
from jax.experimental.pallas import tpu as pltpu

HEAD_DIM = 64
N_MIXERS = 3
N_HEADS = 16
N_KV_HEADS = 4
A_RADIUS = 128
B_CHUNK = 128
B_GROUPS = 8
C_GROUPS = ((128, 1), (512, 4), (2048, 16))
NUM_BUCKETS = 32
REL_MAX_DISTANCE = 1024
RMS_EPS = 1e-6
LN_EPS = 1e-5

Q_COLS = N_HEADS * HEAD_DIM
KV_COLS = N_KV_HEADS * HEAD_DIM
QKV_COLS = Q_COLS + 2 * KV_COLS
LANES = 128
HEADS_PER_TILE = LANES // HEAD_DIM
Q_PER_KV = N_HEADS // N_KV_HEADS
MASK_VALUE = -1e30

V7X_VMEM_BYTES = 64 * 1024 * 1024
VMEM_LIMIT_BYTES = V7X_VMEM_BYTES - 8 * 1024 * 1024

ATTN_TQ = 128
FFN_TM = 512
FFN_CHUNK = 1024
PROJ_TM = 512
GMLP_TM = 256


def _const_spec(shape):
    zeros = (0,) * len(shape)
    return pl.BlockSpec(shape, lambda *_: zeros, pipeline_mode=pl.Buffered(1))


def _params(n_grid_dims):
    return pltpu.CompilerParams(
        dimension_semantics=("arbitrary",) * n_grid_dims,
        vmem_limit_bytes=VMEM_LIMIT_BYTES,
    )


def _rmsnorm_f32(x, g):
    ms = jnp.mean(x * x, axis=-1, keepdims=True)
    return x * lax.rsqrt(ms + RMS_EPS) * g


def _lse_lane(head):
    return head // HEADS_PER_TILE + HEAD_DIM * (head % HEADS_PER_TILE)


def _proj_kernel(x_ref, g_ref, w_ref, o_ref):
    h = _rmsnorm_f32(x_ref[...], g_ref[...]).astype(jnp.bfloat16)
    y = jnp.dot(h, w_ref[...], preferred_element_type=jnp.float32)
    o_ref[:, :Q_COLS] = (y[:, :Q_COLS] * (HEAD_DIM ** -0.5)).astype(o_ref.dtype)
    o_ref[:, Q_COLS:] = y[:, Q_COLS:].astype(o_ref.dtype)


def _qkv_projection(x, g, w, dil):
    B, S, D = x.shape
    L = S // dil
    tl = min(L, PROJ_TM)
    xv = x.reshape(B, L, dil * D)
    return pl.pallas_call(
        _proj_kernel,
        grid=(B, dil, L // tl),
        in_specs=[
            pl.BlockSpec((None, tl, D), lambda b, r, i: (b, i, r)),
            _const_spec((1, D)),
            _const_spec((D, QKV_COLS)),
        ],
        out_specs=pl.BlockSpec((None, None, tl, QKV_COLS), lambda b, r, i: (b, r, i, 0)),
        out_shape=jax.ShapeDtypeStruct((B, dil, L, QKV_COLS), jnp.bfloat16),
        compiler_params=_params(3),
        name=f"qkv_proj_d{dil}",
    )(xv, g.reshape(1, D), w)


def _rel_bucket(rel):
    half = NUM_BUCKETS // 2
    max_exact = half // 2
    n = np.abs(rel)
    large = max_exact + (np.log(np.maximum(n, 1) / max_exact) / np.log(REL_MAX_DISTANCE / max_exact)
                         * (half - max_exact)).astype(np.int32)
    large = np.minimum(large, half - 1)
    return (rel > 0).astype(np.int32) * half + np.where(n < max_exact, n, large)


def _dot_heads(g, v):
    return g * Q_PER_KV + v, g * Q_PER_KV + HEADS_PER_TILE + v


def _band_bias_table(rel_bias, tq, radius, dil, seq_len):
    width = tq + 2 * radius
    n_blk = seq_len // tq
    col = np.arange(width)[None, :]
    rel = col - radius - np.arange(tq)[:, None]
    per_head = jnp.transpose(rel_bias[_rel_bucket(rel * dil)], (2, 0, 1)).astype(jnp.float32)
    order = [h for g in range(N_KV_HEADS) for v in range(HEADS_PER_TILE) for h in _dot_heads(g, v)]
    stacked = per_head[np.asarray(order)].reshape(N_HEADS * tq, width)
    in_band = np.broadcast_to(np.abs(rel) <= radius, (N_HEADS, tq, width)).reshape(N_HEADS * tq, width)
    before = col < radius
    after = col >= width - radius
    if n_blk == 1:
        edge_masks = [before | after]
    else:
        edge_masks = [before, np.zeros_like(before), after]
    tables = [jnp.where(jnp.asarray(in_band & ~e), stacked, MASK_VALUE) for e in edge_masks]
    return jnp.stack(tables, axis=0)


def _attn_kernel(*refs, radius, seq_len, has_sink, want_lse):
    refs = list(refs)
    q_ref, k_ref, v_ref, bias_ref = refs[:4]
    pos = 4
    sink_ref = None
    if has_sink:
        sink_ref = refs[pos]
        pos += 1
    o_ref = refs[pos]
    pos += 1
    lse_ref = None
    if want_lse:
        lse_ref = refs[pos]
        pos += 1
    kext_ref, vext_ref = refs[pos:pos + 2]

    tq = q_ref.shape[0]
    width = tq + 2 * radius
    n_blk = seq_len // tq
    i = pl.program_id(2)
    first_step = (pl.program_id(0) == 0) & (pl.program_id(1) == 0) & (i == 0)
    bf16 = jnp.bfloat16

    @pl.when(first_step)
    def _():
        kext_ref[...] = jnp.zeros(kext_ref.shape, bf16)
        rows = vext_ref.shape[0]
        zeros = jnp.zeros((rows, HEAD_DIM), bf16)
        ones = jnp.ones((rows, HEAD_DIM), bf16)
        for t in range(N_KV_HEADS * HEADS_PER_TILE):
            v = t % HEADS_PER_TILE
            tile = [zeros, zeros, ones, zeros] if v == 0 else [zeros, zeros, zeros, ones]
            vext_ref[:, t * 2 * LANES:(t + 1) * 2 * LANES] = jnp.concatenate(tile, axis=1)

    @pl.when(i == 0)
    def _():
        zeros = jnp.zeros((seq_len, HEAD_DIM), bf16)
        body = slice(radius, radius + seq_len)
        for g in range(N_KV_HEADS):
            kg = k_ref[:, g * HEAD_DIM:(g + 1) * HEAD_DIM]
            vg = v_ref[:, g * HEAD_DIM:(g + 1) * HEAD_DIM]
            for v in range(HEADS_PER_TILE):
                t = g * HEADS_PER_TILE + v
                pair = [kg, zeros] if v == 0 else [zeros, kg]
                kext_ref[body, t * LANES:(t + 1) * LANES] = jnp.concatenate(pair, axis=1)
                pair = [vg, zeros] if v == 0 else [zeros, vg]
                vext_ref[body, t * 2 * LANES:t * 2 * LANES + LANES] = jnp.concatenate(pair, axis=1)

    q_start = pl.multiple_of(i * tq, tq)
    if n_blk == 1:
        variant = 0
    else:
        variant = jnp.where(i == 0, 0, jnp.where(i == n_blk - 1, 2, 1))

    lane = lax.broadcasted_iota(jnp.int32, (2 * tq, LANES), 1)
    low_half = lane < HEAD_DIM
    top_rows = lax.broadcasted_iota(jnp.int32, (2 * tq, 1), 0) < tq
    if want_lse:
        lane_tq = lax.broadcasted_iota(jnp.int32, (tq, LANES), 1)
        lse_acc = jnp.zeros((tq, LANES), jnp.float32)

    for g in range(N_KV_HEADS):
        base = g * Q_PER_KV * HEAD_DIM
        q2 = jnp.concatenate([q_ref[:, base:base + LANES], q_ref[:, base + LANES:base + 2 * LANES]], axis=0)
        acc = None
        maxes = []
        sink_terms = []
        for v in range(HEADS_PER_TILE):
            t = g * HEADS_PER_TILE + v
            k_win = kext_ref[pl.ds(q_start, width), t * LANES:(t + 1) * LANES]
            s = lax.dot_general(q2, k_win, (((1,), (1,)), ((), ())), preferred_element_type=jnp.float32)
            s = s + bias_ref[variant, t * 2 * tq:(t + 1) * 2 * tq, :]
            m = jnp.max(s, axis=-1, keepdims=True)
            if has_sink:
                h_top, h_bot = _dot_heads(g, v)
                sk = jnp.where(top_rows, sink_ref[h_top], sink_ref[h_bot])
                m = jnp.maximum(m, sk)
                sink_terms.append(jnp.exp(sk - m))
            maxes.append(m)
            p = jnp.exp(s - m).astype(bf16)
            v_win = vext_ref[pl.ds(q_start, width), t * 2 * LANES:(t + 1) * 2 * LANES]
            part = jnp.dot(p, v_win, preferred_element_type=jnp.float32)
            acc = part if acc is None else acc + part
        denom = acc[:, LANES:]
        if has_sink:
            denom = denom + jnp.where(low_half, sink_terms[0], sink_terms[1])
        o = acc[:, :LANES] / denom
        o_ref[:, base:base + LANES] = o[:tq].astype(o_ref.dtype)
        o_ref[:, base + LANES:base + 2 * LANES] = o[tq:].astype(o_ref.dtype)
        if want_lse:
            lse = jnp.where(low_half, maxes[0], maxes[1]) + jnp.log(denom)
            for half, rows in ((0, slice(0, tq)), (1, slice(tq, 2 * tq))):
                tile = g * HEADS_PER_TILE + half
                here = (lane_tq == tile) | (lane_tq == HEAD_DIM + tile)
                lse_acc = jnp.where(here, lse[rows], lse_acc)
    if want_lse:
        lse_ref[...] = lse_acc


def _band_attention(qkv, bias, radius, *, sink=None, want_lse=False):
    B, dil, L, _ = qkv.shape
    S = L * dil
    tq = ATTN_TQ
    width = tq + 2 * radius
    D = Q_COLS
    n_tiles = N_KV_HEADS * HEADS_PER_TILE

    view_spec = lambda c: pl.BlockSpec((None, tq, c), lambda b, r, i: (b, i, r))
    in_specs = [
        pl.BlockSpec((None, None, tq, Q_COLS), lambda b, r, i: (b, r, i, 0)),
        pl.BlockSpec((None, None, L, KV_COLS), lambda b, r, i: (b, r, 0, Q_COLS // KV_COLS)),
        pl.BlockSpec((None, None, L, KV_COLS), lambda b, r, i: (b, r, 0, Q_COLS // KV_COLS + 1)),
        _const_spec(bias.shape),
    ]
    args = [qkv, qkv, qkv, bias]
    if sink is not None:
        in_specs.append(pl.BlockSpec(memory_space=pltpu.SMEM))
        args.append(sink.astype(jnp.float32))
    out_specs = [view_spec(D)]
    out_shape = [jax.ShapeDtypeStruct((B, L, dil * D), jnp.bfloat16)]
    if want_lse:
        out_specs.append(view_spec(LANES))
        out_shape.append(jax.ShapeDtypeStruct((B, L, dil * LANES), jnp.float32))
    scratch = [pltpu.VMEM((L + 2 * radius, n_tiles * LANES), jnp.bfloat16),
               pltpu.VMEM((L + 2 * radius, n_tiles * 2 * LANES), jnp.bfloat16)]

    kernel = functools.partial(_attn_kernel, radius=radius, seq_len=L, has_sink=sink is not None,
                               want_lse=want_lse)
    out = pl.pallas_call(
        kernel,
        grid=(B, dil, L // tq),
        in_specs=in_specs,
        out_specs=out_specs,
        out_shape=out_shape,
        scratch_shapes=scratch,
        compiler_params=_params(3),
        name=f"band_attn_r{radius}_d{dil}",
    )(*args)
    o = out[0].reshape(B, S, D)
    if want_lse:
        return o, out[1].reshape(B, S, LANES)
    return o


def _head_expand_matrix():
    e = np.zeros((LANES, Q_COLS), np.float32)
    for h in range(N_HEADS):
        e[_lse_lane(h), h * HEAD_DIM:(h + 1) * HEAD_DIM] = 1.0
    return jnp.asarray(e, jnp.bfloat16)


def _ffn_kernel(*refs, n_mix, final):
    refs = list(refs)
    x_ref = refs[0]
    pos = 1
    x = x_ref[...]
    if n_mix >= 1:
        o_refs = refs[pos:pos + n_mix]
        pos += n_mix
        if n_mix > 1:
            lse_refs = refs[pos:pos + n_mix]
            expand_ref = refs[pos + n_mix]
            pos += n_mix + 1
        wo_ref = refs[pos]
        pos += 1
        if n_mix == 1:
            o = o_refs[0][...]
        else:
            lses = [r[...] for r in lse_refs]
            top = functools.reduce(jnp.maximum, lses)
            es = [jnp.exp(l - top) for l in lses]
            inv_tot = 1.0 / functools.reduce(lambda a, b: a + b, es)
            o = None
            for e, o_ref in zip(es, o_refs):
                w = e * inv_tot
                w_hi = w.astype(jnp.bfloat16)
                w_lo = (w - w_hi.astype(jnp.float32)).astype(jnp.bfloat16)
                w_full = (jnp.dot(w_hi, expand_ref[...], preferred_element_type=jnp.float32)
                          + jnp.dot(w_lo, expand_ref[...], preferred_element_type=jnp.float32))
                term = w_full * o_ref[...].astype(jnp.float32)
                o = term if o is None else o + term
            o = o.astype(jnp.bfloat16)
        x = x + jnp.dot(o, wo_ref[...], preferred_element_type=jnp.float32)
    g_ref, w1_ref, w2_ref = refs[pos:pos + 3]
    pos += 3
    if final:
        fg_ref = refs[pos]
        pos += 1
    out_ref = refs[pos]

    h = _rmsnorm_f32(x, g_ref[...]).astype(jnp.bfloat16)
    acc = x
    d_ff = w1_ref.shape[1]
    for c in range(d_ff // FFN_CHUNK):
        cols = slice(c * FFN_CHUNK, (c + 1) * FFN_CHUNK)
        a = jnp.dot(h, w1_ref[:, cols], preferred_element_type=jnp.float32)
        a = jnp.square(jnp.maximum(a, 0.0)).astype(jnp.bfloat16)
        acc = acc + jnp.dot(a, w2_ref[cols, :], preferred_element_type=jnp.float32)
    if final:
        acc = _rmsnorm_f32(acc, fg_ref[...])
    out_ref[...] = acc


def _ffn(x, g, w1, w2, *, mix=None, final_g=None):
    B, S, D = x.shape
    d_ff = w1.shape[1]
    T = B * S
    tm = FFN_TM
    row_spec = lambda c: pl.BlockSpec((tm, c), lambda i: (i, 0))
    in_specs = [row_spec(D)]
    args = [x.reshape(T, D)]
    n_mix = 0
    if mix is not None:
        o_list, lse_list, w_o = mix
        n_mix = len(o_list)
        in_specs += [row_spec(D)] * n_mix
        args += [o.reshape(T, D) for o in o_list]
        if n_mix > 1:
            in_specs += [row_spec(LANES)] * n_mix + [_const_spec((LANES, Q_COLS))]
            args += [l.reshape(T, LANES) for l in lse_list] + [_head_expand_matrix()]
        in_specs.append(_const_spec((D, D)))
        args.append(w_o)
    in_specs += [_const_spec((1, D)), _const_spec((D, d_ff)), _const_spec((d_ff, D))]
    args += [g.reshape(1, D), w1, w2]
    if final_g is not None:
        in_specs.append(_const_spec((1, D)))
        args.append(final_g.reshape(1, D))
    out = pl.pallas_call(
        functools.partial(_ffn_kernel, n_mix=n_mix, final=final_g is not None),
        grid=(T // tm,),
        in_specs=in_specs,
        out_specs=row_spec(D),
        out_shape=jax.ShapeDtypeStruct((T, D), jnp.float32),
        compiler_params=_params(1),
        name=f"ffn_mix{n_mix}" + ("_final" if final_g is not None else ""),
    )(*args)
    return out.reshape(B, S, D)


def _gmlp_kernel(x_ref, g_ref, win_ref, lng_ref, lnb_ref, ws_ref, bs_ref, wout_ref, o_ref):
    x = x_ref[...]
    tm = x.shape[0]
    hidden = wout_ref.shape[0]
    gcols = hidden // B_GROUPS
    h = _rmsnorm_f32(x, g_ref[...]).astype(jnp.bfloat16)
    z = jnp.dot(h, win_ref[...], preferred_element_type=jnp.float32)
    z = 0.5 * z * (1.0 + lax.erf(z * (2.0 ** -0.5)))
    u = z[:, :hidden]
    v = z[:, hidden:]
    mu = jnp.mean(v, axis=-1, keepdims=True)
    vc = v - mu
    var = jnp.mean(vc * vc, axis=-1, keepdims=True)
    vn = (vc * lax.rsqrt(var + LN_EPS) * lng_ref[...] + lnb_ref[...]).astype(jnp.bfloat16)
    rows = []
    for c in range(tm // B_CHUNK):
        r0 = c * B_CHUNK
        parts = []
        for grp in range(B_GROUPS):
            parts.append(jnp.dot(ws_ref[grp], vn[r0:r0 + B_CHUNK, grp * gcols:(grp + 1) * gcols],
                                 preferred_element_type=jnp.float32))
        mixed = jnp.concatenate(parts, axis=-1) + bs_ref[...]
        rows.append((u[r0:r0 + B_CHUNK, :] * mixed).astype(jnp.bfloat16))
    t = jnp.concatenate(rows, axis=0)
    o_ref[...] = x + jnp.dot(t, wout_ref[...], preferred_element_type=jnp.float32)


def _gmlp(x, g, w_in, ln_g, ln_b, w_s, b_s, w_out):
    B, S, D = x.shape
    hidden = w_out.shape[0]
    T = B * S
    tm = GMLP_TM
    bs_full = jnp.repeat(jnp.transpose(b_s).astype(jnp.float32), hidden // B_GROUPS, axis=1)
    row_spec = pl.BlockSpec((tm, D), lambda i: (i, 0))
    out = pl.pallas_call(
        _gmlp_kernel,
        grid=(T // tm,),
        in_specs=[row_spec, _const_spec((1, D)), _const_spec((D, 2 * hidden)),
                  _const_spec((1, hidden)), _const_spec((1, hidden)),
                  _const_spec((B_GROUPS, B_CHUNK, B_CHUNK)), _const_spec((B_CHUNK, hidden)),
                  _const_spec((hidden, D))],
        out_specs=row_spec,
        out_shape=jax.ShapeDtypeStruct((T, D), jnp.float32),
        compiler_params=_params(1),
        name="gmlp",
    )(x.reshape(T, D), g.reshape(1, D), w_in, ln_g.reshape(1, hidden).astype(jnp.float32),
      ln_b.reshape(1, hidden).astype(jnp.float32), w_s, bs_full, w_out)
    return out.reshape(B, S, D)


def _trunk(x, p):
    depth = p["norm_mix_g"].shape[0]
    S = x.shape[1]
    for i in range(depth):
        kind, j = i % N_MIXERS, i // N_MIXERS
        g_mix = p["norm_mix_g"][i]
        mix = None
        if kind == 0:
            qkv = _qkv_projection(x, g_mix, p["a_wqkv"][j], 1)
            o = _band_attention(qkv, p["a_bias"], A_RADIUS, sink=p["a_sink"][j])
            mix = ([o], None, p["a_wo"][j])
        elif kind == 1:
            x = _gmlp(x, g_mix, p["b_win"][j], p["b_ln_g"][j], p["b_ln_b"][j], p["b_ws"][j],
                      p["b_bs"][j], p["b_wo"][j])
        else:
            o_list, lse_list = [], []
            for gi, (window, dil) in enumerate(C_GROUPS):
                w = p["c_wqkv"][j][:, gi * QKV_COLS:(gi + 1) * QKV_COLS]
                qkv = _qkv_projection(x, g_mix, w, dil)
                o, lse = _band_attention(qkv, p["c_bias"][gi], window // (2 * dil), want_lse=True)
                o_list.append(o)
                lse_list.append(lse)
            mix = (o_list, lse_list, p["c_wo"][j])
        final_g = p["final_g"] if i == depth - 1 else None
        x = _ffn(x, p["norm_ffn_g"][i], p["ffn_w1"][i], p["ffn_w2"][i], mix=mix, final_g=final_g)
    return x


def kernel(x_prompt, x_sample, rel_bias, norm_mix_g, norm_ffn_g, final_g, ffn_w1, ffn_w2, a_wqkv,
           a_sink, a_wo, b_win, b_ln_g, b_ln_b, b_ws, b_bs, b_wo, c_wqkv, c_wo):
    bf16 = jnp.bfloat16
    f32 = jnp.float32
    S = x_prompt.shape[1]
    assert x_sample.shape[1] == S
    p = {
        "norm_mix_g": norm_mix_g.astype(f32), "norm_ffn_g": norm_ffn_g.astype(f32),
        "final_g": final_g.astype(f32),
        "ffn_w1": ffn_w1.astype(bf16), "ffn_w2": ffn_w2.astype(bf16),
        "a_wqkv": a_wqkv.astype(bf16), "a_sink": a_sink, "a_wo": a_wo.astype(bf16),
        "b_win": b_win.astype(bf16), "b_ln_g": b_ln_g, "b_ln_b": b_ln_b,
        "b_ws": b_ws.astype(bf16), "b_bs": b_bs, "b_wo": b_wo.astype(bf16),
        "c_wqkv": c_wqkv.astype(bf16), "c_wo": c_wo.astype(bf16),
        "a_bias": _band_bias_table(rel_bias, ATTN_TQ, A_RADIUS, 1, S),
        "c_bias": [_band_bias_table(rel_bias, ATTN_TQ, window // (2 * dil), dil, S // dil)
                   for window, dil in C_GROUPS],
    }
    return _trunk(x_prompt, p), _trunk(x_sample, p)
```

```python
import functools

import numpy as np
import jax
import jax.numpy as jnp
from jax import lax
from jax.experimental import pallas as pl
from jax.experimental.pallas import tpu as pltpu

HEAD_DIM = 64
N_MIXERS = 3
N_HEADS = 16
N_KV_HEADS = 4
A_RADIUS = 128
B_CHUNK = 128
B_GROUPS = 8
C_GROUPS = ((128, 1), (512, 4), (2048, 16))
NUM_BUCKETS = 32
REL_MAX_DISTANCE = 1024
RMS_EPS = 1e-6
LN_EPS = 1e-5

Q_COLS = N_HEADS * HEAD_DIM
KV_COLS = N_KV_HEADS * HEAD_DIM
QKV_COLS = Q_COLS + 2 * KV_COLS
LANES = 128
HEADS_PER_TILE = LANES // HEAD_DIM
Q_PER_KV = N_HEADS // N_KV_HEADS
MASK_VALUE = -1e30

V7X_VMEM_BYTES = 64 * 1024 * 1024
VMEM_LIMIT_BYTES = V7X_VMEM_BYTES - 8 * 1024 * 1024

ATTN_TQ = 128
FFN_TM = 512
FFN_CHUNK = 1024
PROJ_TM = 512
GMLP_TM = 256


def _const_spec(shape):
    zeros = (0,) * len(shape)
    return pl.BlockSpec(shape, lambda *_: zeros, pipeline_mode=pl.Buffered(1))


def _params(n_grid_dims):
    return pltpu.CompilerParams(
        dimension_semantics=("arbitrary",) * n_grid_dims,
        vmem_limit_bytes=VMEM_LIMIT_BYTES,
    )


def _rmsnorm_f32(x, g):
    ms = jnp.mean(x * x, axis=-1, keepdims=True)
    return x * lax.rsqrt(ms + RMS_EPS) * g


def _lse_lane(head):
    return head // HEADS_PER_TILE + HEAD_DIM * (head % HEADS_PER_TILE)


def _proj_kernel(x_ref, g_ref, w_ref, o_ref):
    h = _rmsnorm_f32(x_ref[...], g_ref[...]).astype(jnp.bfloat16)
    y = jnp.dot(h, w_ref[...], preferred_element_type=jnp.float32)
    o_ref[:, :Q_COLS] = (y[:, :Q_COLS] * (HEAD_DIM ** -0.5)).astype(o_ref.dtype)
    o_ref[:, Q_COLS:] = y[:, Q_COLS:].astype(o_ref.dtype)


def _qkv_projection(x, g, w, dil):
    B, S, D = x.shape
    L = S // dil
    tl = min(L, PROJ_TM)
    xv = x.reshape(B, L, dil * D)
    return pl.pallas_call(
        _proj_kernel,
        grid=(B, dil, L // tl),
        in_specs=[
            pl.BlockSpec((None, tl, D), lambda b, r, i: (b, i, r)),
            _const_spec((1, D)),
            _const_spec((D, QKV_COLS)),
        ],
        out_specs=pl.BlockSpec((None, None, tl, QKV_COLS), lambda b, r, i: (b, r, i, 0)),
        out_shape=jax.ShapeDtypeStruct((B, dil, L, QKV_COLS), jnp.bfloat16),
        compiler_params=_params(3),
        name=f"qkv_proj_d{dil}",
    )(xv, g.reshape(1, D), w)


def _rel_bucket(rel):
    half = NUM_BUCKETS // 2
    max_exact = half // 2
    n = np.abs(rel)
    large = max_exact + (np.log(np.maximum(n, 1) / max_exact) / np.log(REL_MAX_DISTANCE / max_exact)
                         * (half - max_exact)).astype(np.int32)
    large = np.minimum(large, half - 1)
    return (rel > 0).astype(np.int32) * half + np.where(n < max_exact, n, large)


def _dot_heads(g, v):
    return g * Q_PER_KV + v, g * Q_PER_KV + HEADS_PER_TILE + v


def _band_bias_table(rel_bias, tq, radius, dil, seq_len):
    width = tq + 2 * radius
    n_blk = seq_len // tq
    col = np.arange(width)[None, :]
    rel = col - radius - np.arange(tq)[:, None]
    n_diag = width + tq - 1
    diag_rel = (np.arange(n_diag) - (tq - 1) - radius) * dil
    per_diag = jnp.transpose(rel_bias[_rel_bucket(diag_rel)], (1, 0)).astype(jnp.float32)
    padded = jnp.pad(per_diag, ((0, 0), (0, 1)))
    tiled = jnp.broadcast_to(padded[:, None, :], (N_HEADS, tq, n_diag + 1)).reshape(N_HEADS, -1)
    skewed = tiled[:, :tq * n_diag].reshape(N_HEADS, tq, n_diag)
    per_head = skewed[:, :, tq - 1:]
    order = [h for g in range(N_KV_HEADS) for v in range(HEADS_PER_TILE) for h in _dot_heads(g, v)]
    stacked = per_head[np.asarray(order)].reshape(N_HEADS * tq, width)
    in_band = np.broadcast_to(np.abs(rel) <= radius, (N_HEADS, tq, width)).reshape(N_HEADS * tq, width)
    before = col < radius
    after = col >= width - radius
    if n_blk == 1:
        edge_masks = [before | after]
    else:
        edge_masks = [before, np.zeros_like(before), after]
    tables = [jnp.where(jnp.asarray(in_band & ~e), stacked, MASK_VALUE) for e in edge_masks]
    return jnp.stack(tables, axis=0)


def _attn_kernel(*refs, radius, seq_len, has_sink, want_lse):
    refs = list(refs)
    q_ref, k_ref, v_ref, bias_ref = refs[:4]
    pos = 4
    sink_ref = None
    if has_sink:
        sink_ref = refs[pos]
        pos += 1
    o_ref = refs[pos]
    pos += 1
    lse_ref = None
    if want_lse:
        lse_ref = refs[pos]
        pos += 1
    kext_ref, vext_ref = refs[pos:pos + 2]

    tq = q_ref.shape[0]
    width = tq + 2 * radius
    n_blk = seq_len // tq
    i = pl.program_id(2)
    first_step = (pl.program_id(0) == 0) & (pl.program_id(1) == 0) & (i == 0)
    bf16 = jnp.bfloat16

    @pl.when(first_step)
    def _():
        kext_ref[...] = jnp.zeros(kext_ref.shape, bf16)
        rows = vext_ref.shape[0]
        zeros = jnp.zeros((rows, HEAD_DIM), bf16)
        ones = jnp.ones((rows, HEAD_DIM), bf16)
        for t in range(N_KV_HEADS * HEADS_PER_TILE):
            v = t % HEADS_PER_TILE
            tile = [zeros, zeros, ones, zeros] if v == 0 else [zeros, zeros, zeros, ones]
            vext_ref[:, t * 2 * LANES:(t + 1) * 2 * LANES] = jnp.concatenate(tile, axis=1)

    @pl.when(i == 0)
    def _():
        zeros = jnp.zeros((seq_len, HEAD_DIM), bf16)
        body = slice(radius, radius + seq_len)
        for g in range(N_KV_HEADS):
            kg = k_ref[:, g * HEAD_DIM:(g + 1) * HEAD_DIM]
            vg = v_ref[:, g * HEAD_DIM:(g + 1) * HEAD_DIM]
            for v in range(HEADS_PER_TILE):
                t = g * HEADS_PER_TILE + v
                pair = [kg, zeros] if v == 0 else [zeros, kg]
                kext_ref[body, t * LANES:(t + 1) * LANES] = jnp.concatenate(pair, axis=1)
                pair = [vg, zeros] if v == 0 else [zeros, vg]
                vext_ref[body, t * 2 * LANES:t * 2 * LANES + LANES] = jnp.concatenate(pair, axis=1)

    q_start = pl.multiple_of(i * tq, tq)
    if n_blk == 1:
        variant = 0
    else:
        variant = jnp.where(i == 0, 0, jnp.where(i == n_blk - 1, 2, 1))

    lane = lax.broadcasted_iota(jnp.int32, (2 * tq, LANES), 1)
    low_half = lane < HEAD_DIM
    top_rows = lax.broadcasted_iota(jnp.int32, (2 * tq, 1), 0) < tq
    if want_lse:
        lane_tq = lax.broadcasted_iota(jnp.int32, (tq, LANES), 1)
        lse_acc = jnp.zeros((tq, LANES), jnp.float32)

    def scores(g):
        base = g * Q_PER_KV * HEAD_DIM
        q2 = jnp.concatenate([q_ref[:, base:base + LANES], q_ref[:, base + LANES:base + 2 * LANES]], axis=0)
        k_win = jnp.concatenate(
            [kext_ref[pl.ds(q_start, width), t * LANES:(t + 1) * LANES]
             for t in range(g * HEADS_PER_TILE, (g + 1) * HEADS_PER_TILE)], axis=0)
        return lax.dot_general(q2, k_win, (((1,), (1,)), ((), ())), preferred_element_type=jnp.float32)

    s_next = scores(0)
    for g in range(N_KV_HEADS):
        base = g * Q_PER_KV * HEAD_DIM
        s_all = s_next
        if g + 1 < N_KV_HEADS:
            s_next = scores(g + 1)
        probs = []
        maxes = []
        sink_terms = []
        for v in range(HEADS_PER_TILE):
            t = g * HEADS_PER_TILE + v
            s = s_all[:, v * width:(v + 1) * width] + bias_ref[variant, t * 2 * tq:(t + 1) * 2 * tq, :]
            m = jnp.max(s, axis=-1, keepdims=True)
            if has_sink:
                h_top, h_bot = _dot_heads(g, v)
                sk = jnp.where(top_rows, sink_ref[h_top], sink_ref[h_bot])
                m = jnp.maximum(m, sk)
                sink_terms.append(jnp.exp(sk - m))
            maxes.append(m)
            probs.append(jnp.exp(s - m).astype(bf16))
        v_win = jnp.concatenate(
            [vext_ref[pl.ds(q_start, width), t * 2 * LANES:(t + 1) * 2 * LANES]
             for t in range(g * HEADS_PER_TILE, (g + 1) * HEADS_PER_TILE)], axis=0)
        acc = jnp.dot(jnp.concatenate(probs, axis=1), v_win, preferred_element_type=jnp.float32)
        denom = acc[:, LANES:]
        if has_sink:
            denom = denom + jnp.where(low_half, sink_terms[0], sink_terms[1])
        o = acc[:, :LANES] / denom
        o_ref[:, base:base + LANES] = o[:tq].astype(o_ref.dtype)
        o_ref[:, base + LANES:base + 2 * LANES] = o[tq:].astype(o_ref.dtype)
        if want_lse:
            lse = jnp.where(low_half, maxes[0], maxes[1]) + jnp.log(denom)
            for half, rows in ((0, slice(0, tq)), (1, slice(tq, 2 * tq))):
                tile = g * HEADS_PER_TILE + half
                here = (lane_tq == tile) | (lane_tq == HEAD_DIM + tile)
                lse_acc = jnp.where(here, lse[rows], lse_acc)
    if want_lse:
        lse_ref[...] = lse_acc


def _band_attention(qkv, bias, radius, *, sink=None, want_lse=False):
    B, dil, L, _ = qkv.shape
    S = L * dil
    tq = ATTN_TQ
    width = tq + 2 * radius
    D = Q_COLS
    n_tiles = N_KV_HEADS * HEADS_PER_TILE

    view_spec = lambda c: pl.BlockSpec((None, tq, c), lambda b, r, i: (b, i, r))
    in_specs = [
        pl.BlockSpec((None, None, tq, Q_COLS), lambda b, r, i: (b, r, i, 0)),
        pl.BlockSpec((None, None, L, KV_COLS), lambda b, r, i: (b, r, 0, Q_COLS // KV_COLS)),
        pl.BlockSpec((None, None, L, KV_COLS), lambda b, r, i: (b, r, 0, Q_COLS // KV_COLS + 1)),
        _const_spec(bias.shape),
    ]
    args = [qkv, qkv, qkv, bias]
    if sink is not None:
        in_specs.append(pl.BlockSpec(memory_space=pltpu.SMEM))
        args.append(sink.astype(jnp.float32))
    out_specs = [view_spec(D)]
    out_shape = [jax.ShapeDtypeStruct((B, L, dil * D), jnp.bfloat16)]
    if want_lse:
        out_specs.append(view_spec(LANES))
        out_shape.append(jax.ShapeDtypeStruct((B, L, dil * LANES), jnp.float32))
    scratch = [pltpu.VMEM((L + 2 * radius, n_tiles * LANES), jnp.bfloat16),
               pltpu.VMEM((L + 2 * radius, n_tiles * 2 * LANES), jnp.bfloat16)]

    kernel = functools.partial(_attn_kernel, radius=radius, seq_len=L, has_sink=sink is not None,
                               want_lse=want_lse)
    out = pl.pallas_call(
        kernel,
        grid=(B, dil, L // tq),
        in_specs=in_specs,
        out_specs=out_specs,
        out_shape=out_shape,
        scratch_shapes=scratch,
        compiler_params=_params(3),
        name=f"band_attn_r{radius}_d{dil}",
    )(*args)
    o = out[0].reshape(B, S, D)
    if want_lse:
        return o, out[1].reshape(B, S, LANES)
    return o


def _head_expand_matrix():
    e = np.zeros((LANES, Q_COLS), np.float32)
    for h in range(N_HEADS):
        e[_lse_lane(h), h * HEAD_DIM:(h + 1) * HEAD_DIM] = 1.0
    return jnp.asarray(e, jnp.bfloat16)


def _ffn_kernel(*refs, n_mix, final):
    refs = list(refs)
    x_ref = refs[0]
    pos = 1
    x = x_ref[...]
    if n_mix >= 1:
        o_refs = refs[pos:pos + n_mix]
        pos += n_mix
        if n_mix > 1:
            lse_refs = refs[pos:pos + n_mix]
            expand_ref = refs[pos + n_mix]
            pos += n_mix + 1
        wo_ref = refs[pos]
        pos += 1
        if n_mix == 1:
            o = o_refs[0][...]
        else:
            lses = [r[...] for r in lse_refs]
            top = functools.reduce(jnp.maximum, lses)
            es = [jnp.exp(l - top) for l in lses]
            inv_tot = 1.0 / functools.reduce(lambda a, b: a + b, es)
            o = None
            for e, o_ref in zip(es, o_refs):
                w = e * inv_tot
                w_hi = w.astype(jnp.bfloat16)
                w_lo = (w - w_hi.astype(jnp.float32)).astype(jnp.bfloat16)
                w_full = (jnp.dot(w_hi, expand_ref[...], preferred_element_type=jnp.float32)
                          + jnp.dot(w_lo, expand_ref[...], preferred_element_type=jnp.float32))
                term = w_full * o_ref[...].astype(jnp.float32)
                o = term if o is None else o + term
            o = o.astype(jnp.bfloat16)
        x = x + jnp.dot(o, wo_ref[...], preferred_element_type=jnp.float32)
    g_ref, w1_ref, w2_ref = refs[pos:pos + 3]
    pos += 3
    if final:
        fg_ref = refs[pos]
        pos += 1
    out_ref = refs[pos]

    h = _rmsnorm_f32(x, g_ref[...]).astype(jnp.bfloat16)
    acc = x
    d_ff = w1_ref.shape[1]
    for c in range(d_ff // FFN_CHUNK):
        cols = slice(c * FFN_CHUNK, (c + 1) * FFN_CHUNK)
        a = jnp.dot(h, w1_ref[:, cols], preferred_element_type=jnp.float32)
        a = jnp.square(jnp.maximum(a, 0.0)).astype(jnp.bfloat16)
        acc = acc + jnp.dot(a, w2_ref[cols, :], preferred_element_type=jnp.float32)
    if final:
        acc = _rmsnorm_f32(acc, fg_ref[...])
    out_ref[...] = acc


def _ffn(x, g, w1, w2, *, mix=None, final_g=None):
    B, S, D = x.shape
    d_ff = w1.shape[1]
    T = B * S
    tm = FFN_TM
    row_spec = lambda c: pl.BlockSpec((tm, c), lambda i: (i, 0))
    in_specs = [row_spec(D)]
    args = [x.reshape(T, D)]
    n_mix = 0
    if mix is not None:
        o_list, lse_list, w_o = mix
        n_mix = len(o_list)
        in_specs += [row_spec(D)] * n_mix
        args += [o.reshape(T, D) for o in o_list]
        if n_mix > 1:
            in_specs += [row_spec(LANES)] * n_mix + [_const_spec((LANES, Q_COLS))]
            args += [l.reshape(T, LANES) for l in lse_list] + [_head_expand_matrix()]
        in_specs.append(_const_spec((D, D)))
        args.append(w_o)
    in_specs += [_const_spec((1, D)), _const_spec((D, d_ff)), _const_spec((d_ff, D))]
    args += [g.reshape(1, D), w1, w2]
    if final_g is not None:
        in_specs.append(_const_spec((1, D)))
        args.append(final_g.reshape(1, D))
    out = pl.pallas_call(
        functools.partial(_ffn_kernel, n_mix=n_mix, final=final_g is not None),
        grid=(T // tm,),
        in_specs=in_specs,
        out_specs=row_spec(D),
        out_shape=jax.ShapeDtypeStruct((T, D), jnp.float32),
        compiler_params=_params(1),
        name=f"ffn_mix{n_mix}" + ("_final" if final_g is not None else ""),
    )(*args)
    return out.reshape(B, S, D)


def _gmlp_kernel(x_ref, g_ref, win_ref, lng_ref, lnb_ref, ws_ref, bs_ref, wout_ref, o_ref):
    x = x_ref[...]
    tm = x.shape[0]
    hidden = wout_ref.shape[0]
    gcols = hidden // B_GROUPS
    h = _rmsnorm_f32(x, g_ref[...]).astype(jnp.bfloat16)
    z = jnp.dot(h, win_ref[...], preferred_element_type=jnp.float32)
    z = 0.5 * z * (1.0 + lax.erf(z * (2.0 ** -0.5)))
    u = z[:, :hidden]
    v = z[:, hidden:]
    mu = jnp.mean(v, axis=-1, keepdims=True)
    vc = v - mu
    var = jnp.mean(vc * vc, axis=-1, keepdims=True)
    vn = (vc * lax.rsqrt(var + LN_EPS) * lng_ref[...] + lnb_ref[...]).astype(jnp.bfloat16)
    rows = []
    for c in range(tm // B_CHUNK):
        r0 = c * B_CHUNK
        parts = []
        for grp in range(B_GROUPS):
            parts.append(jnp.dot(ws_ref[grp], vn[r0:r0 + B_CHUNK, grp * gcols:(grp + 1) * gcols],
                                 preferred_element_type=jnp.float32))
        mixed = jnp.concatenate(parts, axis=-1) + bs_ref[...]
        rows.append((u[r0:r0 + B_CHUNK, :] * mixed).astype(jnp.bfloat16))
    t = jnp.concatenate(rows, axis=0)
    o_ref[...] = x + jnp.dot(t, wout_ref[...], preferred_element_type=jnp.float32)


def _gmlp(x, g, w_in, ln_g, ln_b, w_s, b_s, w_out):
    B, S, D = x.shape
    hidden = w_out.shape[0]
    T = B * S
    tm = GMLP_TM
    bs_full = jnp.repeat(jnp.transpose(b_s).astype(jnp.float32), hidden // B_GROUPS, axis=1)
    row_spec = pl.BlockSpec((tm, D), lambda i: (i, 0))
    out = pl.pallas_call(
        _gmlp_kernel,
        grid=(T // tm,),
        in_specs=[row_spec, _const_spec((1, D)), _const_spec((D, 2 * hidden)),
                  _const_spec((1, hidden)), _const_spec((1, hidden)),
                  _const_spec((B_GROUPS, B_CHUNK, B_CHUNK)), _const_spec((B_CHUNK, hidden)),
                  _const_spec((hidden, D))],
        out_specs=row_spec,
        out_shape=jax.ShapeDtypeStruct((T, D), jnp.float32),
        compiler_params=_params(1),
        name="gmlp",
    )(x.reshape(T, D), g.reshape(1, D), w_in, ln_g.reshape(1, hidden).astype(jnp.float32),
      ln_b.reshape(1, hidden).astype(jnp.float32), w_s, bs_full, w_out)
    return out.reshape(B, S, D)


def _trunk(x, p):
    depth = p["norm_mix_g"].shape[0]
    S = x.shape[1]
    for i in range(depth):
        kind, j = i % N_MIXERS, i // N_MIXERS
        g_mix = p["norm_mix_g"][i]
        mix = None
        if kind == 0:
            qkv = _qkv_projection(x, g_mix, p["a_wqkv"][j], 1)
            o = _band_attention(qkv, p["a_bias"], A_RADIUS, sink=p["a_sink"][j])
            mix = ([o], None, p["a_wo"][j])
        elif kind == 1:
            x = _gmlp(x, g_mix, p["b_win"][j], p["b_ln_g"][j], p["b_ln_b"][j], p["b_ws"][j],
                      p["b_bs"][j], p["b_wo"][j])
        else:
            o_list, lse_list = [], []
            for gi, (window, dil) in enumerate(C_GROUPS):
                w = p["c_wqkv"][j][:, gi * QKV_COLS:(gi + 1) * QKV_COLS]
                qkv = _qkv_projection(x, g_mix, w, dil)
                o, lse = _band_attention(qkv, p["c_bias"][gi], window // (2 * dil), want_lse=True)
                o_list.append(o)
                lse_list.append(lse)
            mix = (o_list, lse_list, p["c_wo"][j])
        final_g = p["final_g"] if i == depth - 1 else None
        x = _ffn(x, p["norm_ffn_g"][i], p["ffn_w1"][i], p["ffn_w2"][i], mix=mix, final_g=final_g)
    return x


def kernel(x_prompt, x_sample, rel_bias, norm_mix_g, norm_ffn_g, final_g, ffn_w1, ffn_w2, a_wqkv,
           a_sink, a_wo, b_win, b_ln_g, b_ln_b, b_ws, b_bs, b_wo, c_wqkv, c_wo):
    bf16 = jnp.bfloat16
    f32 = jnp.float32
    S = x_prompt.shape[1]
    assert x_sample.shape[1] == S
    p = {
        "norm_mix_g": norm_mix_g.astype(f32), "norm_ffn_g": norm_ffn_g.astype(f32),
        "final_g": final_g.astype(f32),
        "ffn_w1": ffn_w1.astype(bf16), "ffn_w2": ffn_w2.astype(bf16),
        "a_wqkv": a_wqkv.astype(bf16), "a_sink": a_sink, "a_wo": a_wo.astype(bf16),
        "b_win": b_win.astype(bf16), "b_ln_g": b_ln_g, "b_ln_b": b_ln_b,
        "b_ws": b_ws.astype(bf16), "b_bs": b_bs, "b_wo": b_wo.astype(bf16),
        "c_wqkv": c_wqkv.astype(bf16), "c_wo": c_wo.astype(bf16),
        "a_bias": _band_bias_table(rel_bias, ATTN_TQ, A_RADIUS, 1, S),
        "c_bias": [_band_bias_table(rel_bias, ATTN_TQ, window // (2 * dil), dil, S // dil)
                   for window, dil in C_GROUPS],
    }
    return _trunk(x_prompt, p), _trunk(x_sample, p)
```

```python
import functools

import numpy as np
import jax
import jax.numpy as jnp
from jax import lax
from jax.experimental import pallas as pl
from jax.experimental.pallas import tpu as pltpu

HEAD_DIM = 64
N_MIXERS = 3
N_HEADS = 16
N_KV_HEADS = 4
A_RADIUS = 128
B_CHUNK = 128
B_GROUPS = 8
C_GROUPS = ((128, 1), (512, 4), (2048, 16))
NUM_BUCKETS = 32
REL_MAX_DISTANCE = 1024
RMS_EPS = 1e-6
LN_EPS = 1e-5

Q_COLS = N_HEADS * HEAD_DIM
KV_COLS = N_KV_HEADS * HEAD_DIM
QKV_COLS = Q_COLS + 2 * KV_COLS
LANES = 128
HEADS_PER_TILE = LANES // HEAD_DIM
Q_PER_KV = N_HEADS // N_KV_HEADS
MASK_VALUE = -1e30

V7X_VMEM_BYTES = 64 * 1024 * 1024
VMEM_LIMIT_BYTES = V7X_VMEM_BYTES - 8 * 1024 * 1024

ATTN_TQ = 128
FFN_TM = 512
FFN_CHUNK = 1024
PROJ_TM = 512
GMLP_TM = 256


def _const_spec(shape):
    zeros = (0,) * len(shape)
    return pl.BlockSpec(shape, lambda *_: zeros, pipeline_mode=pl.Buffered(1))


def _params(n_grid_dims):
    return pltpu.CompilerParams(
        dimension_semantics=("arbitrary",) * n_grid_dims,
        vmem_limit_bytes=VMEM_LIMIT_BYTES,
    )


def _rmsnorm_f32(x, g):
    ms = jnp.mean(x * x, axis=-1, keepdims=True)
    return x * lax.rsqrt(ms + RMS_EPS) * g


def _lse_lane(head):
    return head // HEADS_PER_TILE + HEAD_DIM * (head % HEADS_PER_TILE)


def _proj_kernel(*refs, dils):
    n = len(dils)
    x_ref, g_ref = refs[:2]
    w_refs = refs[2:2 + n]
    o_refs = refs[2 + n:2 + 2 * n]
    hn = _rmsnorm_f32(x_ref[...], g_ref[...])
    tm, d_model = hn.shape
    n_slabs = d_model // LANES
    if any(d > 1 for d in dils):
        slab_ref = refs[2 + 2 * n]
        for j in range(n_slabs):
            slab_ref[j] = hn[:, j * LANES:(j + 1) * LANES]
    for dil, w_ref, o_ref in zip(dils, w_refs, o_refs):
        n_rows = tm // dil
        if dil == 1:
            h = hn
        else:
            h = jnp.concatenate(
                [jnp.concatenate([slab_ref[j, pl.ds(r, n_rows, stride=dil), :] for j in range(n_slabs)], axis=1)
                 for r in range(dil)], axis=0)
        y = jnp.dot(h.astype(jnp.bfloat16), w_ref[...], preferred_element_type=jnp.float32)
        q = (y[:, :Q_COLS] * (HEAD_DIM ** -0.5)).astype(o_ref.dtype)
        kv = y[:, Q_COLS:].astype(o_ref.dtype)
        for r in range(dil):
            o_ref[r, :, :Q_COLS] = q[r * n_rows:(r + 1) * n_rows]
            o_ref[r, :, Q_COLS:] = kv[r * n_rows:(r + 1) * n_rows]


def _qkv_projection(x, g, weights, dils):
    B, S, D = x.shape
    tm = PROJ_TM
    n = len(dils)
    in_specs = [pl.BlockSpec((None, tm, D), lambda b, i: (b, i, 0)), _const_spec((1, D))]
    in_specs += [_const_spec((D, QKV_COLS))] * n
    out_specs = [pl.BlockSpec((None, dil, tm // dil, QKV_COLS), lambda b, i: (b, 0, i, 0)) for dil in dils]
    out_shape = [jax.ShapeDtypeStruct((B, dil, S // dil, QKV_COLS), jnp.bfloat16) for dil in dils]
    scratch = []
    if any(d > 1 for d in dils):
        scratch.append(pltpu.VMEM((D // LANES, tm, LANES), jnp.float32))
    return pl.pallas_call(
        functools.partial(_proj_kernel, dils=tuple(dils)),
        grid=(B, S // tm),
        in_specs=in_specs,
        out_specs=out_specs,
        out_shape=out_shape,
        scratch_shapes=scratch,
        compiler_params=_params(2),
        name="qkv_proj_" + "_".join(f"d{d}" for d in dils),
    )(x, g.reshape(1, D), *weights)


def _rel_bucket(rel):
    half = NUM_BUCKETS // 2
    max_exact = half // 2
    n = np.abs(rel)
    large = max_exact + (np.log(np.maximum(n, 1) / max_exact) / np.log(REL_MAX_DISTANCE / max_exact)
                         * (half - max_exact)).astype(np.int32)
    large = np.minimum(large, half - 1)
    return (rel > 0).astype(np.int32) * half + np.where(n < max_exact, n, large)


def _dot_heads(g, v):
    return g * Q_PER_KV + v, g * Q_PER_KV + HEADS_PER_TILE + v


def _band_bias_table(rel_bias, tq, radius, dil, seq_len):
    width = tq + 2 * radius
    n_blk = seq_len // tq
    col = np.arange(width)[None, :]
    rel = col - radius - np.arange(tq)[:, None]
    n_diag = width + tq - 1
    diag_rel = (np.arange(n_diag) - (tq - 1) - radius) * dil
    per_diag = jnp.transpose(rel_bias[_rel_bucket(diag_rel)], (1, 0)).astype(jnp.float32)
    padded = jnp.pad(per_diag, ((0, 0), (0, 1)))
    tiled = jnp.broadcast_to(padded[:, None, :], (N_HEADS, tq, n_diag + 1)).reshape(N_HEADS, -1)
    skewed = tiled[:, :tq * n_diag].reshape(N_HEADS, tq, n_diag)
    per_head = skewed[:, :, tq - 1:]
    order = [h for g in range(N_KV_HEADS) for v in range(HEADS_PER_TILE) for h in _dot_heads(g, v)]
    stacked = per_head[np.asarray(order)].reshape(N_HEADS * tq, width)
    in_band = np.broadcast_to(np.abs(rel) <= radius, (N_HEADS, tq, width)).reshape(N_HEADS * tq, width)
    before = col < radius
    after = col >= width - radius
    if n_blk == 1:
        edge_masks = [before | after]
    else:
        edge_masks = [before, np.zeros_like(before), after]
    tables = [jnp.where(jnp.asarray(in_band & ~e), stacked, MASK_VALUE) for e in edge_masks]
    return jnp.stack(tables, axis=0)


def _attn_kernel(*refs, radius, seq_len, dil, has_sink, want_lse):
    refs = list(refs)
    q_ref, k_ref, v_ref, bias_ref = refs[:4]
    pos = 4
    sink_ref = None
    if has_sink:
        sink_ref = refs[pos]
        pos += 1
    o_ref = refs[pos]
    pos += 1
    lse_ref = None
    if want_lse:
        lse_ref = refs[pos]
        pos += 1
    kext_ref, vext_ref = refs[pos:pos + 2]

    tq = q_ref.shape[0]
    width = tq + 2 * radius
    n_blk = seq_len // tq
    i = pl.program_id(2)
    first_step = (pl.program_id(0) == 0) & (pl.program_id(1) == 0) & (i == 0)
    bf16 = jnp.bfloat16
    if dil > 1:
        token_rows = pl.ds(i * (tq * dil) + pl.program_id(1), tq, stride=dil)

    @pl.when(first_step)
    def _():
        kext_ref[...] = jnp.zeros(kext_ref.shape, bf16)
        rows = vext_ref.shape[0]
        zeros = jnp.zeros((rows, HEAD_DIM), bf16)
        ones = jnp.ones((rows, HEAD_DIM), bf16)
        for t in range(N_KV_HEADS * HEADS_PER_TILE):
            v = t % HEADS_PER_TILE
            tile = [zeros, zeros, ones, zeros] if v == 0 else [zeros, zeros, zeros, ones]
            vext_ref[:, t * 2 * LANES:(t + 1) * 2 * LANES] = jnp.concatenate(tile, axis=1)

    @pl.when(i == 0)
    def _():
        zeros = jnp.zeros((seq_len, HEAD_DIM), bf16)
        body = slice(radius, radius + seq_len)
        for g in range(N_KV_HEADS):
            kg = k_ref[:, g * HEAD_DIM:(g + 1) * HEAD_DIM]
            vg = v_ref[:, g * HEAD_DIM:(g + 1) * HEAD_DIM]
            for v in range(HEADS_PER_TILE):
                t = g * HEADS_PER_TILE + v
                pair = [kg, zeros] if v == 0 else [zeros, kg]
                kext_ref[body, t * LANES:(t + 1) * LANES] = jnp.concatenate(pair, axis=1)
                pair = [vg, zeros] if v == 0 else [zeros, vg]
                vext_ref[body, t * 2 * LANES:t * 2 * LANES + LANES] = jnp.concatenate(pair, axis=1)

    q_start = pl.multiple_of(i * tq, tq)
    if n_blk == 1:
        variant = 0
    else:
        variant = jnp.where(i == 0, 0, jnp.where(i == n_blk - 1, 2, 1))

    lane = lax.broadcasted_iota(jnp.int32, (2 * tq, LANES), 1)
    low_half = lane < HEAD_DIM
    top_rows = lax.broadcasted_iota(jnp.int32, (2 * tq, 1), 0) < tq
    if want_lse:
        lane_tq = lax.broadcasted_iota(jnp.int32, (tq, LANES), 1)
        lse_acc = jnp.zeros((tq, LANES), jnp.float32)

    def scores(g):
        base = g * Q_PER_KV * HEAD_DIM
        q2 = jnp.concatenate([q_ref[:, base:base + LANES], q_ref[:, base + LANES:base + 2 * LANES]], axis=0)
        k_win = jnp.concatenate(
            [kext_ref[pl.ds(q_start, width), t * LANES:(t + 1) * LANES]
             for t in range(g * HEADS_PER_TILE, (g + 1) * HEADS_PER_TILE)], axis=0)
        return lax.dot_general(q2, k_win, (((1,), (1,)), ((), ())), preferred_element_type=jnp.float32)

    s_next = scores(0)
    for g in range(N_KV_HEADS):
        base = g * Q_PER_KV * HEAD_DIM
        s_all = s_next
        if g + 1 < N_KV_HEADS:
            s_next = scores(g + 1)
        probs = []
        maxes = []
        sink_terms = []
        for v in range(HEADS_PER_TILE):
            t = g * HEADS_PER_TILE + v
            s = s_all[:, v * width:(v + 1) * width] + bias_ref[variant, t * 2 * tq:(t + 1) * 2 * tq, :]
            m = jnp.max(s, axis=-1, keepdims=True)
            if has_sink:
                h_top, h_bot = _dot_heads(g, v)
                sk = jnp.where(top_rows, sink_ref[h_top], sink_ref[h_bot])
                m = jnp.maximum(m, sk)
                sink_terms.append(jnp.exp(sk - m))
            maxes.append(m)
            probs.append(jnp.exp(s - m).astype(bf16))
        v_win = jnp.concatenate(
            [vext_ref[pl.ds(q_start, width), t * 2 * LANES:(t + 1) * 2 * LANES]
             for t in range(g * HEADS_PER_TILE, (g + 1) * HEADS_PER_TILE)], axis=0)
        acc = jnp.dot(jnp.concatenate(probs, axis=1), v_win, preferred_element_type=jnp.float32)
        denom = acc[:, LANES:]
        if has_sink:
            denom = denom + jnp.where(low_half, sink_terms[0], sink_terms[1])
        o = acc[:, :LANES] / denom
        if dil == 1:
            o_ref[:, base:base + LANES] = o[:tq].astype(o_ref.dtype)
            o_ref[:, base + LANES:base + 2 * LANES] = o[tq:].astype(o_ref.dtype)
        else:
            o_ref[g * HEADS_PER_TILE, token_rows, :] = o[:tq]
            o_ref[g * HEADS_PER_TILE + 1, token_rows, :] = o[tq:]
        if want_lse:
            lse = jnp.where(low_half, maxes[0], maxes[1]) + jnp.log(denom)
            for half, rows in ((0, slice(0, tq)), (1, slice(tq, 2 * tq))):
                tile = g * HEADS_PER_TILE + half
                here = (lane_tq == tile) | (lane_tq == HEAD_DIM + tile)
                lse_acc = jnp.where(here, lse[rows], lse_acc)
    if want_lse:
        if dil == 1:
            lse_ref[...] = lse_acc
        else:
            lse_ref[token_rows, :] = lse_acc


def _band_attention(qkv, bias, radius, *, sink=None, want_lse=False):
    B, dil, L, _ = qkv.shape
    S = L * dil
    tq = ATTN_TQ
    D = Q_COLS
    n_tiles = N_KV_HEADS * HEADS_PER_TILE

    in_specs = [
        pl.BlockSpec((None, None, tq, Q_COLS), lambda b, r, i: (b, r, i, 0)),
        pl.BlockSpec((None, None, L, KV_COLS), lambda b, r, i: (b, r, 0, Q_COLS // KV_COLS)),
        pl.BlockSpec((None, None, L, KV_COLS), lambda b, r, i: (b, r, 0, Q_COLS // KV_COLS + 1)),
        _const_spec(bias.shape),
    ]
    args = [qkv, qkv, qkv, bias]
    if sink is not None:
        in_specs.append(pl.BlockSpec(memory_space=pltpu.SMEM))
        args.append(sink.astype(jnp.float32))
    if dil == 1:
        out_specs = [pl.BlockSpec((None, tq, D), lambda b, r, i: (b, i, 0))]
        out_shape = [jax.ShapeDtypeStruct((B, S, D), jnp.bfloat16)]
        lse_spec = pl.BlockSpec((None, tq, LANES), lambda b, r, i: (b, i, 0))
    else:
        assert want_lse
        out_specs = [pl.BlockSpec((None, D // LANES, S, LANES), lambda b, r, i: (b, 0, 0, 0))]
        out_shape = [jax.ShapeDtypeStruct((B, D // LANES, S, LANES), jnp.float32)]
        lse_spec = pl.BlockSpec((None, S, LANES), lambda b, r, i: (b, 0, 0))
    if want_lse:
        out_specs.append(lse_spec)
        out_shape.append(jax.ShapeDtypeStruct((B, S, LANES), jnp.float32))
    scratch = [pltpu.VMEM((L + 2 * radius, n_tiles * LANES), jnp.bfloat16),
               pltpu.VMEM((L + 2 * radius, n_tiles * 2 * LANES), jnp.bfloat16)]

    kernel = functools.partial(_attn_kernel, radius=radius, seq_len=L, dil=dil,
                               has_sink=sink is not None, want_lse=want_lse)
    out = pl.pallas_call(
        kernel,
        grid=(B, dil, L // tq),
        in_specs=in_specs,
        out_specs=out_specs,
        out_shape=out_shape,
        scratch_shapes=scratch,
        compiler_params=_params(3),
        name=f"band_attn_r{radius}_d{dil}",
    )(*args)
    if want_lse:
        return out[0], out[1]
    return out[0]


def _head_expand_matrix():
    e = np.zeros((LANES, Q_COLS), np.float32)
    for h in range(N_HEADS):
        e[_lse_lane(h), h * HEAD_DIM:(h + 1) * HEAD_DIM] = 1.0
    return jnp.asarray(e, jnp.bfloat16)


def _read_rows_f32(o_ref):
    if len(o_ref.shape) == 2:
        return o_ref[...].astype(jnp.float32)
    return jnp.concatenate([o_ref[j] for j in range(o_ref.shape[0])], axis=1)


def _ffn_kernel(*refs, n_mix, final):
    refs = list(refs)
    x_ref = refs[0]
    pos = 1
    x = x_ref[...]
    if n_mix >= 1:
        o_refs = refs[pos:pos + n_mix]
        pos += n_mix
        if n_mix > 1:
            lse_refs = refs[pos:pos + n_mix]
            expand_ref = refs[pos + n_mix]
            pos += n_mix + 1
        wo_ref = refs[pos]
        pos += 1
        if n_mix == 1:
            o = o_refs[0][...]
        else:
            lses = [r[...] for r in lse_refs]
            top = functools.reduce(jnp.maximum, lses)
            es = [jnp.exp(l - top) for l in lses]
            inv_tot = 1.0 / functools.reduce(lambda a, b: a + b, es)
            o = None
            for e, o_ref in zip(es, o_refs):
                w = e * inv_tot
                w_hi = w.astype(jnp.bfloat16)
                w_lo = (w - w_hi.astype(jnp.float32)).astype(jnp.bfloat16)
                w_full = (jnp.dot(w_hi, expand_ref[...], preferred_element_type=jnp.float32)
                          + jnp.dot(w_lo, expand_ref[...], preferred_element_type=jnp.float32))
                term = w_full * _read_rows_f32(o_ref)
                o = term if o is None else o + term
            o = o.astype(jnp.bfloat16)
        x = x + jnp.dot(o, wo_ref[...], preferred_element_type=jnp.float32)
    g_ref, w1_ref, w2_ref = refs[pos:pos + 3]
    pos += 3
    if final:
        fg_ref = refs[pos]
        pos += 1
    out_ref = refs[pos]

    h = _rmsnorm_f32(x, g_ref[...]).astype(jnp.bfloat16)
    acc = x
    d_ff = w1_ref.shape[1]
    for c in range(d_ff // FFN_CHUNK):
        cols = slice(c * FFN_CHUNK, (c + 1) * FFN_CHUNK)
        a = jnp.dot(h, w1_ref[:, cols], preferred_element_type=jnp.float32)
        a = jnp.square(jnp.maximum(a, 0.0)).astype(jnp.bfloat16)
        acc = acc + jnp.dot(a, w2_ref[cols, :], preferred_element_type=jnp.float32)
    if final:
        acc = _rmsnorm_f32(acc, fg_ref[...])
    out_ref[...] = acc


def _ffn(x, g, w1, w2, *, mix=None, final_g=None):
    B, S, D = x.shape
    d_ff = w1.shape[1]
    tm = FFN_TM
    row_spec = lambda c: pl.BlockSpec((None, tm, c), lambda b, i: (b, i, 0))
    slab_spec = lambda c: pl.BlockSpec((None, c // LANES, tm, LANES), lambda b, i: (b, 0, i, 0))
    in_specs = [row_spec(D)]
    args = [x]
    n_mix = 0
    if mix is not None:
        o_list, lse_list, w_o = mix
        n_mix = len(o_list)
        in_specs += [row_spec(D) if o.ndim == 3 else slab_spec(D) for o in o_list]
        args += o_list
        if n_mix > 1:
            in_specs += [row_spec(LANES)] * n_mix + [_const_spec((LANES, Q_COLS))]
            args += lse_list + [_head_expand_matrix()]
        in_specs.append(_const_spec((D, D)))
        args.append(w_o)
    in_specs += [_const_spec((1, D)), _const_spec((D, d_ff)), _const_spec((d_ff, D))]
    args += [g.reshape(1, D), w1, w2]
    if final_g is not None:
        in_specs.append(_const_spec((1, D)))
        args.append(final_g.reshape(1, D))
    out = pl.pallas_call(
        functools.partial(_ffn_kernel, n_mix=n_mix, final=final_g is not None),
        grid=(B, S // tm),
        in_specs=in_specs,
        out_specs=row_spec(D),
        out_shape=jax.ShapeDtypeStruct((B, S, D), jnp.float32),
        compiler_params=_params(2),
        name=f"ffn_mix{n_mix}" + ("_final" if final_g is not None else ""),
    )(*args)
    return out


def _gmlp_kernel(x_ref, g_ref, win_ref, lng_ref, lnb_ref, ws_ref, bs_ref, wout_ref, o_ref):
    x = x_ref[...]
    tm = x.shape[0]
    hidden = wout_ref.shape[0]
    gcols = hidden // B_GROUPS
    h = _rmsnorm_f32(x, g_ref[...]).astype(jnp.bfloat16)
    z = jnp.dot(h, win_ref[...], preferred_element_type=jnp.float32)
    z = 0.5 * z * (1.0 + lax.erf(z * (2.0 ** -0.5)))
    u = z[:, :hidden]
    v = z[:, hidden:]
    mu = jnp.mean(v, axis=-1, keepdims=True)
    vc = v - mu
    var = jnp.mean(vc * vc, axis=-1, keepdims=True)
    vn = (vc * lax.rsqrt(var + LN_EPS) * lng_ref[...] + lnb_ref[...]).astype(jnp.bfloat16)
    rows = []
    for c in range(tm // B_CHUNK):
        r0 = c * B_CHUNK
        parts = []
        for grp in range(B_GROUPS):
            parts.append(jnp.dot(ws_ref[grp], vn[r0:r0 + B_CHUNK, grp * gcols:(grp + 1) * gcols],
                                 preferred_element_type=jnp.float32))
        mixed = jnp.concatenate(parts, axis=-1) + bs_ref[...]
        rows.append((u[r0:r0 + B_CHUNK, :] * mixed).astype(jnp.bfloat16))
    t = jnp.concatenate(rows, axis=0)
    o_ref[...] = x + jnp.dot(t, wout_ref[...], preferred_element_type=jnp.float32)


def _gmlp(x, g, w_in, ln_g, ln_b, w_s, b_s, w_out):
    B, S, D = x.shape
    hidden = w_out.shape[0]
    T = B * S
    tm = GMLP_TM
    bs_full = jnp.repeat(jnp.transpose(b_s).astype(jnp.float32), hidden // B_GROUPS, axis=1)
    row_spec = pl.BlockSpec((tm, D), lambda i: (i, 0))
    out = pl.pallas_call(
        _gmlp_kernel,
        grid=(T // tm,),
        in_specs=[row_spec, _const_spec((1, D)), _const_spec((D, 2 * hidden)),
                  _const_spec((1, hidden)), _const_spec((1, hidden)),
                  _const_spec((B_GROUPS, B_CHUNK, B_CHUNK)), _const_spec((B_CHUNK, hidden)),
                  _const_spec((hidden, D))],
        out_specs=row_spec,
        out_shape=jax.ShapeDtypeStruct((T, D), jnp.float32),
        compiler_params=_params(1),
        name="gmlp",
    )(x.reshape(T, D), g.reshape(1, D), w_in, ln_g.reshape(1, hidden).astype(jnp.float32),
      ln_b.reshape(1, hidden).astype(jnp.float32), w_s, bs_full, w_out)
    return out.reshape(B, S, D)


def _trunk(x, p):
    depth = p["norm_mix_g"].shape[0]
    S = x.shape[1]
    for i in range(depth):
        kind, j = i % N_MIXERS, i // N_MIXERS
        g_mix = p["norm_mix_g"][i]
        mix = None
        if kind == 0:
            (qkv,) = _qkv_projection(x, g_mix, [p["a_wqkv"][j]], [1])
            o = _band_attention(qkv, p["a_bias"], A_RADIUS, sink=p["a_sink"][j])
            mix = ([o], None, p["a_wo"][j])
        elif kind == 1:
            x = _gmlp(x, g_mix, p["b_win"][j], p["b_ln_g"][j], p["b_ln_b"][j], p["b_ws"][j],
                      p["b_bs"][j], p["b_wo"][j])
        else:
            o_list, lse_list = [], []
            dils = [dil for _, dil in C_GROUPS]
            weights = [p["c_wqkv"][j][:, gi * QKV_COLS:(gi + 1) * QKV_COLS] for gi in range(len(dils))]
            qkvs = _qkv_projection(x, g_mix, weights, dils)
            for gi, (window, dil) in enumerate(C_GROUPS):
                o, lse = _band_attention(qkvs[gi], p["c_bias"][gi], window // (2 * dil), want_lse=True)
                o_list.append(o)
                lse_list.append(lse)
            mix = (o_list, lse_list, p["c_wo"][j])
        final_g = p["final_g"] if i == depth - 1 else None
        x = _ffn(x, p["norm_ffn_g"][i], p["ffn_w1"][i], p["ffn_w2"][i], mix=mix, final_g=final_g)
    return x


def kernel(x_prompt, x_sample, rel_bias, norm_mix_g, norm_ffn_g, final_g, ffn_w1, ffn_w2, a_wqkv,
           a_sink, a_wo, b_win, b_ln_g, b_ln_b, b_ws, b_bs, b_wo, c_wqkv, c_wo):
    bf16 = jnp.bfloat16
    f32 = jnp.float32
    S = x_prompt.shape[1]
    assert x_sample.shape[1] == S
    p = {
        "norm_mix_g": norm_mix_g.astype(f32), "norm_ffn_g": norm_ffn_g.astype(f32),
        "final_g": final_g.astype(f32),
        "ffn_w1": ffn_w1.astype(bf16), "ffn_w2": ffn_w2.astype(bf16),
        "a_wqkv": a_wqkv.astype(bf16), "a_sink": a_sink, "a_wo": a_wo.astype(bf16),
        "b_win": b_win.astype(bf16), "b_ln_g": b_ln_g, "b_ln_b": b_ln_b,
        "b_ws": b_ws.astype(bf16), "b_bs": b_bs, "b_wo": b_wo.astype(bf16),
        "c_wqkv": c_wqkv.astype(bf16), "c_wo": c_wo.astype(bf16),
        "a_bias": _band_bias_table(rel_bias, ATTN_TQ, A_RADIUS, 1, S),
        "c_bias": [_band_bias_table(rel_bias, ATTN_TQ, window // (2 * dil), dil, S // dil)
                   for window, dil in C_GROUPS],
    }
    return _trunk(x_prompt, p), _trunk(x_sample, p)
```

```python
import functools

import numpy as np
import jax
import jax.numpy as jnp
from jax import lax
from jax.experimental import pallas as pl
from jax.experimental.pallas import tpu as pltpu

HEAD_DIM = 64
N_MIXERS = 3
N_HEADS = 16
N_KV_HEADS = 4
A_RADIUS = 128
B_CHUNK = 128
B_GROUPS = 8
C_GROUPS = ((128, 1), (512, 4), (2048, 16))
NUM_BUCKETS = 32
REL_MAX_DISTANCE = 1024
RMS_EPS = 1e-6
LN_EPS = 1e-5

Q_COLS = N_HEADS * HEAD_DIM
KV_COLS = N_KV_HEADS * HEAD_DIM
QKV_COLS = Q_COLS + 2 * KV_COLS
LANES = 128
HEADS_PER_TILE = LANES // HEAD_DIM
Q_PER_KV = N_HEADS // N_KV_HEADS
MASK_VALUE = -1e30

V7X_VMEM_BYTES = 64 * 1024 * 1024
VMEM_LIMIT_BYTES = V7X_VMEM_BYTES - 8 * 1024 * 1024

ATTN_TQ = 128
ATTN_STEP_ROWS = 256
FFN_TM = 512
FFN_CHUNK = 1024
PROJ_TM = 512
GMLP_TM = 256


def _const_spec(shape):
    zeros = (0,) * len(shape)
    return pl.BlockSpec(shape, lambda *_: zeros, pipeline_mode=pl.Buffered(1))


def _params(n_grid_dims):
    return pltpu.CompilerParams(
        dimension_semantics=("arbitrary",) * n_grid_dims,
        vmem_limit_bytes=VMEM_LIMIT_BYTES,
    )


def _rmsnorm_f32(x, g):
    ms = jnp.mean(x * x, axis=-1, keepdims=True)
    return x * lax.rsqrt(ms + RMS_EPS) * g


def _lse_lane(head):
    return head // HEADS_PER_TILE + HEAD_DIM * (head % HEADS_PER_TILE)


def _proj_kernel(*refs, dils):
    n = len(dils)
    x_ref, g_ref = refs[:2]
    w_refs = refs[2:2 + n]
    o_refs = refs[2 + n:2 + 2 * n]
    hn = _rmsnorm_f32(x_ref[...], g_ref[...])
    tm, d_model = hn.shape
    n_slabs = d_model // LANES
    if any(d > 1 for d in dils):
        slab_ref = refs[2 + 2 * n]
        for j in range(n_slabs):
            slab_ref[j] = hn[:, j * LANES:(j + 1) * LANES]
    for dil, w_ref, o_ref in zip(dils, w_refs, o_refs):
        n_rows = tm // dil
        if dil == 1:
            h = hn
        else:
            h = jnp.concatenate(
                [jnp.concatenate([slab_ref[j, pl.ds(r, n_rows, stride=dil), :] for j in range(n_slabs)], axis=1)
                 for r in range(dil)], axis=0)
        y = jnp.dot(h.astype(jnp.bfloat16), w_ref[...], preferred_element_type=jnp.float32)
        q = (y[:, :Q_COLS] * (HEAD_DIM ** -0.5)).astype(o_ref.dtype)
        kv = y[:, Q_COLS:].astype(o_ref.dtype)
        for r in range(dil):
            o_ref[r, :, :Q_COLS] = q[r * n_rows:(r + 1) * n_rows]
            o_ref[r, :, Q_COLS:] = kv[r * n_rows:(r + 1) * n_rows]


def _qkv_projection(x, g, weights, dils):
    B, S, D = x.shape
    tm = PROJ_TM
    n = len(dils)
    in_specs = [pl.BlockSpec((None, tm, D), lambda b, i: (b, i, 0)), _const_spec((1, D))]
    in_specs += [_const_spec((D, QKV_COLS))] * n
    out_specs = [pl.BlockSpec((None, dil, tm // dil, QKV_COLS), lambda b, i: (b, 0, i, 0)) for dil in dils]
    out_shape = [jax.ShapeDtypeStruct((B, dil, S // dil, QKV_COLS), jnp.bfloat16) for dil in dils]
    scratch = []
    if any(d > 1 for d in dils):
        scratch.append(pltpu.VMEM((D // LANES, tm, LANES), jnp.float32))
    return pl.pallas_call(
        functools.partial(_proj_kernel, dils=tuple(dils)),
        grid=(B, S // tm),
        in_specs=in_specs,
        out_specs=out_specs,
        out_shape=out_shape,
        scratch_shapes=scratch,
        compiler_params=_params(2),
        name="qkv_proj_" + "_".join(f"d{d}" for d in dils),
    )(x, g.reshape(1, D), *weights)


def _rel_bucket(rel):
    half = NUM_BUCKETS // 2
    max_exact = half // 2
    n = np.abs(rel)
    large = max_exact + (np.log(np.maximum(n, 1) / max_exact) / np.log(REL_MAX_DISTANCE / max_exact)
                         * (half - max_exact)).astype(np.int32)
    large = np.minimum(large, half - 1)
    return (rel > 0).astype(np.int32) * half + np.where(n < max_exact, n, large)


def _dot_heads(g, v):
    return g * Q_PER_KV + v, g * Q_PER_KV + HEADS_PER_TILE + v


def _band_bias_table(rel_bias, tq, radius, dil, seq_len):
    width = tq + 2 * radius
    n_blk = seq_len // tq
    col = np.arange(width)[None, :]
    rel = col - radius - np.arange(tq)[:, None]
    n_diag = width + tq - 1
    diag_rel = (np.arange(n_diag) - (tq - 1) - radius) * dil
    per_diag = jnp.transpose(rel_bias[_rel_bucket(diag_rel)], (1, 0)).astype(jnp.float32)
    padded = jnp.pad(per_diag, ((0, 0), (0, 1)))
    tiled = jnp.broadcast_to(padded[:, None, :], (N_HEADS, tq, n_diag + 1)).reshape(N_HEADS, -1)
    skewed = tiled[:, :tq * n_diag].reshape(N_HEADS, tq, n_diag)
    per_head = skewed[:, :, tq - 1:]
    order = [h for g in range(N_KV_HEADS) for v in range(HEADS_PER_TILE) for h in _dot_heads(g, v)]
    stacked = per_head[np.asarray(order)].reshape(N_HEADS * tq, width)
    in_band = np.broadcast_to(np.abs(rel) <= radius, (N_HEADS, tq, width)).reshape(N_HEADS * tq, width)
    before = col < radius
    after = col >= width - radius
    if n_blk == 1:
        edge_masks = [before | after]
    else:
        edge_masks = [before, np.zeros_like(before), after]
    tables = [jnp.where(jnp.asarray(in_band & ~e), stacked, MASK_VALUE) for e in edge_masks]
    return jnp.stack(tables, axis=0)


def _attn_kernel(*refs, tq, radius, seq_len, dil, has_sink, want_lse):
    refs = list(refs)
    q_ref, k_ref, v_ref, bias_ref = refs[:4]
    pos = 4
    sink_ref = None
    if has_sink:
        sink_ref = refs[pos]
        pos += 1
    o_ref = refs[pos]
    pos += 1
    lse_ref = None
    if want_lse:
        lse_ref = refs[pos]
        pos += 1
    kext_ref, vext_ref = refs[pos:pos + 2]

    blocks = q_ref.shape[0] // tq
    width = tq + 2 * radius
    n_blk = seq_len // tq
    i = pl.program_id(2)
    first_step = (pl.program_id(0) == 0) & (pl.program_id(1) == 0) & (i == 0)
    bf16 = jnp.bfloat16

    @pl.when(first_step)
    def _():
        kext_ref[...] = jnp.zeros(kext_ref.shape, bf16)
        rows = vext_ref.shape[0]
        zeros = jnp.zeros((rows, HEAD_DIM), bf16)
        ones = jnp.ones((rows, HEAD_DIM), bf16)
        for t in range(N_KV_HEADS * HEADS_PER_TILE):
            v = t % HEADS_PER_TILE
            tile = [zeros, zeros, ones, zeros] if v == 0 else [zeros, zeros, zeros, ones]
            vext_ref[:, t * 2 * LANES:(t + 1) * 2 * LANES] = jnp.concatenate(tile, axis=1)

    @pl.when(i == 0)
    def _():
        zeros = jnp.zeros((seq_len, HEAD_DIM), bf16)
        body = slice(radius, radius + seq_len)
        for g in range(N_KV_HEADS):
            kg = k_ref[:, g * HEAD_DIM:(g + 1) * HEAD_DIM]
            vg = v_ref[:, g * HEAD_DIM:(g + 1) * HEAD_DIM]
            for v in range(HEADS_PER_TILE):
                t = g * HEADS_PER_TILE + v
                pair = [kg, zeros] if v == 0 else [zeros, kg]
                kext_ref[body, t * LANES:(t + 1) * LANES] = jnp.concatenate(pair, axis=1)
                pair = [vg, zeros] if v == 0 else [zeros, vg]
                vext_ref[body, t * 2 * LANES:t * 2 * LANES + LANES] = jnp.concatenate(pair, axis=1)

    lane = lax.broadcasted_iota(jnp.int32, (2 * tq, LANES), 1)
    low_half = lane < HEAD_DIM
    top_rows = lax.broadcasted_iota(jnp.int32, (2 * tq, 1), 0) < tq
    lane_tq = lax.broadcasted_iota(jnp.int32, (tq, LANES), 1)

    def key_start(j):
        return pl.multiple_of((i * blocks + j) * tq, tq)

    def scores(j, g):
        base = g * Q_PER_KV * HEAD_DIM
        rows = slice(j * tq, (j + 1) * tq)
        q2 = jnp.concatenate([q_ref[rows, base:base + LANES], q_ref[rows, base + LANES:base + 2 * LANES]],
                             axis=0)
        k_win = jnp.concatenate(
            [kext_ref[pl.ds(key_start(j), width), t * LANES:(t + 1) * LANES]
             for t in range(g * HEADS_PER_TILE, (g + 1) * HEADS_PER_TILE)], axis=0)
        return lax.dot_general(q2, k_win, (((1,), (1,)), ((), ())), preferred_element_type=jnp.float32)

    work = [(j, g) for j in range(blocks) for g in range(N_KV_HEADS)]
    s_next = scores(*work[0])
    for n, (j, g) in enumerate(work):
        base = g * Q_PER_KV * HEAD_DIM
        rows = slice(j * tq, (j + 1) * tq)
        blk = i * blocks + j
        if n_blk == 1:
            variant = 0
        else:
            variant = jnp.where(blk == 0, 0, jnp.where(blk == n_blk - 1, 2, 1))
        if dil > 1:
            token_rows = pl.ds(blk * (tq * dil) + pl.program_id(1), tq, stride=dil)
        s_all = s_next
        if n + 1 < len(work):
            s_next = scores(*work[n + 1])
        probs = []
        maxes = []
        sink_terms = []
        for v in range(HEADS_PER_TILE):
            t = g * HEADS_PER_TILE + v
            s = s_all[:, v * width:(v + 1) * width] + bias_ref[variant, t * 2 * tq:(t + 1) * 2 * tq, :]
            m = jnp.max(s, axis=-1, keepdims=True)
            if has_sink:
                h_top, h_bot = _dot_heads(g, v)
                sk = jnp.where(top_rows, sink_ref[h_top], sink_ref[h_bot])
                m = jnp.maximum(m, sk)
                sink_terms.append(jnp.exp(sk - m))
            maxes.append(m)
            probs.append(jnp.exp(s - m).astype(bf16))
        v_win = jnp.concatenate(
            [vext_ref[pl.ds(key_start(j), width), t * 2 * LANES:(t + 1) * 2 * LANES]
             for t in range(g * HEADS_PER_TILE, (g + 1) * HEADS_PER_TILE)], axis=0)
        acc = jnp.dot(jnp.concatenate(probs, axis=1), v_win, preferred_element_type=jnp.float32)
        denom = acc[:, LANES:]
        if has_sink:
            denom = denom + jnp.where(low_half, sink_terms[0], sink_terms[1])
        o = acc[:, :LANES] / denom
        if dil == 1:
            o_ref[rows, base:base + LANES] = o[:tq].astype(o_ref.dtype)
            o_ref[rows, base + LANES:base + 2 * LANES] = o[tq:].astype(o_ref.dtype)
        else:
            o_ref[g * HEADS_PER_TILE, token_rows, :] = o[:tq]
            o_ref[g * HEADS_PER_TILE + 1, token_rows, :] = o[tq:]
        if want_lse:
            if g == 0:
                lse_acc = jnp.zeros((tq, LANES), jnp.float32)
            lse = jnp.where(low_half, maxes[0], maxes[1]) + jnp.log(denom)
            for half, half_rows in ((0, slice(0, tq)), (1, slice(tq, 2 * tq))):
                tile = g * HEADS_PER_TILE + half
                here = (lane_tq == tile) | (lane_tq == HEAD_DIM + tile)
                lse_acc = jnp.where(here, lse[half_rows], lse_acc)
            if g == N_KV_HEADS - 1:
                if dil == 1:
                    lse_ref[rows, :] = lse_acc
                else:
                    lse_ref[token_rows, :] = lse_acc


def _band_attention(qkv, bias, radius, *, sink=None, want_lse=False):
    B, dil, L, _ = qkv.shape
    S = L * dil
    tq = ATTN_TQ
    ts = min(L, ATTN_STEP_ROWS)
    D = Q_COLS
    n_tiles = N_KV_HEADS * HEADS_PER_TILE

    in_specs = [
        pl.BlockSpec((None, None, ts, Q_COLS), lambda b, r, i: (b, r, i, 0)),
        pl.BlockSpec((None, None, L, KV_COLS), lambda b, r, i: (b, r, 0, Q_COLS // KV_COLS)),
        pl.BlockSpec((None, None, L, KV_COLS), lambda b, r, i: (b, r, 0, Q_COLS // KV_COLS + 1)),
        _const_spec(bias.shape),
    ]
    args = [qkv, qkv, qkv, bias]
    if sink is not None:
        in_specs.append(pl.BlockSpec(memory_space=pltpu.SMEM))
        args.append(sink.astype(jnp.float32))
    if dil == 1:
        out_specs = [pl.BlockSpec((None, ts, D), lambda b, r, i: (b, i, 0))]
        out_shape = [jax.ShapeDtypeStruct((B, S, D), jnp.bfloat16)]
        lse_spec = pl.BlockSpec((None, ts, LANES), lambda b, r, i: (b, i, 0))
    else:
        assert want_lse
        out_specs = [pl.BlockSpec((None, D // LANES, S, LANES), lambda b, r, i: (b, 0, 0, 0))]
        out_shape = [jax.ShapeDtypeStruct((B, D // LANES, S, LANES), jnp.float32)]
        lse_spec = pl.BlockSpec((None, S, LANES), lambda b, r, i: (b, 0, 0))
    if want_lse:
        out_specs.append(lse_spec)
        out_shape.append(jax.ShapeDtypeStruct((B, S, LANES), jnp.float32))
    scratch = [pltpu.VMEM((L + 2 * radius, n_tiles * LANES), jnp.bfloat16),
               pltpu.VMEM((L + 2 * radius, n_tiles * 2 * LANES), jnp.bfloat16)]

    kernel = functools.partial(_attn_kernel, tq=tq, radius=radius, seq_len=L, dil=dil,
                               has_sink=sink is not None, want_lse=want_lse)
    out = pl.pallas_call(
        kernel,
        grid=(B, dil, L // ts),
        in_specs=in_specs,
        out_specs=out_specs,
        out_shape=out_shape,
        scratch_shapes=scratch,
        compiler_params=_params(3),
        name=f"band_attn_r{radius}_d{dil}",
    )(*args)
    if want_lse:
        return out[0], out[1]
    return out[0]


def _head_expand_matrix():
    e = np.zeros((LANES, Q_COLS), np.float32)
    for h in range(N_HEADS):
        e[_lse_lane(h), h * HEAD_DIM:(h + 1) * HEAD_DIM] = 1.0
    return jnp.asarray(e, jnp.bfloat16)


def _read_rows_f32(o_ref):
    if len(o_ref.shape) == 2:
        return o_ref[...].astype(jnp.float32)
    return jnp.concatenate([o_ref[j] for j in range(o_ref.shape[0])], axis=1)


def _ffn_kernel(*refs, n_mix, final):
    refs = list(refs)
    x_ref = refs[0]
    pos = 1
    x = x_ref[...]
    if n_mix >= 1:
        o_refs = refs[pos:pos + n_mix]
        pos += n_mix
        if n_mix > 1:
            lse_refs = refs[pos:pos + n_mix]
            expand_ref = refs[pos + n_mix]
            pos += n_mix + 1
        wo_ref = refs[pos]
        pos += 1
        if n_mix == 1:
            o = o_refs[0][...]
        else:
            lses = [r[...] for r in lse_refs]
            top = functools.reduce(jnp.maximum, lses)
            es = [jnp.exp(l - top) for l in lses]
            inv_tot = 1.0 / functools.reduce(lambda a, b: a + b, es)
            o = None
            for e, o_ref in zip(es, o_refs):
                w = e * inv_tot
                w_hi = w.astype(jnp.bfloat16)
                w_lo = (w - w_hi.astype(jnp.float32)).astype(jnp.bfloat16)
                w_full = (jnp.dot(w_hi, expand_ref[...], preferred_element_type=jnp.float32)
                          + jnp.dot(w_lo, expand_ref[...], preferred_element_type=jnp.float32))
                term = w_full * _read_rows_f32(o_ref)
                o = term if o is None else o + term
            o = o.astype(jnp.bfloat16)
        x = x + jnp.dot(o, wo_ref[...], preferred_element_type=jnp.float32)
    g_ref, w1_ref, w2_ref = refs[pos:pos + 3]
    pos += 3
    if final:
        fg_ref = refs[pos]
        pos += 1
    out_ref = refs[pos]

    h = _rmsnorm_f32(x, g_ref[...]).astype(jnp.bfloat16)
    acc = x
    d_ff = w1_ref.shape[1]
    for c in range(d_ff // FFN_CHUNK):
        cols = slice(c * FFN_CHUNK, (c + 1) * FFN_CHUNK)
        a = jnp.dot(h, w1_ref[:, cols], preferred_element_type=jnp.float32)
        a = jnp.square(jnp.maximum(a, 0.0)).astype(jnp.bfloat16)
        acc = acc + jnp.dot(a, w2_ref[cols, :], preferred_element_type=jnp.float32)
    if final:
        acc = _rmsnorm_f32(acc, fg_ref[...])
    out_ref[...] = acc


def _ffn(x, g, w1, w2, *, mix=None, final_g=None):
    B, S, D = x.shape
    d_ff = w1.shape[1]
    tm = FFN_TM
    row_spec = lambda c: pl.BlockSpec((None, tm, c), lambda b, i: (b, i, 0))
    slab_spec = lambda c: pl.BlockSpec((None, c // LANES, tm, LANES), lambda b, i: (b, 0, i, 0))
    in_specs = [row_spec(D)]
    args = [x]
    n_mix = 0
    if mix is not None:
        o_list, lse_list, w_o = mix
        n_mix = len(o_list)
        in_specs += [row_spec(D) if o.ndim == 3 else slab_spec(D) for o in o_list]
        args += o_list
        if n_mix > 1:
            in_specs += [row_spec(LANES)] * n_mix + [_const_spec((LANES, Q_COLS))]
            args += lse_list + [_head_expand_matrix()]
        in_specs.append(_const_spec((D, D)))
        args.append(w_o)
    in_specs += [_const_spec((1, D)), _const_spec((D, d_ff)), _const_spec((d_ff, D))]
    args += [g.reshape(1, D), w1, w2]
    if final_g is not None:
        in_specs.append(_const_spec((1, D)))
        args.append(final_g.reshape(1, D))
    out = pl.pallas_call(
        functools.partial(_ffn_kernel, n_mix=n_mix, final=final_g is not None),
        grid=(B, S // tm),
        in_specs=in_specs,
        out_specs=row_spec(D),
        out_shape=jax.ShapeDtypeStruct((B, S, D), jnp.float32),
        compiler_params=_params(2),
        name=f"ffn_mix{n_mix}" + ("_final" if final_g is not None else ""),
    )(*args)
    return out


def _gmlp_kernel(x_ref, g_ref, win_ref, lng_ref, lnb_ref, ws_ref, bs_ref, wout_ref, o_ref):
    x = x_ref[...]
    tm = x.shape[0]
    hidden = wout_ref.shape[0]
    gcols = hidden // B_GROUPS
    h = _rmsnorm_f32(x, g_ref[...]).astype(jnp.bfloat16)
    z = jnp.dot(h, win_ref[...], preferred_element_type=jnp.float32)
    z = 0.5 * z * (1.0 + lax.erf(z * (2.0 ** -0.5)))
    u = z[:, :hidden]
    v = z[:, hidden:]
    mu = jnp.mean(v, axis=-1, keepdims=True)
    vc = v - mu
    var = jnp.mean(vc * vc, axis=-1, keepdims=True)
    vn = (vc * lax.rsqrt(var + LN_EPS) * lng_ref[...] + lnb_ref[...]).astype(jnp.bfloat16)
    rows = []
    for c in range(tm // B_CHUNK):
        r0 = c * B_CHUNK
        parts = []
        for grp in range(B_GROUPS):
            parts.append(jnp.dot(ws_ref[grp], vn[r0:r0 + B_CHUNK, grp * gcols:(grp + 1) * gcols],
                                 preferred_element_type=jnp.float32))
        mixed = jnp.concatenate(parts, axis=-1) + bs_ref[...]
        rows.append((u[r0:r0 + B_CHUNK, :] * mixed).astype(jnp.bfloat16))
    t = jnp.concatenate(rows, axis=0)
    o_ref[...] = x + jnp.dot(t, wout_ref[...], preferred_element_type=jnp.float32)


def _gmlp(x, g, w_in, ln_g, ln_b, w_s, b_s, w_out):
    B, S, D = x.shape
    hidden = w_out.shape[0]
    T = B * S
    tm = GMLP_TM
    bs_full = jnp.repeat(jnp.transpose(b_s).astype(jnp.float32), hidden // B_GROUPS, axis=1)
    row_spec = pl.BlockSpec((tm, D), lambda i: (i, 0))
    out = pl.pallas_call(
        _gmlp_kernel,
        grid=(T // tm,),
        in_specs=[row_spec, _const_spec((1, D)), _const_spec((D, 2 * hidden)),
                  _const_spec((1, hidden)), _const_spec((1, hidden)),
                  _const_spec((B_GROUPS, B_CHUNK, B_CHUNK)), _const_spec((B_CHUNK, hidden)),
                  _const_spec((hidden, D))],
        out_specs=row_spec,
        out_shape=jax.ShapeDtypeStruct((T, D), jnp.float32),
        compiler_params=_params(1),
        name="gmlp",
    )(x.reshape(T, D), g.reshape(1, D), w_in, ln_g.reshape(1, hidden).astype(jnp.float32),
      ln_b.reshape(1, hidden).astype(jnp.float32), w_s, bs_full, w_out)
    return out.reshape(B, S, D)


def _trunk(x, p):
    depth = p["norm_mix_g"].shape[0]
    S = x.shape[1]
    for i in range(depth):
        kind, j = i % N_MIXERS, i // N_MIXERS
        g_mix = p["norm_mix_g"][i]
        mix = None
        if kind == 0:
            (qkv,) = _qkv_projection(x, g_mix, [p["a_wqkv"][j]], [1])
            o = _band_attention(qkv, p["a_bias"], A_RADIUS, sink=p["a_sink"][j])
            mix = ([o], None, p["a_wo"][j])
        elif kind == 1:
            x = _gmlp(x, g_mix, p["b_win"][j], p["b_ln_g"][j], p["b_ln_b"][j], p["b_ws"][j],
                      p["b_bs"][j], p["b_wo"][j])
        else:
            o_list, lse_list = [], []
            dils = [dil for _, dil in C_GROUPS]
            weights = [p["c_wqkv"][j][:, gi * QKV_COLS:(gi + 1) * QKV_COLS] for gi in range(len(dils))]
            qkvs = _qkv_projection(x, g_mix, weights, dils)
            for gi, (window, dil) in enumerate(C_GROUPS):
                o, lse = _band_attention(qkvs[gi], p["c_bias"][gi], window // (2 * dil), want_lse=True)
                o_list.append(o)
                lse_list.append(lse)
            mix = (o_list, lse_list, p["c_wo"][j])
        final_g = p["final_g"] if i == depth - 1 else None
        x = _ffn(x, p["norm_ffn_g"][i], p["ffn_w1"][i], p["ffn_w2"][i], mix=mix, final_g=final_g)
    return x


def kernel(x_prompt, x_sample, rel_bias, norm_mix_g, norm_ffn_g, final_g, ffn_w1, ffn_w2, a_wqkv,
           a_sink, a_wo, b_win, b_ln_g, b_ln_b, b_ws, b_bs, b_wo, c_wqkv, c_wo):
    bf16 = jnp.bfloat16
    f32 = jnp.float32
    S = x_prompt.shape[1]
    assert x_sample.shape[1] == S
    p = {
        "norm_mix_g": norm_mix_g.astype(f32), "norm_ffn_g": norm_ffn_g.astype(f32),
        "final_g": final_g.astype(f32),
        "ffn_w1": ffn_w1.astype(bf16), "ffn_w2": ffn_w2.astype(bf16),
        "a_wqkv": a_wqkv.astype(bf16), "a_sink": a_sink, "a_wo": a_wo.astype(bf16),
        "b_win": b_win.astype(bf16), "b_ln_g": b_ln_g, "b_ln_b": b_ln_b,
        "b_ws": b_ws.astype(bf16), "b_bs": b_bs, "b_wo": b_wo.astype(bf16),
        "c_wqkv": c_wqkv.astype(bf16), "c_wo": c_wo.astype(bf16),
        "a_bias": _band_bias_table(rel_bias, ATTN_TQ, A_RADIUS, 1, S),
        "c_bias": [_band_bias_table(rel_bias, ATTN_TQ, window // (2 * dil), dil, S // dil)
                   for window, dil in C_GROUPS],
    }
    return _trunk(x_prompt, p), _trunk(x_sample, p)
```

```python
import functools

import numpy as np
import jax
import jax.numpy as jnp
from jax import lax
from jax.experimental import pallas as pl
from jax.experimental.pallas import tpu as pltpu

HEAD_DIM = 64
N_MIXERS = 3
N_HEADS = 16
N_KV_HEADS = 4
A_RADIUS = 128
B_CHUNK = 128
B_GROUPS = 8
C_GROUPS = ((128, 1), (512, 4), (2048, 16))
NUM_BUCKETS = 32
REL_MAX_DISTANCE = 1024
RMS_EPS = 1e-6
LN_EPS = 1e-5

Q_COLS = N_HEADS * HEAD_DIM
KV_COLS = N_KV_HEADS * HEAD_DIM
QKV_COLS = Q_COLS + 2 * KV_COLS
LANES = 128
HEADS_PER_TILE = LANES // HEAD_DIM
Q_PER_KV = N_HEADS // N_KV_HEADS
MASK_VALUE = -1e30

V7X_VMEM_BYTES = 64 * 1024 * 1024
VMEM_LIMIT_BYTES = V7X_VMEM_BYTES - 8 * 1024 * 1024

ATTN_TQ = 128
ATTN_STEP_ROWS = 512
FFN_TM = 512
FFN_CHUNK = 1024
PROJ_TM = 512
GMLP_TM = 256


def _const_spec(shape):
    zeros = (0,) * len(shape)
    return pl.BlockSpec(shape, lambda *_: zeros, pipeline_mode=pl.Buffered(1))


def _params(n_grid_dims):
    return pltpu.CompilerParams(
        dimension_semantics=("arbitrary",) * n_grid_dims,
        vmem_limit_bytes=VMEM_LIMIT_BYTES,
    )


def _rmsnorm_f32(x, g):
    ms = jnp.mean(x * x, axis=-1, keepdims=True)
    return x * lax.rsqrt(ms + RMS_EPS) * g


def _lse_lane(head):
    return head // HEADS_PER_TILE + HEAD_DIM * (head % HEADS_PER_TILE)


def _proj_kernel(*refs, dils):
    n = len(dils)
    x_ref, g_ref = refs[:2]
    w_refs = refs[2:2 + n]
    o_refs = refs[2 + n:2 + 2 * n]
    hn = _rmsnorm_f32(x_ref[...], g_ref[...])
    tm, d_model = hn.shape
    n_slabs = d_model // LANES
    if any(d > 1 for d in dils):
        slab_ref = refs[2 + 2 * n]
        for j in range(n_slabs):
            slab_ref[j] = hn[:, j * LANES:(j + 1) * LANES]
    for dil, w_ref, o_ref in zip(dils, w_refs, o_refs):
        n_rows = tm // dil
        if dil == 1:
            h = hn
        else:
            h = jnp.concatenate(
                [jnp.concatenate([slab_ref[j, pl.ds(r, n_rows, stride=dil), :] for j in range(n_slabs)], axis=1)
                 for r in range(dil)], axis=0)
        y = jnp.dot(h.astype(jnp.bfloat16), w_ref[...], preferred_element_type=jnp.float32)
        q = (y[:, :Q_COLS] * (HEAD_DIM ** -0.5)).astype(o_ref.dtype)
        kv = y[:, Q_COLS:].astype(o_ref.dtype)
        for r in range(dil):
            o_ref[r, :, :Q_COLS] = q[r * n_rows:(r + 1) * n_rows]
            o_ref[r, :, Q_COLS:] = kv[r * n_rows:(r + 1) * n_rows]


def _qkv_projection(x, g, weights, dils):
    B, S, D = x.shape
    tm = PROJ_TM
    n = len(dils)
    in_specs = [pl.BlockSpec((None, tm, D), lambda b, i: (b, i, 0)), _const_spec((1, D))]
    in_specs += [_const_spec((D, QKV_COLS))] * n
    out_specs = [pl.BlockSpec((None, dil, tm // dil, QKV_COLS), lambda b, i: (b, 0, i, 0)) for dil in dils]
    out_shape = [jax.ShapeDtypeStruct((B, dil, S // dil, QKV_COLS), jnp.bfloat16) for dil in dils]
    scratch = []
    if any(d > 1 for d in dils):
        scratch.append(pltpu.VMEM((D // LANES, tm, LANES), jnp.float32))
    return pl.pallas_call(
        functools.partial(_proj_kernel, dils=tuple(dils)),
        grid=(B, S // tm),
        in_specs=in_specs,
        out_specs=out_specs,
        out_shape=out_shape,
        scratch_shapes=scratch,
        compiler_params=_params(2),
        name="qkv_proj_" + "_".join(f"d{d}" for d in dils),
    )(x, g.reshape(1, D), *weights)


def _rel_bucket(rel):
    half = NUM_BUCKETS // 2
    max_exact = half // 2
    n = np.abs(rel)
    large = max_exact + (np.log(np.maximum(n, 1) / max_exact) / np.log(REL_MAX_DISTANCE / max_exact)
                         * (half - max_exact)).astype(np.int32)
    large = np.minimum(large, half - 1)
    return (rel > 0).astype(np.int32) * half + np.where(n < max_exact, n, large)


def _dot_heads(g, v):
    return g * Q_PER_KV + v, g * Q_PER_KV + HEADS_PER_TILE + v


def _band_bias_table(rel_bias, tq, radius, dil, seq_len):
    width = tq + 2 * radius
    n_blk = seq_len // tq
    col = np.arange(width)[None, :]
    rel = col - radius - np.arange(tq)[:, None]
    n_diag = width + tq - 1
    diag_rel = (np.arange(n_diag) - (tq - 1) - radius) * dil
    per_diag = jnp.transpose(rel_bias[_rel_bucket(diag_rel)], (1, 0)).astype(jnp.float32)
    padded = jnp.pad(per_diag, ((0, 0), (0, 1)))
    tiled = jnp.broadcast_to(padded[:, None, :], (N_HEADS, tq, n_diag + 1)).reshape(N_HEADS, -1)
    skewed = tiled[:, :tq * n_diag].reshape(N_HEADS, tq, n_diag)
    per_head = skewed[:, :, tq - 1:]
    order = [h for g in range(N_KV_HEADS) for v in range(HEADS_PER_TILE) for h in _dot_heads(g, v)]
    stacked = per_head[np.asarray(order)].reshape(N_HEADS * tq, width)
    in_band = np.broadcast_to(np.abs(rel) <= radius, (N_HEADS, tq, width)).reshape(N_HEADS * tq, width)
    before = col < radius
    after = col >= width - radius
    if n_blk == 1:
        edge_masks = [before | after]
    else:
        edge_masks = [before, np.zeros_like(before), after]
    tables = [jnp.where(jnp.asarray(in_band & ~e), stacked, MASK_VALUE) for e in edge_masks]
    return jnp.stack(tables, axis=0)


def _attn_kernel(*refs, tq, radius, seq_len, dil, has_sink, want_lse):
    refs = list(refs)
    q_ref, k_ref, v_ref, bias_ref = refs[:4]
    pos = 4
    sink_ref = None
    if has_sink:
        sink_ref = refs[pos]
        pos += 1
    o_ref = refs[pos]
    pos += 1
    lse_ref = None
    if want_lse:
        lse_ref = refs[pos]
        pos += 1
    kext_ref, vext_ref = refs[pos:pos + 2]

    n_res = q_ref.shape[0]
    blocks = q_ref.shape[1] // tq
    width = tq + 2 * radius
    n_blk = seq_len // tq
    i = pl.program_id(2)
    first_step = (pl.program_id(0) == 0) & (pl.program_id(1) == 0) & (i == 0)
    bf16 = jnp.bfloat16

    @pl.when(first_step)
    def _():
        kext_ref[...] = jnp.zeros(kext_ref.shape, bf16)
        rows = vext_ref.shape[1]
        zeros = jnp.zeros((rows, HEAD_DIM), bf16)
        ones = jnp.ones((rows, HEAD_DIM), bf16)
        for res in range(n_res):
            for t in range(N_KV_HEADS * HEADS_PER_TILE):
                v = t % HEADS_PER_TILE
                tile = [zeros, zeros, ones, zeros] if v == 0 else [zeros, zeros, zeros, ones]
                vext_ref[res, :, t * 2 * LANES:(t + 1) * 2 * LANES] = jnp.concatenate(tile, axis=1)

    @pl.when(i == 0)
    def _():
        zeros = jnp.zeros((seq_len, HEAD_DIM), bf16)
        body = slice(radius, radius + seq_len)
        for res in range(n_res):
            for g in range(N_KV_HEADS):
                kg = k_ref[res, :, g * HEAD_DIM:(g + 1) * HEAD_DIM]
                vg = v_ref[res, :, g * HEAD_DIM:(g + 1) * HEAD_DIM]
                for v in range(HEADS_PER_TILE):
                    t = g * HEADS_PER_TILE + v
                    pair = [kg, zeros] if v == 0 else [zeros, kg]
                    kext_ref[res, body, t * LANES:(t + 1) * LANES] = jnp.concatenate(pair, axis=1)
                    pair = [vg, zeros] if v == 0 else [zeros, vg]
                    vext_ref[res, body, t * 2 * LANES:t * 2 * LANES + LANES] = jnp.concatenate(pair, axis=1)

    lane = lax.broadcasted_iota(jnp.int32, (2 * tq, LANES), 1)
    low_half = lane < HEAD_DIM
    top_rows = lax.broadcasted_iota(jnp.int32, (2 * tq, 1), 0) < tq
    lane_tq = lax.broadcasted_iota(jnp.int32, (tq, LANES), 1)

    def key_start(j):
        return pl.multiple_of((i * blocks + j) * tq, tq)

    def scores(res, j, g):
        base = g * Q_PER_KV * HEAD_DIM
        rows = slice(j * tq, (j + 1) * tq)
        q2 = jnp.concatenate([q_ref[res, rows, base:base + LANES],
                              q_ref[res, rows, base + LANES:base + 2 * LANES]], axis=0)
        k_win = jnp.concatenate(
            [kext_ref[res, pl.ds(key_start(j), width), t * LANES:(t + 1) * LANES]
             for t in range(g * HEADS_PER_TILE, (g + 1) * HEADS_PER_TILE)], axis=0)
        return lax.dot_general(q2, k_win, (((1,), (1,)), ((), ())), preferred_element_type=jnp.float32)

    work = [(res, j, g) for res in range(n_res) for j in range(blocks) for g in range(N_KV_HEADS)]
    s_next = scores(*work[0])
    for n, (res, j, g) in enumerate(work):
        base = g * Q_PER_KV * HEAD_DIM
        rows = slice(j * tq, (j + 1) * tq)
        blk = i * blocks + j
        if n_blk == 1:
            variant = 0
        else:
            variant = jnp.where(blk == 0, 0, jnp.where(blk == n_blk - 1, 2, 1))
        if dil > 1:
            token_rows = pl.ds(blk * (tq * dil) + pl.program_id(1) * n_res + res, tq, stride=dil)
        s_all = s_next
        if n + 1 < len(work):
            s_next = scores(*work[n + 1])
        probs = []
        maxes = []
        sink_terms = []
        for v in range(HEADS_PER_TILE):
            t = g * HEADS_PER_TILE + v
            s = s_all[:, v * width:(v + 1) * width] + bias_ref[variant, t * 2 * tq:(t + 1) * 2 * tq, :]
            m = jnp.max(s, axis=-1, keepdims=True)
            if has_sink:
                h_top, h_bot = _dot_heads(g, v)
                sk = jnp.where(top_rows, sink_ref[h_top], sink_ref[h_bot])
                m = jnp.maximum(m, sk)
                sink_terms.append(jnp.exp(sk - m))
            maxes.append(m)
            probs.append(jnp.exp(s - m).astype(bf16))
        v_win = jnp.concatenate(
            [vext_ref[res, pl.ds(key_start(j), width), t * 2 * LANES:(t + 1) * 2 * LANES]
             for t in range(g * HEADS_PER_TILE, (g + 1) * HEADS_PER_TILE)], axis=0)
        acc = jnp.dot(jnp.concatenate(probs, axis=1), v_win, preferred_element_type=jnp.float32)
        denom = acc[:, LANES:]
        if has_sink:
            denom = denom + jnp.where(low_half, sink_terms[0], sink_terms[1])
        o = acc[:, :LANES] / denom
        if dil == 1:
            o_ref[rows, base:base + LANES] = o[:tq].astype(o_ref.dtype)
            o_ref[rows, base + LANES:base + 2 * LANES] = o[tq:].astype(o_ref.dtype)
        else:
            o_ref[g * HEADS_PER_TILE, token_rows, :] = o[:tq]
            o_ref[g * HEADS_PER_TILE + 1, token_rows, :] = o[tq:]
        if want_lse:
            if g == 0:
                lse_acc = jnp.zeros((tq, LANES), jnp.float32)
            lse = jnp.where(low_half, maxes[0], maxes[1]) + jnp.log(denom)
            for half, half_rows in ((0, slice(0, tq)), (1, slice(tq, 2 * tq))):
                tile = g * HEADS_PER_TILE + half
                here = (lane_tq == tile) | (lane_tq == HEAD_DIM + tile)
                lse_acc = jnp.where(here, lse[half_rows], lse_acc)
            if g == N_KV_HEADS - 1:
                if dil == 1:
                    lse_ref[rows, :] = lse_acc
                else:
                    lse_ref[token_rows, :] = lse_acc


def _band_attention(qkv, bias, radius, *, sink=None, want_lse=False):
    B, dil, L, _ = qkv.shape
    S = L * dil
    tq = ATTN_TQ
    ts = min(L, ATTN_STEP_ROWS)
    n_res = min(dil, ATTN_STEP_ROWS // ts)
    D = Q_COLS
    n_tiles = N_KV_HEADS * HEADS_PER_TILE

    in_specs = [
        pl.BlockSpec((None, n_res, ts, Q_COLS), lambda b, r, i: (b, r, i, 0)),
        pl.BlockSpec((None, n_res, L, KV_COLS), lambda b, r, i: (b, r, 0, Q_COLS // KV_COLS)),
        pl.BlockSpec((None, n_res, L, KV_COLS), lambda b, r, i: (b, r, 0, Q_COLS // KV_COLS + 1)),
        _const_spec(bias.shape),
    ]
    args = [qkv, qkv, qkv, bias]
    if sink is not None:
        in_specs.append(pl.BlockSpec(memory_space=pltpu.SMEM))
        args.append(sink.astype(jnp.float32))
    if dil == 1:
        out_specs = [pl.BlockSpec((None, ts, D), lambda b, r, i: (b, i, 0))]
        out_shape = [jax.ShapeDtypeStruct((B, S, D), jnp.bfloat16)]
        lse_spec = pl.BlockSpec((None, ts, LANES), lambda b, r, i: (b, i, 0))
    else:
        assert want_lse
        out_specs = [pl.BlockSpec((None, D // LANES, S, LANES), lambda b, r, i: (b, 0, 0, 0))]
        out_shape = [jax.ShapeDtypeStruct((B, D // LANES, S, LANES), jnp.float32)]
        lse_spec = pl.BlockSpec((None, S, LANES), lambda b, r, i: (b, 0, 0))
    if want_lse:
        out_specs.append(lse_spec)
        out_shape.append(jax.ShapeDtypeStruct((B, S, LANES), jnp.float32))
    scratch = [pltpu.VMEM((n_res, L + 2 * radius, n_tiles * LANES), jnp.bfloat16),
               pltpu.VMEM((n_res, L + 2 * radius, n_tiles * 2 * LANES), jnp.bfloat16)]

    kernel = functools.partial(_attn_kernel, tq=tq, radius=radius, seq_len=L, dil=dil,
                               has_sink=sink is not None, want_lse=want_lse)
    out = pl.pallas_call(
        kernel,
        grid=(B, dil // n_res, L // ts),
        in_specs=in_specs,
        out_specs=out_specs,
        out_shape=out_shape,
        scratch_shapes=scratch,
        compiler_params=_params(3),
        name=f"band_attn_r{radius}_d{dil}",
    )(*args)
    if want_lse:
        return out[0], out[1]
    return out[0]


def _head_expand_matrix():
    e = np.zeros((LANES, Q_COLS), np.float32)
    for h in range(N_HEADS):
        e[_lse_lane(h), h * HEAD_DIM:(h + 1) * HEAD_DIM] = 1.0
    return jnp.asarray(e, jnp.bfloat16)


def _read_rows_f32(o_ref):
    if len(o_ref.shape) == 2:
        return o_ref[...].astype(jnp.float32)
    return jnp.concatenate([o_ref[j] for j in range(o_ref.shape[0])], axis=1)


def _ffn_kernel(*refs, n_mix, final):
    refs = list(refs)
    x_ref = refs[0]
    pos = 1
    x = x_ref[...]
    if n_mix >= 1:
        o_refs = refs[pos:pos + n_mix]
        pos += n_mix
        if n_mix > 1:
            lse_refs = refs[pos:pos + n_mix]
            expand_ref = refs[pos + n_mix]
            pos += n_mix + 1
        wo_ref = refs[pos]
        pos += 1
        if n_mix == 1:
            o = o_refs[0][...]
        else:
            lses = [r[...] for r in lse_refs]
            top = functools.reduce(jnp.maximum, lses)
            es = [jnp.exp(l - top) for l in lses]
            inv_tot = 1.0 / functools.reduce(lambda a, b: a + b, es)
            o = None
            for e, o_ref in zip(es, o_refs):
                w = e * inv_tot
                w_hi = w.astype(jnp.bfloat16)
                w_lo = (w - w_hi.astype(jnp.float32)).astype(jnp.bfloat16)
                w_full = (jnp.dot(w_hi, expand_ref[...], preferred_element_type=jnp.float32)
                          + jnp.dot(w_lo, expand_ref[...], preferred_element_type=jnp.float32))
                term = w_full * _read_rows_f32(o_ref)
                o = term if o is None else o + term
            o = o.astype(jnp.bfloat16)
        x = x + jnp.dot(o, wo_ref[...], preferred_element_type=jnp.float32)
    g_ref, w1_ref, w2_ref = refs[pos:pos + 3]
    pos += 3
    if final:
        fg_ref = refs[pos]
        pos += 1
    out_ref = refs[pos]

    h = _rmsnorm_f32(x, g_ref[...]).astype(jnp.bfloat16)
    acc = x
    d_ff = w1_ref.shape[1]
    for c in range(d_ff // FFN_CHUNK):
        cols = slice(c * FFN_CHUNK, (c + 1) * FFN_CHUNK)
        a = jnp.dot(h, w1_ref[:, cols], preferred_element_type=jnp.float32)
        a = jnp.square(jnp.maximum(a, 0.0)).astype(jnp.bfloat16)
        acc = acc + jnp.dot(a, w2_ref[cols, :], preferred_element_type=jnp.float32)
    if final:
        acc = _rmsnorm_f32(acc, fg_ref[...])
    out_ref[...] = acc


def _ffn(x, g, w1, w2, *, mix=None, final_g=None):
    B, S, D = x.shape
    d_ff = w1.shape[1]
    tm = FFN_TM
    row_spec = lambda c: pl.BlockSpec((None, tm, c), lambda b, i: (b, i, 0))
    slab_spec = lambda c: pl.BlockSpec((None, c // LANES, tm, LANES), lambda b, i: (b, 0, i, 0))
    in_specs = [row_spec(D)]
    args = [x]
    n_mix = 0
    if mix is not None:
        o_list, lse_list, w_o = mix
        n_mix = len(o_list)
        in_specs += [row_spec(D) if o.ndim == 3 else slab_spec(D) for o in o_list]
        args += o_list
        if n_mix > 1:
            in_specs += [row_spec(LANES)] * n_mix + [_const_spec((LANES, Q_COLS))]
            args += lse_list + [_head_expand_matrix()]
        in_specs.append(_const_spec((D, D)))
        args.append(w_o)
    in_specs += [_const_spec((1, D)), _const_spec((D, d_ff)), _const_spec((d_ff, D))]
    args += [g.reshape(1, D), w1, w2]
    if final_g is not None:
        in_specs.append(_const_spec((1, D)))
        args.append(final_g.reshape(1, D))
    out = pl.pallas_call(
        functools.partial(_ffn_kernel, n_mix=n_mix, final=final_g is not None),
        grid=(B, S // tm),
        in_specs=in_specs,
        out_specs=row_spec(D),
        out_shape=jax.ShapeDtypeStruct((B, S, D), jnp.float32),
        compiler_params=_params(2),
        name=f"ffn_mix{n_mix}" + ("_final" if final_g is not None else ""),
    )(*args)
    return out


def _gmlp_kernel(x_ref, g_ref, win_ref, lng_ref, lnb_ref, ws_ref, bs_ref, wout_ref, o_ref):
    x = x_ref[...]
    tm = x.shape[0]
    hidden = wout_ref.shape[0]
    gcols = hidden // B_GROUPS
    h = _rmsnorm_f32(x, g_ref[...]).astype(jnp.bfloat16)
    z = jnp.dot(h, win_ref[...], preferred_element_type=jnp.float32)
    z = 0.5 * z * (1.0 + lax.erf(z * (2.0 ** -0.5)))
    u = z[:, :hidden]
    v = z[:, hidden:]
    mu = jnp.mean(v, axis=-1, keepdims=True)
    vc = v - mu
    var = jnp.mean(vc * vc, axis=-1, keepdims=True)
    vn = (vc * lax.rsqrt(var + LN_EPS) * lng_ref[...] + lnb_ref[...]).astype(jnp.bfloat16)
    rows = []
    for c in range(tm // B_CHUNK):
        r0 = c * B_CHUNK
        parts = []
        for grp in range(B_GROUPS):
            parts.append(jnp.dot(ws_ref[grp], vn[r0:r0 + B_CHUNK, grp * gcols:(grp + 1) * gcols],
                                 preferred_element_type=jnp.float32))
        mixed = jnp.concatenate(parts, axis=-1) + bs_ref[...]
        rows.append((u[r0:r0 + B_CHUNK, :] * mixed).astype(jnp.bfloat16))
    t = jnp.concatenate(rows, axis=0)
    o_ref[...] = x + jnp.dot(t, wout_ref[...], preferred_element_type=jnp.float32)


def _gmlp(x, g, w_in, ln_g, ln_b, w_s, b_s, w_out):
    B, S, D = x.shape
    hidden = w_out.shape[0]
    T = B * S
    tm = GMLP_TM
    bs_full = jnp.repeat(jnp.transpose(b_s).astype(jnp.float32), hidden // B_GROUPS, axis=1)
    row_spec = pl.BlockSpec((tm, D), lambda i: (i, 0))
    out = pl.pallas_call(
        _gmlp_kernel,
        grid=(T // tm,),
        in_specs=[row_spec, _const_spec((1, D)), _const_spec((D, 2 * hidden)),
                  _const_spec((1, hidden)), _const_spec((1, hidden)),
                  _const_spec((B_GROUPS, B_CHUNK, B_CHUNK)), _const_spec((B_CHUNK, hidden)),
                  _const_spec((hidden, D))],
        out_specs=row_spec,
        out_shape=jax.ShapeDtypeStruct((T, D), jnp.float32),
        compiler_params=_params(1),
        name="gmlp",
    )(x.reshape(T, D), g.reshape(1, D), w_in, ln_g.reshape(1, hidden).astype(jnp.float32),
      ln_b.reshape(1, hidden).astype(jnp.float32), w_s, bs_full, w_out)
    return out.reshape(B, S, D)


def _trunk(x, p):
    depth = p["norm_mix_g"].shape[0]
    S = x.shape[1]
    for i in range(depth):
        kind, j = i % N_MIXERS, i // N_MIXERS
        g_mix = p["norm_mix_g"][i]
        mix = None
        if kind == 0:
            (qkv,) = _qkv_projection(x, g_mix, [p["a_wqkv"][j]], [1])
            o = _band_attention(qkv, p["a_bias"], A_RADIUS, sink=p["a_sink"][j])
            mix = ([o], None, p["a_wo"][j])
        elif kind == 1:
            x = _gmlp(x, g_mix, p["b_win"][j], p["b_ln_g"][j], p["b_ln_b"][j], p["b_ws"][j],
                      p["b_bs"][j], p["b_wo"][j])
        else:
            o_list, lse_list = [], []
            dils = [dil for _, dil in C_GROUPS]
            weights = [p["c_wqkv"][j][:, gi * QKV_COLS:(gi + 1) * QKV_COLS] for gi in range(len(dils))]
            qkvs = _qkv_projection(x, g_mix, weights, dils)
            for gi, (window, dil) in enumerate(C_GROUPS):
                o, lse = _band_attention(qkvs[gi], p["c_bias"][gi], window // (2 * dil), want_lse=True)
                o_list.append(o)
                lse_list.append(lse)
            mix = (o_list, lse_list, p["c_wo"][j])
        final_g = p["final_g"] if i == depth - 1 else None
        x = _ffn(x, p["norm_ffn_g"][i], p["ffn_w1"][i], p["ffn_w2"][i], mix=mix, final_g=final_g)
    return x


def kernel(x_prompt, x_sample, rel_bias, norm_mix_g, norm_ffn_g, final_g, ffn_w1, ffn_w2, a_wqkv,
           a_sink, a_wo, b_win, b_ln_g, b_ln_b, b_ws, b_bs, b_wo, c_wqkv, c_wo):
    bf16 = jnp.bfloat16
    f32 = jnp.float32
    S = x_prompt.shape[1]
    assert x_sample.shape[1] == S
    p = {
        "norm_mix_g": norm_mix_g.astype(f32), "norm_ffn_g": norm_ffn_g.astype(f32),
        "final_g": final_g.astype(f32),
        "ffn_w1": ffn_w1.astype(bf16), "ffn_w2": ffn_w2.astype(bf16),
        "a_wqkv": a_wqkv.astype(bf16), "a_sink": a_sink, "a_wo": a_wo.astype(bf16),
        "b_win": b_win.astype(bf16), "b_ln_g": b_ln_g, "b_ln_b": b_ln_b,
        "b_ws": b_ws.astype(bf16), "b_bs": b_bs, "b_wo": b_wo.astype(bf16),
        "c_wqkv": c_wqkv.astype(bf16), "c_wo": c_wo.astype(bf16),
        "a_bias": _band_bias_table(rel_bias, ATTN_TQ, A_RADIUS, 1, S),
        "c_bias": [_band_bias_table(rel_bias, ATTN_TQ, window // (2 * dil), dil, S // dil)
                   for window, dil in C_GROUPS],
    }
    return _trunk(x_prompt, p), _trunk(x_sample, p)
```

```python
import functools

import numpy as np
import jax
import jax.numpy as jnp
from jax import lax
from jax.experimental import pallas as pl
from jax.experimental.pallas import tpu as pltpu

HEAD_DIM = 64
N_MIXERS = 3
N_HEADS = 16
N_KV_HEADS = 4
A_RADIUS = 128
B_CHUNK = 128
B_GROUPS = 8
C_GROUPS = ((128, 1), (512, 4), (2048, 16))
NUM_BUCKETS = 32
REL_MAX_DISTANCE = 1024
RMS_EPS = 1e-6
LN_EPS = 1e-5

Q_COLS = N_HEADS * HEAD_DIM
KV_COLS = N_KV_HEADS * HEAD_DIM
QKV_COLS = Q_COLS + 2 * KV_COLS
LANES = 128
HEADS_PER_TILE = LANES // HEAD_DIM
Q_PER_KV = N_HEADS // N_KV_HEADS
MASK_VALUE = -1e30

V7X_VMEM_BYTES = 64 * 1024 * 1024
VMEM_LIMIT_BYTES = V7X_VMEM_BYTES - 8 * 1024 * 1024

ATTN_TQ = 128
ATTN_STEP_ROWS = 512
FFN_TM = 512
FFN_CHUNK = 1024
PROJ_TM = 512
GMLP_TM = 256


def _const_spec(shape):
    zeros = (0,) * len(shape)
    return pl.BlockSpec(shape, lambda *_: zeros, pipeline_mode=pl.Buffered(1))


def _params(n_grid_dims):
    return pltpu.CompilerParams(
        dimension_semantics=("arbitrary",) * n_grid_dims,
        vmem_limit_bytes=VMEM_LIMIT_BYTES,
    )


def _rmsnorm_f32(x, g):
    ms = jnp.mean(x * x, axis=-1, keepdims=True)
    return x * lax.rsqrt(ms + RMS_EPS) * g


def _lse_lane(head):
    return head // HEADS_PER_TILE + HEAD_DIM * (head % HEADS_PER_TILE)


def _proj_kernel(*refs, dils):
    n = len(dils)
    x_ref, g_ref = refs[:2]
    w_refs = refs[2:2 + n]
    o_refs = refs[2 + n:2 + 2 * n]
    hn = _rmsnorm_f32(x_ref[...], g_ref[...])
    tm, d_model = hn.shape
    n_slabs = d_model // LANES
    if any(d > 1 for d in dils):
        slab_ref = refs[2 + 2 * n]
        for j in range(n_slabs):
            slab_ref[j] = hn[:, j * LANES:(j + 1) * LANES]
    for dil, w_ref, o_ref in zip(dils, w_refs, o_refs):
        n_rows = tm // dil
        if dil == 1:
            h = hn
        else:
            h = jnp.concatenate(
                [jnp.concatenate([slab_ref[j, pl.ds(r, n_rows, stride=dil), :] for j in range(n_slabs)], axis=1)
                 for r in range(dil)], axis=0)
        y = jnp.dot(h.astype(jnp.bfloat16), w_ref[...], preferred_element_type=jnp.float32)
        q = (y[:, :Q_COLS] * (HEAD_DIM ** -0.5)).astype(o_ref.dtype)
        kv = y[:, Q_COLS:].astype(o_ref.dtype)
        for r in range(dil):
            o_ref[r, :, :Q_COLS] = q[r * n_rows:(r + 1) * n_rows]
            o_ref[r, :, Q_COLS:] = kv[r * n_rows:(r + 1) * n_rows]


def _qkv_projection(x, g, weights, dils):
    B, S, D = x.shape
    tm = PROJ_TM
    n = len(dils)
    in_specs = [pl.BlockSpec((None, tm, D), lambda b, i: (b, i, 0)), _const_spec((1, D))]
    in_specs += [_const_spec((D, QKV_COLS))] * n
    out_specs = [pl.BlockSpec((None, dil, tm // dil, QKV_COLS), lambda b, i: (b, 0, i, 0)) for dil in dils]
    out_shape = [jax.ShapeDtypeStruct((B, dil, S // dil, QKV_COLS), jnp.bfloat16) for dil in dils]
    scratch = []
    if any(d > 1 for d in dils):
        scratch.append(pltpu.VMEM((D // LANES, tm, LANES), jnp.float32))
    return pl.pallas_call(
        functools.partial(_proj_kernel, dils=tuple(dils)),
        grid=(B, S // tm),
        in_specs=in_specs,
        out_specs=out_specs,
        out_shape=out_shape,
        scratch_shapes=scratch,
        compiler_params=_params(2),
        name="qkv_proj_" + "_".join(f"d{d}" for d in dils),
    )(x, g.reshape(1, D), *weights)


def _rel_bucket(rel):
    half = NUM_BUCKETS // 2
    max_exact = half // 2
    n = np.abs(rel)
    large = max_exact + (np.log(np.maximum(n, 1) / max_exact) / np.log(REL_MAX_DISTANCE / max_exact)
                         * (half - max_exact)).astype(np.int32)
    large = np.minimum(large, half - 1)
    return (rel > 0).astype(np.int32) * half + np.where(n < max_exact, n, large)


def _dot_heads(g, v):
    return g * Q_PER_KV + v, g * Q_PER_KV + HEADS_PER_TILE + v


def _band_bias_table(rel_bias, tq, radius, dil, seq_len):
    width = tq + 2 * radius
    n_blk = seq_len // tq
    col = np.arange(width)[None, :]
    rel = col - radius - np.arange(tq)[:, None]
    n_diag = width + tq - 1
    diag_rel = (np.arange(n_diag) - (tq - 1) - radius) * dil
    per_diag = jnp.transpose(rel_bias[_rel_bucket(diag_rel)], (1, 0)).astype(jnp.float32)
    padded = jnp.pad(per_diag, ((0, 0), (0, 1)))
    tiled = jnp.broadcast_to(padded[:, None, :], (N_HEADS, tq, n_diag + 1)).reshape(N_HEADS, -1)
    skewed = tiled[:, :tq * n_diag].reshape(N_HEADS, tq, n_diag)
    per_head = skewed[:, :, tq - 1:]
    order = [h for g in range(N_KV_HEADS) for v in range(HEADS_PER_TILE) for h in _dot_heads(g, v)]
    stacked = per_head[np.asarray(order)].reshape(N_HEADS * tq, width)
    in_band = np.broadcast_to(np.abs(rel) <= radius, (N_HEADS, tq, width)).reshape(N_HEADS * tq, width)
    before = col < radius
    after = col >= width - radius
    if n_blk == 1:
        edge_masks = [before | after]
    else:
        edge_masks = [before, np.zeros_like(before), after]
    tables = [jnp.where(jnp.asarray(in_band & ~e), stacked, MASK_VALUE) for e in edge_masks]
    return jnp.stack(tables, axis=0)


def _attn_kernel(*refs, tq, radius, seq_len, dil, has_sink, want_lse):
    refs = list(refs)
    q_ref, k_ref, v_ref, bias_ref, zero_ref = refs[:5]
    pos = 5
    sink_ref = None
    if has_sink:
        sink_ref = refs[pos]
        pos += 1
    o_ref = refs[pos]
    pos += 1
    lse_ref = None
    if want_lse:
        lse_ref = refs[pos]
        pos += 1
    kext_ref, vext_ref = refs[pos:pos + 2]

    n_res = q_ref.shape[0]
    blocks = q_ref.shape[1] // tq
    width = tq + 2 * radius
    n_blk = seq_len // tq
    i = pl.program_id(2)
    first_step = (pl.program_id(0) == 0) & (pl.program_id(1) == 0) & (i == 0)
    bf16 = jnp.bfloat16

    @pl.when(first_step)
    def _():
        kext_ref[...] = jnp.zeros(kext_ref.shape, bf16)
        rows = vext_ref.shape[1]
        zeros = jnp.zeros((rows, HEAD_DIM), bf16)
        ones = jnp.ones((rows, HEAD_DIM), bf16)
        for res in range(n_res):
            for t in range(N_KV_HEADS * HEADS_PER_TILE):
                v = t % HEADS_PER_TILE
                tile = [zeros, zeros, ones, zeros] if v == 0 else [zeros, zeros, zeros, ones]
                vext_ref[res, :, t * 2 * LANES:(t + 1) * 2 * LANES] = jnp.concatenate(tile, axis=1)

    @pl.when(i + zero_ref[0] == 0)
    def _():
        zeros = jnp.zeros((seq_len, HEAD_DIM), bf16)
        body = slice(radius, radius + seq_len)
        for res in range(n_res):
            for g in range(N_KV_HEADS):
                kg = k_ref[res, :, g * HEAD_DIM:(g + 1) * HEAD_DIM]
                vg = v_ref[res, :, g * HEAD_DIM:(g + 1) * HEAD_DIM]
                for v in range(HEADS_PER_TILE):
                    t = g * HEADS_PER_TILE + v
                    pair = [kg, zeros] if v == 0 else [zeros, kg]
                    kext_ref[res, body, t * LANES:(t + 1) * LANES] = jnp.concatenate(pair, axis=1)
                    pair = [vg, zeros] if v == 0 else [zeros, vg]
                    vext_ref[res, body, t * 2 * LANES:t * 2 * LANES + LANES] = jnp.concatenate(pair, axis=1)

    lane = lax.broadcasted_iota(jnp.int32, (2 * tq, LANES), 1)
    low_half = lane < HEAD_DIM
    top_rows = lax.broadcasted_iota(jnp.int32, (2 * tq, 1), 0) < tq
    lane_tq = lax.broadcasted_iota(jnp.int32, (tq, LANES), 1)

    def key_start(j):
        return pl.multiple_of((i * blocks + j) * tq, tq)

    def scores(res, j, g):
        base = g * Q_PER_KV * HEAD_DIM
        rows = slice(j * tq, (j + 1) * tq)
        q2 = jnp.concatenate([q_ref[res, rows, base:base + LANES],
                              q_ref[res, rows, base + LANES:base + 2 * LANES]], axis=0)
        k_win = jnp.concatenate(
            [kext_ref[res, pl.ds(key_start(j), width), t * LANES:(t + 1) * LANES]
             for t in range(g * HEADS_PER_TILE, (g + 1) * HEADS_PER_TILE)], axis=0)
        return lax.dot_general(q2, k_win, (((1,), (1,)), ((), ())), preferred_element_type=jnp.float32)

    work = [(res, j, g) for res in range(n_res) for j in range(blocks) for g in range(N_KV_HEADS)]
    s_next = scores(*work[0])
    for n, (res, j, g) in enumerate(work):
        base = g * Q_PER_KV * HEAD_DIM
        rows = slice(j * tq, (j + 1) * tq)
        blk = i * blocks + j
        if n_blk == 1:
            variant = 0
        else:
            variant = jnp.where(blk == 0, 0, jnp.where(blk == n_blk - 1, 2, 1))
        if dil > 1:
            token_rows = pl.ds(blk * (tq * dil) + pl.program_id(1) * n_res + res, tq, stride=dil)
        s_all = s_next
        if n + 1 < len(work):
            s_next = scores(*work[n + 1])
        probs = []
        maxes = []
        sink_terms = []
        for v in range(HEADS_PER_TILE):
            t = g * HEADS_PER_TILE + v
            s = s_all[:, v * width:(v + 1) * width] + bias_ref[variant, t * 2 * tq:(t + 1) * 2 * tq, :]
            m = jnp.max(s, axis=-1, keepdims=True)
            if has_sink:
                h_top, h_bot = _dot_heads(g, v)
                sk = jnp.where(top_rows, sink_ref[h_top], sink_ref[h_bot])
                m = jnp.maximum(m, sk)
                sink_terms.append(jnp.exp(sk - m))
            maxes.append(m)
            probs.append(jnp.exp(s - m).astype(bf16))
        v_win = jnp.concatenate(
            [vext_ref[res, pl.ds(key_start(j), width), t * 2 * LANES:(t + 1) * 2 * LANES]
             for t in range(g * HEADS_PER_TILE, (g + 1) * HEADS_PER_TILE)], axis=0)
        acc = jnp.dot(jnp.concatenate(probs, axis=1), v_win, preferred_element_type=jnp.float32)
        denom = acc[:, LANES:]
        if has_sink:
            denom = denom + jnp.where(low_half, sink_terms[0], sink_terms[1])
        o = acc[:, :LANES] / denom
        if dil == 1:
            o_ref[rows, base:base + LANES] = o[:tq].astype(o_ref.dtype)
            o_ref[rows, base + LANES:base + 2 * LANES] = o[tq:].astype(o_ref.dtype)
        else:
            o_ref[g * HEADS_PER_TILE, token_rows, :] = o[:tq]
            o_ref[g * HEADS_PER_TILE + 1, token_rows, :] = o[tq:]
        if want_lse:
            if g == 0:
                lse_acc = jnp.zeros((tq, LANES), jnp.float32)
            lse = jnp.where(low_half, maxes[0], maxes[1]) + jnp.log(denom)
            for half, half_rows in ((0, slice(0, tq)), (1, slice(tq, 2 * tq))):
                tile = g * HEADS_PER_TILE + half
                here = (lane_tq == tile) | (lane_tq == HEAD_DIM + tile)
                lse_acc = jnp.where(here, lse[half_rows], lse_acc)
            if g == N_KV_HEADS - 1:
                if dil == 1:
                    lse_ref[rows, :] = lse_acc
                else:
                    lse_ref[token_rows, :] = lse_acc


def _band_attention(qkv, bias, radius, *, sink=None, want_lse=False):
    B, dil, L, _ = qkv.shape
    S = L * dil
    tq = ATTN_TQ
    ts = min(L, ATTN_STEP_ROWS)
    n_res = min(dil, ATTN_STEP_ROWS // ts)
    D = Q_COLS
    n_tiles = N_KV_HEADS * HEADS_PER_TILE

    in_specs = [
        pl.BlockSpec((None, n_res, ts, Q_COLS), lambda b, r, i: (b, r, i, 0)),
        pl.BlockSpec((None, n_res, L, KV_COLS), lambda b, r, i: (b, r, 0, Q_COLS // KV_COLS)),
        pl.BlockSpec((None, n_res, L, KV_COLS), lambda b, r, i: (b, r, 0, Q_COLS // KV_COLS + 1)),
        _const_spec(bias.shape),
        pl.BlockSpec(memory_space=pltpu.SMEM),
    ]
    args = [qkv, qkv, qkv, bias, jnp.zeros((1,), jnp.int32)]
    if sink is not None:
        in_specs.append(pl.BlockSpec(memory_space=pltpu.SMEM))
        args.append(sink.astype(jnp.float32))
    if dil == 1:
        out_specs = [pl.BlockSpec((None, ts, D), lambda b, r, i: (b, i, 0))]
        out_shape = [jax.ShapeDtypeStruct((B, S, D), jnp.bfloat16)]
        lse_spec = pl.BlockSpec((None, ts, LANES), lambda b, r, i: (b, i, 0))
    else:
        assert want_lse
        out_specs = [pl.BlockSpec((None, D // LANES, S, LANES), lambda b, r, i: (b, 0, 0, 0))]
        out_shape = [jax.ShapeDtypeStruct((B, D // LANES, S, LANES), jnp.float32)]
        lse_spec = pl.BlockSpec((None, S, LANES), lambda b, r, i: (b, 0, 0))
    if want_lse:
        out_specs.append(lse_spec)
        out_shape.append(jax.ShapeDtypeStruct((B, S, LANES), jnp.float32))
    scratch = [pltpu.VMEM((n_res, L + 2 * radius, n_tiles * LANES), jnp.bfloat16),
               pltpu.VMEM((n_res, L + 2 * radius, n_tiles * 2 * LANES), jnp.bfloat16)]

    kernel = functools.partial(_attn_kernel, tq=tq, radius=radius, seq_len=L, dil=dil,
                               has_sink=sink is not None, want_lse=want_lse)
    out = pl.pallas_call(
        kernel,
        grid=(B, dil // n_res, L // ts),
        in_specs=in_specs,
        out_specs=out_specs,
        out_shape=out_shape,
        scratch_shapes=scratch,
        compiler_params=_params(3),
        name=f"band_attn_r{radius}_d{dil}",
    )(*args)
    if want_lse:
        return out[0], out[1]
    return out[0]


def _head_expand_matrix():
    e = np.zeros((LANES, Q_COLS), np.float32)
    for h in range(N_HEADS):
        e[_lse_lane(h), h * HEAD_DIM:(h + 1) * HEAD_DIM] = 1.0
    return jnp.asarray(e, jnp.bfloat16)


def _read_rows_f32(o_ref):
    if len(o_ref.shape) == 2:
        return o_ref[...].astype(jnp.float32)
    return jnp.concatenate([o_ref[j] for j in range(o_ref.shape[0])], axis=1)


def _ffn_kernel(*refs, n_mix, final):
    refs = list(refs)
    x_ref = refs[0]
    pos = 1
    x = x_ref[...]
    if n_mix >= 1:
        o_refs = refs[pos:pos + n_mix]
        pos += n_mix
        if n_mix > 1:
            lse_refs = refs[pos:pos + n_mix]
            expand_ref = refs[pos + n_mix]
            pos += n_mix + 1
        wo_ref = refs[pos]
        pos += 1
        if n_mix == 1:
            o = o_refs[0][...]
        else:
            lses = [r[...] for r in lse_refs]
            top = functools.reduce(jnp.maximum, lses)
            es = [jnp.exp(l - top) for l in lses]
            inv_tot = 1.0 / functools.reduce(lambda a, b: a + b, es)
            o = None
            for e, o_ref in zip(es, o_refs):
                w = e * inv_tot
                w_hi = w.astype(jnp.bfloat16)
                w_lo = (w - w_hi.astype(jnp.float32)).astype(jnp.bfloat16)
                w_full = (jnp.dot(w_hi, expand_ref[...], preferred_element_type=jnp.float32)
                          + jnp.dot(w_lo, expand_ref[...], preferred_element_type=jnp.float32))
                term = w_full * _read_rows_f32(o_ref)
                o = term if o is None else o + term
            o = o.astype(jnp.bfloat16)
        x = x + jnp.dot(o, wo_ref[...], preferred_element_type=jnp.float32)
    g_ref, w1_ref, w2_ref = refs[pos:pos + 3]
    pos += 3
    if final:
        fg_ref = refs[pos]
        pos += 1
    out_ref = refs[pos]

    h = _rmsnorm_f32(x, g_ref[...]).astype(jnp.bfloat16)
    acc = x
    d_ff = w1_ref.shape[1]
    for c in range(d_ff // FFN_CHUNK):
        cols = slice(c * FFN_CHUNK, (c + 1) * FFN_CHUNK)
        a = jnp.dot(h, w1_ref[:, cols], preferred_element_type=jnp.float32)
        a = jnp.square(jnp.maximum(a, 0.0)).astype(jnp.bfloat16)
        acc = acc + jnp.dot(a, w2_ref[cols, :], preferred_element_type=jnp.float32)
    if final:
        acc = _rmsnorm_f32(acc, fg_ref[...])
    out_ref[...] = acc


def _ffn(x, g, w1, w2, *, mix=None, final_g=None):
    B, S, D = x.shape
    d_ff = w1.shape[1]
    tm = FFN_TM
    row_spec = lambda c: pl.BlockSpec((None, tm, c), lambda b, i: (b, i, 0))
    slab_spec = lambda c: pl.BlockSpec((None, c // LANES, tm, LANES), lambda b, i: (b, 0, i, 0))
    in_specs = [row_spec(D)]
    args = [x]
    n_mix = 0
    if mix is not None:
        o_list, lse_list, w_o = mix
        n_mix = len(o_list)
        in_specs += [row_spec(D) if o.ndim == 3 else slab_spec(D) for o in o_list]
        args += o_list
        if n_mix > 1:
            in_specs += [row_spec(LANES)] * n_mix + [_const_spec((LANES, Q_COLS))]
            args += lse_list + [_head_expand_matrix()]
        in_specs.append(_const_spec((D, D)))
        args.append(w_o)
    in_specs += [_const_spec((1, D)), _const_spec((D, d_ff)), _const_spec((d_ff, D))]
    args += [g.reshape(1, D), w1, w2]
    if final_g is not None:
        in_specs.append(_const_spec((1, D)))
        args.append(final_g.reshape(1, D))
    out = pl.pallas_call(
        functools.partial(_ffn_kernel, n_mix=n_mix, final=final_g is not None),
        grid=(B, S // tm),
        in_specs=in_specs,
        out_specs=row_spec(D),
        out_shape=jax.ShapeDtypeStruct((B, S, D), jnp.float32),
        compiler_params=_params(2),
        name=f"ffn_mix{n_mix}" + ("_final" if final_g is not None else ""),
    )(*args)
    return out


def _gmlp_kernel(x_ref, g_ref, win_ref, lng_ref, lnb_ref, ws_ref, bs_ref, wout_ref, o_ref):
    x = x_ref[...]
    tm = x.shape[0]
    hidden = wout_ref.shape[0]
    gcols = hidden // B_GROUPS
    h = _rmsnorm_f32(x, g_ref[...]).astype(jnp.bfloat16)
    z = jnp.dot(h, win_ref[...], preferred_element_type=jnp.float32)
    z = 0.5 * z * (1.0 + lax.erf(z * (2.0 ** -0.5)))
    u = z[:, :hidden]
    v = z[:, hidden:]
    mu = jnp.mean(v, axis=-1, keepdims=True)
    vc = v - mu
    var = jnp.mean(vc * vc, axis=-1, keepdims=True)
    vn = (vc * lax.rsqrt(var + LN_EPS) * lng_ref[...] + lnb_ref[...]).astype(jnp.bfloat16)
    rows = []
    for c in range(tm // B_CHUNK):
        r0 = c * B_CHUNK
        parts = []
        for grp in range(B_GROUPS):
            parts.append(jnp.dot(ws_ref[grp], vn[r0:r0 + B_CHUNK, grp * gcols:(grp + 1) * gcols],
                                 preferred_element_type=jnp.float32))
        mixed = jnp.concatenate(parts, axis=-1) + bs_ref[...]
        rows.append((u[r0:r0 + B_CHUNK, :] * mixed).astype(jnp.bfloat16))
    t = jnp.concatenate(rows, axis=0)
    o_ref[...] = x + jnp.dot(t, wout_ref[...], preferred_element_type=jnp.float32)


def _gmlp(x, g, w_in, ln_g, ln_b, w_s, b_s, w_out):
    B, S, D = x.shape
    hidden = w_out.shape[0]
    T = B * S
    tm = GMLP_TM
    bs_full = jnp.repeat(jnp.transpose(b_s).astype(jnp.float32), hidden // B_GROUPS, axis=1)
    row_spec = pl.BlockSpec((tm, D), lambda i: (i, 0))
    out = pl.pallas_call(
        _gmlp_kernel,
        grid=(T // tm,),
        in_specs=[row_spec, _const_spec((1, D)), _const_spec((D, 2 * hidden)),
                  _const_spec((1, hidden)), _const_spec((1, hidden)),
                  _const_spec((B_GROUPS, B_CHUNK, B_CHUNK)), _const_spec((B_CHUNK, hidden)),
                  _const_spec((hidden, D))],
        out_specs=row_spec,
        out_shape=jax.ShapeDtypeStruct((T, D), jnp.float32),
        compiler_params=_params(1),
        name="gmlp",
    )(x.reshape(T, D), g.reshape(1, D), w_in, ln_g.reshape(1, hidden).astype(jnp.float32),
      ln_b.reshape(1, hidden).astype(jnp.float32), w_s, bs_full, w_out)
    return out.reshape(B, S, D)


def _trunk(x, p):
    depth = p["norm_mix_g"].shape[0]
    S = x.shape[1]
    for i in range(depth):
        kind, j = i % N_MIXERS, i // N_MIXERS
        g_mix = p["norm_mix_g"][i]
        mix = None
        if kind == 0:
            (qkv,) = _qkv_projection(x, g_mix, [p["a_wqkv"][j]], [1])
            o = _band_attention(qkv, p["a_bias"], A_RADIUS, sink=p["a_sink"][j])
            mix = ([o], None, p["a_wo"][j])
        elif kind == 1:
            x = _gmlp(x, g_mix, p["b_win"][j], p["b_ln_g"][j], p["b_ln_b"][j], p["b_ws"][j],
                      p["b_bs"][j], p["b_wo"][j])
        else:
            o_list, lse_list = [], []
            dils = [dil for _, dil in C_GROUPS]
            weights = [p["c_wqkv"][j][:, gi * QKV_COLS:(gi + 1) * QKV_COLS] for gi in range(len(dils))]
            qkvs = _qkv_projection(x, g_mix, weights, dils)
            for gi, (window, dil) in enumerate(C_GROUPS):
                o, lse = _band_attention(qkvs[gi], p["c_bias"][gi], window // (2 * dil), want_lse=True)
                o_list.append(o)
                lse_list.append(lse)
            mix = (o_list, lse_list, p["c_wo"][j])
        final_g = p["final_g"] if i == depth - 1 else None
        x = _ffn(x, p["norm_ffn_g"][i], p["ffn_w1"][i], p["ffn_w2"][i], mix=mix, final_g=final_g)
    return x


def kernel(x_prompt, x_sample, rel_bias, norm_mix_g, norm_ffn_g, final_g, ffn_w1, ffn_w2, a_wqkv,
           a_sink, a_wo, b_win, b_ln_g, b_ln_b, b_ws, b_bs, b_wo, c_wqkv, c_wo):
    bf16 = jnp.bfloat16
    f32 = jnp.float32
    S = x_prompt.shape[1]
    assert x_sample.shape[1] == S
    p = {
        "norm_mix_g": norm_mix_g.astype(f32), "norm_ffn_g": norm_ffn_g.astype(f32),
        "final_g": final_g.astype(f32),
        "ffn_w1": ffn_w1.astype(bf16), "ffn_w2": ffn_w2.astype(bf16),
        "a_wqkv": a_wqkv.astype(bf16), "a_sink": a_sink, "a_wo": a_wo.astype(bf16),
        "b_win": b_win.astype(bf16), "b_ln_g": b_ln_g, "b_ln_b": b_ln_b,
        "b_ws": b_ws.astype(bf16), "b_bs": b_bs, "b_wo": b_wo.astype(bf16),
        "c_wqkv": c_wqkv.astype(bf16), "c_wo": c_wo.astype(bf16),
        "a_bias": _band_bias_table(rel_bias, ATTN_TQ, A_RADIUS, 1, S),
        "c_bias": [_band_bias_table(rel_bias, ATTN_TQ, window // (2 * dil), dil, S // dil)
                   for window, dil in C_GROUPS],
    }
    return _trunk(x_prompt, p), _trunk(x_sample, p)
```

```python
import functools

import numpy as np
import jax
import jax.numpy as jnp
from jax import lax
from jax.experimental import pallas as pl
from jax.experimental.pallas import tpu as pltpu

HEAD_DIM = 64
N_MIXERS = 3
N_HEADS = 16
N_KV_HEADS = 4
A_RADIUS = 128
B_CHUNK = 128
B_GROUPS = 8
C_GROUPS = ((128, 1), (512, 4), (2048, 16))
NUM_BUCKETS = 32
REL_MAX_DISTANCE = 1024
RMS_EPS = 1e-6
LN_EPS = 1e-5

Q_COLS = N_HEADS * HEAD_DIM
KV_COLS = N_KV_HEADS * HEAD_DIM
QKV_COLS = Q_COLS + 2 * KV_COLS
LANES = 128
HEADS_PER_TILE = LANES // HEAD_DIM
Q_PER_KV = N_HEADS // N_KV_HEADS
MASK_VALUE = -1e30

V7X_VMEM_BYTES = 64 * 1024 * 1024
VMEM_LIMIT_BYTES = V7X_VMEM_BYTES - 8 * 1024 * 1024

ATTN_TQ = 128
ATTN_STEP_ROWS = 512
FFN_TM = 512
FFN_CHUNK = 1024
PROJ_TM = 1024
GMLP_TM = 512
GMLP_SUB = 256


def _const_spec(shape):
    zeros = (0,) * len(shape)
    return pl.BlockSpec(shape, lambda *_: zeros, pipeline_mode=pl.Buffered(1))


def _params(n_grid_dims):
    return pltpu.CompilerParams(
        dimension_semantics=("arbitrary",) * n_grid_dims,
        vmem_limit_bytes=VMEM_LIMIT_BYTES,
    )


def _rmsnorm_f32(x, g):
    ms = jnp.mean(x * x, axis=-1, keepdims=True)
    return x * lax.rsqrt(ms + RMS_EPS) * g


def _lse_lane(head):
    return head // HEADS_PER_TILE + HEAD_DIM * (head % HEADS_PER_TILE)


def _proj_kernel(*refs, dils):
    n = len(dils)
    x_ref, g_ref = refs[:2]
    w_refs = refs[2:2 + n]
    o_refs = refs[2 + n:2 + 2 * n]
    hn = _rmsnorm_f32(x_ref[...], g_ref[...])
    tm, d_model = hn.shape
    n_slabs = d_model // LANES
    if any(d > 1 for d in dils):
        slab_ref = refs[2 + 2 * n]
        for j in range(n_slabs):
            slab_ref[j] = hn[:, j * LANES:(j + 1) * LANES]
    for dil, w_ref, o_ref in zip(dils, w_refs, o_refs):
        n_rows = tm // dil
        if dil == 1:
            h = hn
        else:
            h = jnp.concatenate(
                [jnp.concatenate([slab_ref[j, pl.ds(r, n_rows, stride=dil), :] for j in range(n_slabs)], axis=1)
                 for r in range(dil)], axis=0)
        y = jnp.dot(h.astype(jnp.bfloat16), w_ref[...], preferred_element_type=jnp.float32)
        q = (y[:, :Q_COLS] * (HEAD_DIM ** -0.5)).astype(o_ref.dtype)
        kv = y[:, Q_COLS:].astype(o_ref.dtype)
        for r in range(dil):
            o_ref[r, :, :Q_COLS] = q[r * n_rows:(r + 1) * n_rows]
            o_ref[r, :, Q_COLS:] = kv[r * n_rows:(r + 1) * n_rows]


def _qkv_projection(x, g, weights, dils):
    B, S, D = x.shape
    tm = PROJ_TM
    n = len(dils)
    in_specs = [pl.BlockSpec((None, tm, D), lambda b, i: (b, i, 0)), _const_spec((1, D))]
    in_specs += [_const_spec((D, QKV_COLS))] * n
    out_specs = [pl.BlockSpec((None, dil, tm // dil, QKV_COLS), lambda b, i: (b, 0, i, 0)) for dil in dils]
    out_shape = [jax.ShapeDtypeStruct((B, dil, S // dil, QKV_COLS), jnp.bfloat16) for dil in dils]
    scratch = []
    if any(d > 1 for d in dils):
        scratch.append(pltpu.VMEM((D // LANES, tm, LANES), jnp.float32))
    return pl.pallas_call(
        functools.partial(_proj_kernel, dils=tuple(dils)),
        grid=(B, S // tm),
        in_specs=in_specs,
        out_specs=out_specs,
        out_shape=out_shape,
        scratch_shapes=scratch,
        compiler_params=_params(2),
        name="qkv_proj_" + "_".join(f"d{d}" for d in dils),
    )(x, g.reshape(1, D), *weights)


def _rel_bucket(rel):
    half = NUM_BUCKETS // 2
    max_exact = half // 2
    n = np.abs(rel)
    large = max_exact + (np.log(np.maximum(n, 1) / max_exact) / np.log(REL_MAX_DISTANCE / max_exact)
                         * (half - max_exact)).astype(np.int32)
    large = np.minimum(large, half - 1)
    return (rel > 0).astype(np.int32) * half + np.where(n < max_exact, n, large)


def _dot_heads(g, v):
    return g * Q_PER_KV + v, g * Q_PER_KV + HEADS_PER_TILE + v


def _band_bias_table(rel_bias, tq, radius, dil, seq_len):
    width = tq + 2 * radius
    n_blk = seq_len // tq
    col = np.arange(width)[None, :]
    rel = col - radius - np.arange(tq)[:, None]
    n_diag = width + tq - 1
    diag_rel = (np.arange(n_diag) - (tq - 1) - radius) * dil
    per_diag = jnp.transpose(rel_bias[_rel_bucket(diag_rel)], (1, 0)).astype(jnp.float32)
    padded = jnp.pad(per_diag, ((0, 0), (0, 1)))
    tiled = jnp.broadcast_to(padded[:, None, :], (N_HEADS, tq, n_diag + 1)).reshape(N_HEADS, -1)
    skewed = tiled[:, :tq * n_diag].reshape(N_HEADS, tq, n_diag)
    per_head = skewed[:, :, tq - 1:]
    order = [h for g in range(N_KV_HEADS) for v in range(HEADS_PER_TILE) for h in _dot_heads(g, v)]
    stacked = per_head[np.asarray(order)].reshape(N_HEADS * tq, width)
    in_band = np.broadcast_to(np.abs(rel) <= radius, (N_HEADS, tq, width)).reshape(N_HEADS * tq, width)
    before = col < radius
    after = col >= width - radius
    if n_blk == 1:
        edge_masks = [before | after]
    else:
        edge_masks = [before, np.zeros_like(before), after]
    tables = [jnp.where(jnp.asarray(in_band & ~e), stacked, MASK_VALUE) for e in edge_masks]
    return jnp.stack(tables, axis=0)


def _attn_kernel(*refs, tq, radius, seq_len, dil, has_sink, want_lse):
    refs = list(refs)
    q_ref, k_ref, v_ref, bias_ref, zero_ref = refs[:5]
    pos = 5
    sink_ref = None
    if has_sink:
        sink_ref = refs[pos]
        pos += 1
    o_ref = refs[pos]
    pos += 1
    lse_ref = None
    if want_lse:
        lse_ref = refs[pos]
        pos += 1
    kext_ref, vext_ref = refs[pos:pos + 2]

    n_res = q_ref.shape[0]
    blocks = q_ref.shape[1] // tq
    width = tq + 2 * radius
    n_blk = seq_len // tq
    i = pl.program_id(2)
    first_step = (pl.program_id(0) == 0) & (pl.program_id(1) == 0) & (i == 0)
    bf16 = jnp.bfloat16

    @pl.when(first_step)
    def _():
        kext_ref[...] = jnp.zeros(kext_ref.shape, bf16)
        rows = vext_ref.shape[1]
        zeros = jnp.zeros((rows, HEAD_DIM), bf16)
        ones = jnp.ones((rows, HEAD_DIM), bf16)
        for res in range(n_res):
            for t in range(N_KV_HEADS * HEADS_PER_TILE):
                v = t % HEADS_PER_TILE
                tile = [zeros, zeros, ones, zeros] if v == 0 else [zeros, zeros, zeros, ones]
                vext_ref[res, :, t * 2 * LANES:(t + 1) * 2 * LANES] = jnp.concatenate(tile, axis=1)

    @pl.when(i + zero_ref[0] == 0)
    def _():
        zeros = jnp.zeros((seq_len, HEAD_DIM), bf16)
        body = slice(radius, radius + seq_len)
        for res in range(n_res):
            for g in range(N_KV_HEADS):
                kg = k_ref[res, :, g * HEAD_DIM:(g + 1) * HEAD_DIM]
                vg = v_ref[res, :, g * HEAD_DIM:(g + 1) * HEAD_DIM]
                for v in range(HEADS_PER_TILE):
                    t = g * HEADS_PER_TILE + v
                    pair = [kg, zeros] if v == 0 else [zeros, kg]
                    kext_ref[res, body, t * LANES:(t + 1) * LANES] = jnp.concatenate(pair, axis=1)
                    pair = [vg, zeros] if v == 0 else [zeros, vg]
                    vext_ref[res, body, t * 2 * LANES:t * 2 * LANES + LANES] = jnp.concatenate(pair, axis=1)

    lane = lax.broadcasted_iota(jnp.int32, (2 * tq, LANES), 1)
    low_half = lane < HEAD_DIM
    top_rows = lax.broadcasted_iota(jnp.int32, (2 * tq, 1), 0) < tq
    lane_tq = lax.broadcasted_iota(jnp.int32, (tq, LANES), 1)

    def key_start(j):
        return pl.multiple_of((i * blocks + j) * tq, tq)

    def scores(res, j, g):
        base = g * Q_PER_KV * HEAD_DIM
        rows = slice(j * tq, (j + 1) * tq)
        q2 = jnp.concatenate([q_ref[res, rows, base:base + LANES],
                              q_ref[res, rows, base + LANES:base + 2 * LANES]], axis=0)
        k_win = jnp.concatenate(
            [kext_ref[res, pl.ds(key_start(j), width), t * LANES:(t + 1) * LANES]
             for t in range(g * HEADS_PER_TILE, (g + 1) * HEADS_PER_TILE)], axis=0)
        return lax.dot_general(q2, k_win, (((1,), (1,)), ((), ())), preferred_element_type=jnp.float32)

    work = [(res, j, g) for res in range(n_res) for j in range(blocks) for g in range(N_KV_HEADS)]
    s_next = scores(*work[0])
    for n, (res, j, g) in enumerate(work):
        base = g * Q_PER_KV * HEAD_DIM
        rows = slice(j * tq, (j + 1) * tq)
        blk = i * blocks + j
        if n_blk == 1:
            variant = 0
        else:
            variant = jnp.where(blk == 0, 0, jnp.where(blk == n_blk - 1, 2, 1))
        if dil > 1:
            token_rows = pl.ds(blk * (tq * dil) + pl.program_id(1) * n_res + res, tq, stride=dil)
        s_all = s_next
        if n + 1 < len(work):
            s_next = scores(*work[n + 1])
        probs = []
        maxes = []
        sink_terms = []
        for v in range(HEADS_PER_TILE):
            t = g * HEADS_PER_TILE + v
            s = s_all[:, v * width:(v + 1) * width] + bias_ref[variant, t * 2 * tq:(t + 1) * 2 * tq, :]
            m = jnp.max(s, axis=-1, keepdims=True)
            if has_sink:
                h_top, h_bot = _dot_heads(g, v)
                sk = jnp.where(top_rows, sink_ref[h_top], sink_ref[h_bot])
                m = jnp.maximum(m, sk)
                sink_terms.append(jnp.exp(sk - m))
            maxes.append(m)
            probs.append(jnp.exp(s - m).astype(bf16))
        v_win = jnp.concatenate(
            [vext_ref[res, pl.ds(key_start(j), width), t * 2 * LANES:(t + 1) * 2 * LANES]
             for t in range(g * HEADS_PER_TILE, (g + 1) * HEADS_PER_TILE)], axis=0)
        acc = jnp.dot(jnp.concatenate(probs, axis=1), v_win, preferred_element_type=jnp.float32)
        denom = acc[:, LANES:]
        if has_sink:
            denom = denom + jnp.where(low_half, sink_terms[0], sink_terms[1])
        o = acc[:, :LANES] / denom
        if dil == 1:
            o_ref[rows, base:base + LANES] = o[:tq].astype(o_ref.dtype)
            o_ref[rows, base + LANES:base + 2 * LANES] = o[tq:].astype(o_ref.dtype)
        else:
            o_ref[g * HEADS_PER_TILE, token_rows, :] = o[:tq]
            o_ref[g * HEADS_PER_TILE + 1, token_rows, :] = o[tq:]
        if want_lse:
            if g == 0:
                lse_acc = jnp.zeros((tq, LANES), jnp.float32)
            lse = jnp.where(low_half, maxes[0], maxes[1]) + jnp.log(denom)
            for half, half_rows in ((0, slice(0, tq)), (1, slice(tq, 2 * tq))):
                tile = g * HEADS_PER_TILE + half
                here = (lane_tq == tile) | (lane_tq == HEAD_DIM + tile)
                lse_acc = jnp.where(here, lse[half_rows], lse_acc)
            if g == N_KV_HEADS - 1:
                if dil == 1:
                    lse_ref[rows, :] = lse_acc
                else:
                    lse_ref[token_rows, :] = lse_acc


def _band_attention(qkv, bias, radius, *, sink=None, want_lse=False):
    B, dil, L, _ = qkv.shape
    S = L * dil
    tq = ATTN_TQ
    ts = min(L, ATTN_STEP_ROWS)
    n_res = min(dil, ATTN_STEP_ROWS // ts)
    D = Q_COLS
    n_tiles = N_KV_HEADS * HEADS_PER_TILE

    in_specs = [
        pl.BlockSpec((None, n_res, ts, Q_COLS), lambda b, r, i: (b, r, i, 0)),
        pl.BlockSpec((None, n_res, L, KV_COLS), lambda b, r, i: (b, r, 0, Q_COLS // KV_COLS)),
        pl.BlockSpec((None, n_res, L, KV_COLS), lambda b, r, i: (b, r, 0, Q_COLS // KV_COLS + 1)),
        _const_spec(bias.shape),
        pl.BlockSpec(memory_space=pltpu.SMEM),
    ]
    args = [qkv, qkv, qkv, bias, jnp.zeros((1,), jnp.int32)]
    if sink is not None:
        in_specs.append(pl.BlockSpec(memory_space=pltpu.SMEM))
        args.append(sink.astype(jnp.float32))
    if dil == 1:
        out_specs = [pl.BlockSpec((None, ts, D), lambda b, r, i: (b, i, 0))]
        out_shape = [jax.ShapeDtypeStruct((B, S, D), jnp.bfloat16)]
        lse_spec = pl.BlockSpec((None, ts, LANES), lambda b, r, i: (b, i, 0))
    else:
        assert want_lse
        out_specs = [pl.BlockSpec((None, D // LANES, S, LANES), lambda b, r, i: (b, 0, 0, 0))]
        out_shape = [jax.ShapeDtypeStruct((B, D // LANES, S, LANES), jnp.float32)]
        lse_spec = pl.BlockSpec((None, S, LANES), lambda b, r, i: (b, 0, 0))
    if want_lse:
        out_specs.append(lse_spec)
        out_shape.append(jax.ShapeDtypeStruct((B, S, LANES), jnp.float32))
    scratch = [pltpu.VMEM((n_res, L + 2 * radius, n_tiles * LANES), jnp.bfloat16),
               pltpu.VMEM((n_res, L + 2 * radius, n_tiles * 2 * LANES), jnp.bfloat16)]

    kernel = functools.partial(_attn_kernel, tq=tq, radius=radius, seq_len=L, dil=dil,
                               has_sink=sink is not None, want_lse=want_lse)
    out = pl.pallas_call(
        kernel,
        grid=(B, dil // n_res, L // ts),
        in_specs=in_specs,
        out_specs=out_specs,
        out_shape=out_shape,
        scratch_shapes=scratch,
        compiler_params=_params(3),
        name=f"band_attn_r{radius}_d{dil}",
    )(*args)
    if want_lse:
        return out[0], out[1]
    return out[0]


def _head_expand_matrix():
    e = np.zeros((LANES, Q_COLS), np.float32)
    for h in range(N_HEADS):
        e[_lse_lane(h), h * HEAD_DIM:(h + 1) * HEAD_DIM] = 1.0
    return jnp.asarray(np.concatenate([e, e], axis=0), jnp.bfloat16)


def _read_rows_f32(o_ref):
    if len(o_ref.shape) == 2:
        return o_ref[...].astype(jnp.float32)
    return jnp.concatenate([o_ref[j] for j in range(o_ref.shape[0])], axis=1)


def _ffn_kernel(*refs, n_mix, final):
    refs = list(refs)
    x_ref = refs[0]
    pos = 1
    x = x_ref[...]
    if n_mix >= 1:
        o_refs = refs[pos:pos + n_mix]
        pos += n_mix
        if n_mix > 1:
            lse_refs = refs[pos:pos + n_mix]
            expand_ref = refs[pos + n_mix]
            pos += n_mix + 1
        wo_ref = refs[pos]
        pos += 1
        if n_mix == 1:
            o = o_refs[0][...]
        else:
            lses = [r[...] for r in lse_refs]
            top = functools.reduce(jnp.maximum, lses)
            es = [jnp.exp(l - top) for l in lses]
            inv_tot = 1.0 / functools.reduce(lambda a, b: a + b, es)
            o = None
            for e, o_ref in zip(es, o_refs):
                w = e * inv_tot
                w_hi = w.astype(jnp.bfloat16)
                w_lo = (w - w_hi.astype(jnp.float32)).astype(jnp.bfloat16)
                w_full = jnp.dot(jnp.concatenate([w_hi, w_lo], axis=1), expand_ref[...],
                                 preferred_element_type=jnp.float32)
                term = w_full * _read_rows_f32(o_ref)
                o = term if o is None else o + term
            o = o.astype(jnp.bfloat16)
        x = x + jnp.dot(o, wo_ref[...], preferred_element_type=jnp.float32)
    g_ref, w1_ref, w2_ref = refs[pos:pos + 3]
    pos += 3
    if final:
        fg_ref = refs[pos]
        pos += 1
    out_ref = refs[pos]

    h = _rmsnorm_f32(x, g_ref[...]).astype(jnp.bfloat16)
    acc = x
    d_ff = w1_ref.shape[1]
    for c in range(d_ff // FFN_CHUNK):
        cols = slice(c * FFN_CHUNK, (c + 1) * FFN_CHUNK)
        a = jnp.dot(h, w1_ref[:, cols], preferred_element_type=jnp.float32)
        a = jnp.square(jnp.maximum(a, 0.0)).astype(jnp.bfloat16)
        acc = acc + jnp.dot(a, w2_ref[cols, :], preferred_element_type=jnp.float32)
    if final:
        acc = _rmsnorm_f32(acc, fg_ref[...])
    out_ref[...] = acc


def _ffn(x, g, w1, w2, *, mix=None, final_g=None):
    B, S, D = x.shape
    d_ff = w1.shape[1]
    tm = FFN_TM
    row_spec = lambda c: pl.BlockSpec((None, tm, c), lambda b, i: (b, i, 0))
    slab_spec = lambda c: pl.BlockSpec((None, c // LANES, tm, LANES), lambda b, i: (b, 0, i, 0))
    in_specs = [row_spec(D)]
    args = [x]
    n_mix = 0
    if mix is not None:
        o_list, lse_list, w_o = mix
        n_mix = len(o_list)
        in_specs += [row_spec(D) if o.ndim == 3 else slab_spec(D) for o in o_list]
        args += o_list
        if n_mix > 1:
            in_specs += [row_spec(LANES)] * n_mix + [_const_spec((2 * LANES, Q_COLS))]
            args += lse_list + [_head_expand_matrix()]
        in_specs.append(_const_spec((D, D)))
        args.append(w_o)
    in_specs += [_const_spec((1, D)), _const_spec((D, d_ff)), _const_spec((d_ff, D))]
    args += [g.reshape(1, D), w1, w2]
    if final_g is not None:
        in_specs.append(_const_spec((1, D)))
        args.append(final_g.reshape(1, D))
    out = pl.pallas_call(
        functools.partial(_ffn_kernel, n_mix=n_mix, final=final_g is not None),
        grid=(B, S // tm),
        in_specs=in_specs,
        out_specs=row_spec(D),
        out_shape=jax.ShapeDtypeStruct((B, S, D), jnp.float32),
        compiler_params=_params(2),
        name=f"ffn_mix{n_mix}" + ("_final" if final_g is not None else ""),
    )(*args)
    return out


def _gmlp_kernel(x_ref, g_ref, win_ref, lng_ref, lnb_ref, ws_ref, bs_ref, wout_ref, o_ref):
    tm = x_ref.shape[0]
    hidden = wout_ref.shape[0]
    gcols = hidden // B_GROUPS
    subs = [slice(r0, r0 + GMLP_SUB) for r0 in range(0, tm, GMLP_SUB)]
    zs = []
    for rows in subs:
        h = _rmsnorm_f32(x_ref[rows, :], g_ref[...]).astype(jnp.bfloat16)
        zs.append(jnp.dot(h, win_ref[...], preferred_element_type=jnp.float32))
    for rows, z in zip(subs, zs):
        z = 0.5 * z * (1.0 + lax.erf(z * (2.0 ** -0.5)))
        u = z[:, :hidden]
        v = z[:, hidden:]
        mu = jnp.mean(v, axis=-1, keepdims=True)
        vc = v - mu
        var = jnp.mean(vc * vc, axis=-1, keepdims=True)
        vn = (vc * lax.rsqrt(var + LN_EPS) * lng_ref[...] + lnb_ref[...]).astype(jnp.bfloat16)
        gated = []
        for r0 in range(0, GMLP_SUB, B_CHUNK):
            parts = []
            for grp in range(B_GROUPS):
                parts.append(jnp.dot(ws_ref[grp], vn[r0:r0 + B_CHUNK, grp * gcols:(grp + 1) * gcols],
                                     preferred_element_type=jnp.float32))
            mixed = jnp.concatenate(parts, axis=-1) + bs_ref[...]
            gated.append((u[r0:r0 + B_CHUNK, :] * mixed).astype(jnp.bfloat16))
        t = jnp.concatenate(gated, axis=0)
        o_ref[rows, :] = x_ref[rows, :] + jnp.dot(t, wout_ref[...], preferred_element_type=jnp.float32)


def _gmlp(x, g, w_in, ln_g, ln_b, w_s, b_s, w_out):
    B, S, D = x.shape
    hidden = w_out.shape[0]
    T = B * S
    tm = GMLP_TM
    bs_full = jnp.repeat(jnp.transpose(b_s).astype(jnp.float32), hidden // B_GROUPS, axis=1)
    row_spec = pl.BlockSpec((tm, D), lambda i: (i, 0))
    out = pl.pallas_call(
        _gmlp_kernel,
        grid=(T // tm,),
        in_specs=[row_spec, _const_spec((1, D)), _const_spec((D, 2 * hidden)),
                  _const_spec((1, hidden)), _const_spec((1, hidden)),
                  _const_spec((B_GROUPS, B_CHUNK, B_CHUNK)), _const_spec((B_CHUNK, hidden)),
                  _const_spec((hidden, D))],
        out_specs=row_spec,
        out_shape=jax.ShapeDtypeStruct((T, D), jnp.float32),
        compiler_params=_params(1),
        name="gmlp",
    )(x.reshape(T, D), g.reshape(1, D), w_in, ln_g.reshape(1, hidden).astype(jnp.float32),
      ln_b.reshape(1, hidden).astype(jnp.float32), w_s, bs_full, w_out)
    return out.reshape(B, S, D)


def _trunk(x, p):
    depth = p["norm_mix_g"].shape[0]
    S = x.shape[1]
    for i in range(depth):
        kind, j = i % N_MIXERS, i // N_MIXERS
        g_mix = p["norm_mix_g"][i]
        mix = None
        if kind == 0:
            (qkv,) = _qkv_projection(x, g_mix, [p["a_wqkv"][j]], [1])
            o = _band_attention(qkv, p["a_bias"], A_RADIUS, sink=p["a_sink"][j])
            mix = ([o], None, p["a_wo"][j])
        elif kind == 1:
            x = _gmlp(x, g_mix, p["b_win"][j], p["b_ln_g"][j], p["b_ln_b"][j], p["b_ws"][j],
                      p["b_bs"][j], p["b_wo"][j])
        else:
            o_list, lse_list = [], []
            dils = [dil for _, dil in C_GROUPS]
            weights = [p["c_wqkv"][j][:, gi * QKV_COLS:(gi + 1) * QKV_COLS] for gi in range(len(dils))]
            qkvs = _qkv_projection(x, g_mix, weights, dils)
            for gi, (window, dil) in enumerate(C_GROUPS):
                o, lse = _band_attention(qkvs[gi], p["c_bias"][gi], window // (2 * dil), want_lse=True)
                o_list.append(o)
                lse_list.append(lse)
            mix = (o_list, lse_list, p["c_wo"][j])
        final_g = p["final_g"] if i == depth - 1 else None
        x = _ffn(x, p["norm_ffn_g"][i], p["ffn_w1"][i], p["ffn_w2"][i], mix=mix, final_g=final_g)
    return x


def kernel(x_prompt, x_sample, rel_bias, norm_mix_g, norm_ffn_g, final_g, ffn_w1, ffn_w2, a_wqkv,
           a_sink, a_wo, b_win, b_ln_g, b_ln_b, b_ws, b_bs, b_wo, c_wqkv, c_wo):
    bf16 = jnp.bfloat16
    f32 = jnp.float32
    S = x_prompt.shape[1]
    assert x_sample.shape[1] == S
    p = {
        "norm_mix_g": norm_mix_g.astype(f32), "norm_ffn_g": norm_ffn_g.astype(f32),
        "final_g": final_g.astype(f32),
        "ffn_w1": ffn_w1.astype(bf16), "ffn_w2": ffn_w2.astype(bf16),
        "a_wqkv": a_wqkv.astype(bf16), "a_sink": a_sink, "a_wo": a_wo.astype(bf16),
        "b_win": b_win.astype(bf16), "b_ln_g": b_ln_g, "b_ln_b": b_ln_b,
        "b_ws": b_ws.astype(bf16), "b_bs": b_bs, "b_wo": b_wo.astype(bf16),
        "c_wqkv": c_wqkv.astype(bf16), "c_wo": c_wo.astype(bf16),
        "a_bias": _band_bias_table(rel_bias, ATTN_TQ, A_RADIUS, 1, S),
        "c_bias": [_band_bias_table(rel_bias, ATTN_TQ, window // (2 * dil), dil, S // dil)
                   for window, dil in C_GROUPS],
    }
    return _trunk(x_prompt, p), _trunk(x_sample, p)
```

```python
import functools

import numpy as np
import jax
import jax.numpy as jnp
from jax import lax
from jax.experimental import pallas as pl
from jax.experimental.pallas import tpu as pltpu

HEAD_DIM = 64
N_MIXERS = 3
N_HEADS = 16
N_KV_HEADS = 4
A_RADIUS = 128
B_CHUNK = 128
B_GROUPS = 8
C_GROUPS = ((128, 1), (512, 4), (2048, 16))
NUM_BUCKETS = 32
REL_MAX_DISTANCE = 1024
RMS_EPS = 1e-6
LN_EPS = 1e-5

Q_COLS = N_HEADS * HEAD_DIM
KV_COLS = N_KV_HEADS * HEAD_DIM
QKV_COLS = Q_COLS + 2 * KV_COLS
LANES = 128
HEADS_PER_TILE = LANES // HEAD_DIM
Q_PER_KV = N_HEADS // N_KV_HEADS
KVX_COLS = N_KV_HEADS * HEADS_PER_TILE * LANES
MASK_VALUE = -1e30

V7X_VMEM_BYTES = 64 * 1024 * 1024
VMEM_LIMIT_BYTES = V7X_VMEM_BYTES - 8 * 1024 * 1024

ATTN_TQ = 128
ATTN_STEP_ROWS = 512
FFN_TM = 512
FFN_CHUNK = 1024
PROJ_TM = 1024
PROJ_TM_MULTI = 512
GMLP_TM = 512
GMLP_SUB = 256


def _const_spec(shape):
    zeros = (0,) * len(shape)
    return pl.BlockSpec(shape, lambda *_: zeros, pipeline_mode=pl.Buffered(1))


def _params(n_grid_dims):
    return pltpu.CompilerParams(
        dimension_semantics=("arbitrary",) * n_grid_dims,
        vmem_limit_bytes=VMEM_LIMIT_BYTES,
    )


def _rmsnorm_f32(x, g):
    ms = jnp.mean(x * x, axis=-1, keepdims=True)
    return x * lax.rsqrt(ms + RMS_EPS) * g


def _lse_lane(head):
    return head // HEADS_PER_TILE + HEAD_DIM * (head % HEADS_PER_TILE)


def _proj_kernel(*refs, dils):
    n = len(dils)
    x_ref, g_ref = refs[:2]
    w_refs = refs[2:2 + n]
    out_refs = refs[2 + n:2 + 4 * n]
    hn = _rmsnorm_f32(x_ref[...], g_ref[...])
    tm, d_model = hn.shape
    n_slabs = d_model // LANES
    if any(d > 1 for d in dils):
        slab_ref = refs[2 + 4 * n]
        for j in range(n_slabs):
            slab_ref[j] = hn[:, j * LANES:(j + 1) * LANES]
    low_half = lax.broadcasted_iota(jnp.int32, (tm, LANES), 1) < HEAD_DIM
    for gi, (dil, w_ref) in enumerate(zip(dils, w_refs)):
        q_ref, kx_ref, vx_ref = out_refs[3 * gi:3 * gi + 3]
        n_rows = tm // dil
        if dil == 1:
            h = hn
        else:
            h = jnp.concatenate(
                [jnp.concatenate([slab_ref[j, pl.ds(r, n_rows, stride=dil), :] for j in range(n_slabs)], axis=1)
                 for r in range(dil)], axis=0)
        y = jnp.dot(h.astype(jnp.bfloat16), w_ref[...], preferred_element_type=jnp.float32)
        pieces = [(q_ref, (y[:, :Q_COLS] * (HEAD_DIM ** -0.5)).astype(q_ref.dtype))]
        for o_ref, col0 in ((kx_ref, Q_COLS), (vx_ref, Q_COLS + KV_COLS)):
            tiles = []
            for t in range(KV_COLS // LANES):
                pair = y[:, col0 + t * LANES:col0 + (t + 1) * LANES]
                only_a = jnp.where(low_half, pair, 0.0)
                only_b = jnp.where(low_half, 0.0, pair)
                tiles += [only_a, pltpu.roll(only_a, HEAD_DIM, 1), pltpu.roll(only_b, HEAD_DIM, 1), only_b]
            pieces.append((o_ref, jnp.concatenate(tiles, axis=1).astype(o_ref.dtype)))
        for o_ref, val in pieces:
            for r in range(dil):
                o_ref[r] = val[r * n_rows:(r + 1) * n_rows]


def _qkv_projection(x, g, weights, dils):
    B, S, D = x.shape
    n = len(dils)
    tm = PROJ_TM if n == 1 else PROJ_TM_MULTI
    in_specs = [pl.BlockSpec((None, tm, D), lambda b, i: (b, i, 0)), _const_spec((1, D))]
    in_specs += [_const_spec((D, QKV_COLS))] * n
    out_specs, out_shape = [], []
    for dil in dils:
        for cols in (Q_COLS, KVX_COLS, KVX_COLS):
            out_specs.append(pl.BlockSpec((None, dil, tm // dil, cols), lambda b, i: (b, 0, i, 0)))
            out_shape.append(jax.ShapeDtypeStruct((B, dil, S // dil, cols), jnp.bfloat16))
    scratch = []
    if any(d > 1 for d in dils):
        scratch.append(pltpu.VMEM((D // LANES, tm, LANES), jnp.float32))
    outs = pl.pallas_call(
        functools.partial(_proj_kernel, dils=tuple(dils)),
        grid=(B, S // tm),
        in_specs=in_specs,
        out_specs=out_specs,
        out_shape=out_shape,
        scratch_shapes=scratch,
        compiler_params=_params(2),
        name="qkv_proj_" + "_".join(f"d{d}" for d in dils),
    )(x, g.reshape(1, D), *weights)
    return [tuple(outs[3 * gi:3 * gi + 3]) for gi in range(n)]


def _rel_bucket(rel):
    half = NUM_BUCKETS // 2
    max_exact = half // 2
    n = np.abs(rel)
    large = max_exact + (np.log(np.maximum(n, 1) / max_exact) / np.log(REL_MAX_DISTANCE / max_exact)
                         * (half - max_exact)).astype(np.int32)
    large = np.minimum(large, half - 1)
    return (rel > 0).astype(np.int32) * half + np.where(n < max_exact, n, large)


def _window_geometry(tq, radius, seq_len):
    width = min(tq + 2 * radius, seq_len)
    if seq_len // tq == 1:
        return width, (0,)
    return width, (0, radius, width - tq)


def _dot_heads(g, v):
    return g * Q_PER_KV + v, g * Q_PER_KV + HEADS_PER_TILE + v


def _band_bias_table(rel_bias, tq, radius, dil, seq_len):
    width, offsets = _window_geometry(tq, radius, seq_len)
    span = max(offsets) + tq - 1
    n_diag = span + width
    diag_rel = np.arange(n_diag) - span
    per_diag = jnp.transpose(rel_bias[_rel_bucket(diag_rel * dil)], (1, 0)).astype(jnp.float32)
    per_diag = jnp.where(jnp.asarray(np.abs(diag_rel) <= radius)[None], per_diag, MASK_VALUE)
    padded = jnp.pad(per_diag, ((0, 0), (0, 1)))
    tiled = jnp.broadcast_to(padded[:, None, :], (N_HEADS, tq, n_diag + 1)).reshape(N_HEADS, -1)
    skewed = tiled[:, :tq * n_diag].reshape(N_HEADS, tq, n_diag)
    order = [h for g in range(N_KV_HEADS) for v in range(HEADS_PER_TILE) for h in _dot_heads(g, v)]
    skewed = skewed[np.asarray(order)]
    tables = []
    for off in offsets:
        c0 = span - off
        tables.append(skewed[:, :, c0:c0 + width].reshape(N_HEADS * tq, width))
    return jnp.stack(tables, axis=0)


def _attn_kernel(*refs, tq, radius, seq_len, dil, has_sink, want_lse):
    refs = list(refs)
    q_ref, kx_ref, vx_ref, bias_ref = refs[:4]
    pos = 4
    sink_ref = None
    if has_sink:
        sink_ref = refs[pos]
        pos += 1
    o_ref = refs[pos]
    pos += 1
    lse_ref = refs[pos] if want_lse else None

    n_res = q_ref.shape[0]
    blocks = q_ref.shape[1] // tq
    width, offsets = _window_geometry(tq, radius, seq_len)
    n_blk = seq_len // tq
    i = pl.program_id(2)
    bf16 = jnp.bfloat16

    lane = lax.broadcasted_iota(jnp.int32, (tq, LANES), 1)
    low_half2 = lax.broadcasted_iota(jnp.int32, (2 * tq, LANES), 1) < HEAD_DIM
    top_rows = lax.broadcasted_iota(jnp.int32, (2 * tq, 1), 0) < tq
    ones_lo = jnp.where(lax.broadcasted_iota(jnp.int32, (width, LANES), 1) < HEAD_DIM, 1.0, 0.0).astype(bf16)
    ones_hi = jnp.where(lax.broadcasted_iota(jnp.int32, (width, LANES), 1) < HEAD_DIM, 0.0, 1.0).astype(bf16)

    def window_start(j):
        blk = i * blocks + j
        if n_blk == 1:
            return 0
        start = jnp.clip(blk * tq - radius, 0, seq_len - width)
        return pl.multiple_of(start, HEAD_DIM)

    def scores(res, j, g):
        base = g * Q_PER_KV * HEAD_DIM
        rows = slice(j * tq, (j + 1) * tq)
        q2 = jnp.concatenate([q_ref[res, rows, base:base + LANES],
                              q_ref[res, rows, base + LANES:base + 2 * LANES]], axis=0)
        k_rows = pl.ds(window_start(j), width)
        k_win = jnp.concatenate(
            [kx_ref[res, k_rows, t * LANES:(t + 1) * LANES]
             for t in range(g * HEADS_PER_TILE, (g + 1) * HEADS_PER_TILE)], axis=0)
        return lax.dot_general(q2, k_win, (((1,), (1,)), ((), ())), preferred_element_type=jnp.float32)

    work = [(res, j, g) for res in range(n_res) for j in range(blocks) for g in range(N_KV_HEADS)]
    s_next = scores(*work[0])
    for n, (res, j, g) in enumerate(work):
        base = g * Q_PER_KV * HEAD_DIM
        rows = slice(j * tq, (j + 1) * tq)
        blk = i * blocks + j
        if n_blk == 1:
            variant = 0
        else:
            variant = jnp.where(blk == 0, 0, jnp.where(blk == n_blk - 1, 2, 1))
        if dil > 1:
            token_rows = pl.ds(blk * (tq * dil) + pl.program_id(1) * n_res + res, tq, stride=dil)
        s_all = s_next
        if n + 1 < len(work):
            s_next = scores(*work[n + 1])
        probs = []
        maxes = []
        sink_terms = []
        for v in range(HEADS_PER_TILE):
            t = g * HEADS_PER_TILE + v
            s = s_all[:, v * width:(v + 1) * width] + bias_ref[variant, t * 2 * tq:(t + 1) * 2 * tq, :]
            m = jnp.max(s, axis=-1, keepdims=True)
            if has_sink:
                h_top, h_bot = _dot_heads(g, v)
                sk = jnp.where(top_rows, sink_ref[h_top], sink_ref[h_bot])
                m = jnp.maximum(m, sk)
                sink_terms.append(jnp.exp(sk - m))
            maxes.append(m)
            probs.append(jnp.exp(s - m).astype(bf16))
        v_rows = pl.ds(window_start(j), width)
        v_rhs = jnp.concatenate(
            [jnp.concatenate([vx_ref[res, v_rows, (2 * g) * LANES:(2 * g + 1) * LANES], ones_lo], axis=1),
             jnp.concatenate([vx_ref[res, v_rows, (2 * g + 1) * LANES:(2 * g + 2) * LANES], ones_hi], axis=1)],
            axis=0)
        acc = jnp.dot(jnp.concatenate(probs, axis=1), v_rhs, preferred_element_type=jnp.float32)
        denom = acc[:, LANES:]
        if has_sink:
            denom = denom + jnp.where(low_half2, sink_terms[0], sink_terms[1])
        o = acc[:, :LANES] / denom
        if dil == 1:
            o_ref[rows, base:base + LANES] = o[:tq].astype(o_ref.dtype)
            o_ref[rows, base + LANES:base + 2 * LANES] = o[tq:].astype(o_ref.dtype)
        else:
            o_ref[g * HEADS_PER_TILE, token_rows, :] = o[:tq]
            o_ref[g * HEADS_PER_TILE + 1, token_rows, :] = o[tq:]
        if want_lse:
            if g == 0:
                lse_acc = jnp.zeros((tq, LANES), jnp.float32)
            lse = jnp.where(low_half2, maxes[0], maxes[1]) + jnp.log(denom)
            for half, half_rows in ((0, slice(0, tq)), (1, slice(tq, 2 * tq))):
                tile = g * HEADS_PER_TILE + half
                here = (lane == tile) | (lane == HEAD_DIM + tile)
                lse_acc = jnp.where(here, lse[half_rows], lse_acc)
            if g == N_KV_HEADS - 1:
                if dil == 1:
                    lse_ref[rows, :] = lse_acc
                else:
                    lse_ref[token_rows, :] = lse_acc


def _band_attention(qkv, bias, radius, *, sink=None, want_lse=False):
    q, kx, vx = qkv
    B, dil, L, _ = q.shape
    S = L * dil
    tq = ATTN_TQ
    ts = min(L, ATTN_STEP_ROWS)
    n_res = min(dil, ATTN_STEP_ROWS // ts)
    D = Q_COLS

    in_specs = [
        pl.BlockSpec((None, n_res, ts, Q_COLS), lambda b, r, i: (b, r, i, 0)),
        pl.BlockSpec((None, n_res, L, KVX_COLS), lambda b, r, i: (b, r, 0, 0)),
        pl.BlockSpec((None, n_res, L, KVX_COLS), lambda b, r, i: (b, r, 0, 0)),
        _const_spec(bias.shape),
    ]
    args = [q, kx, vx, bias]
    if sink is not None:
        in_specs.append(pl.BlockSpec(memory_space=pltpu.SMEM))
        args.append(sink.astype(jnp.float32))
    if dil == 1:
        out_specs = [pl.BlockSpec((None, ts, D), lambda b, r, i: (b, i, 0))]
        out_shape = [jax.ShapeDtypeStruct((B, S, D), jnp.bfloat16)]
        lse_spec = pl.BlockSpec((None, ts, LANES), lambda b, r, i: (b, i, 0))
    else:
        assert want_lse
        out_specs = [pl.BlockSpec((None, D // LANES, S, LANES), lambda b, r, i: (b, 0, 0, 0))]
        out_shape = [jax.ShapeDtypeStruct((B, D // LANES, S, LANES), jnp.float32)]
        lse_spec = pl.BlockSpec((None, S, LANES), lambda b, r, i: (b, 0, 0))
    if want_lse:
        out_specs.append(lse_spec)
        out_shape.append(jax.ShapeDtypeStruct((B, S, LANES), jnp.float32))

    kernel = functools.partial(_attn_kernel, tq=tq, radius=radius, seq_len=L, dil=dil,
                               has_sink=sink is not None, want_lse=want_lse)
    out = pl.pallas_call(
        kernel,
        grid=(B, dil // n_res, L // ts),
        in_specs=in_specs,
        out_specs=out_specs,
        out_shape=out_shape,
        compiler_params=_params(3),
        name=f"band_attn_r{radius}_d{dil}",
    )(*args)
    if want_lse:
        return out[0], out[1]
    return out[0]


def _head_expand_matrix():
    e = np.zeros((LANES, Q_COLS), np.float32)
    for h in range(N_HEADS):
        e[_lse_lane(h), h * HEAD_DIM:(h + 1) * HEAD_DIM] = 1.0
    return jnp.asarray(np.concatenate([e, e], axis=0), jnp.bfloat16)


def _read_rows_f32(o_ref):
    if len(o_ref.shape) == 2:
        return o_ref[...].astype(jnp.float32)
    return jnp.concatenate([o_ref[j] for j in range(o_ref.shape[0])], axis=1)


def _ffn_kernel(*refs, n_mix, final):
    refs = list(refs)
    x_ref = refs[0]
    pos = 1
    x = x_ref[...]
    if n_mix >= 1:
        o_refs = refs[pos:pos + n_mix]
        pos += n_mix
        if n_mix > 1:
            lse_refs = refs[pos:pos + n_mix]
            expand_ref = refs[pos + n_mix]
            pos += n_mix + 1
        wo_ref = refs[pos]
        pos += 1
        if n_mix == 1:
            o = o_refs[0][...]
        else:
            lses = [r[...] for r in lse_refs]
            top = functools.reduce(jnp.maximum, lses)
            es = [jnp.exp(l - top) for l in lses]
            inv_tot = 1.0 / functools.reduce(lambda a, b: a + b, es)
            o = None
            for e, o_ref in zip(es, o_refs):
                w = e * inv_tot
                w_hi = w.astype(jnp.bfloat16)
                w_lo = (w - w_hi.astype(jnp.float32)).astype(jnp.bfloat16)
                w_full = jnp.dot(jnp.concatenate([w_hi, w_lo], axis=1), expand_ref[...],
                                 preferred_element_type=jnp.float32)
                term = w_full * _read_rows_f32(o_ref)
                o = term if o is None else o + term
            o = o.astype(jnp.bfloat16)
        x = x + jnp.dot(o, wo_ref[...], preferred_element_type=jnp.float32)
    g_ref, w1_ref, w2_ref = refs[pos:pos + 3]
    pos += 3
    if final:
        fg_ref = refs[pos]
        pos += 1
    out_ref = refs[pos]

    h = _rmsnorm_f32(x, g_ref[...]).astype(jnp.bfloat16)
    acc = x
    d_ff = w1_ref.shape[1]
    for c in range(d_ff // FFN_CHUNK):
        cols = slice(c * FFN_CHUNK, (c + 1) * FFN_CHUNK)
        a = jnp.dot(h, w1_ref[:, cols], preferred_element_type=jnp.float32)
        a = jnp.square(jnp.maximum(a, 0.0)).astype(jnp.bfloat16)
        acc = acc + jnp.dot(a, w2_ref[cols, :], preferred_element_type=jnp.float32)
    if final:
        acc = _rmsnorm_f32(acc, fg_ref[...])
    out_ref[...] = acc


def _ffn(x, g, w1, w2, *, mix=None, final_g=None):
    B, S, D = x.shape
    d_ff = w1.shape[1]
    tm = FFN_TM
    row_spec = lambda c: pl.BlockSpec((None, tm, c), lambda b, i: (b, i, 0))
    slab_spec = lambda c: pl.BlockSpec((None, c // LANES, tm, LANES), lambda b, i: (b, 0, i, 0))
    in_specs = [row_spec(D)]
    args = [x]
    n_mix = 0
    if mix is not None:
        o_list, lse_list, w_o = mix
        n_mix = len(o_list)
        in_specs += [row_spec(D) if o.ndim == 3 else slab_spec(D) for o in o_list]
        args += o_list
        if n_mix > 1:
            in_specs += [row_spec(LANES)] * n_mix + [_const_spec((2 * LANES, Q_COLS))]
            args += lse_list + [_head_expand_matrix()]
        in_specs.append(_const_spec((D, D)))
        args.append(w_o)
    in_specs += [_const_spec((1, D)), _const_spec((D, d_ff)), _const_spec((d_ff, D))]
    args += [g.reshape(1, D), w1, w2]
    if final_g is not None:
        in_specs.append(_const_spec((1, D)))
        args.append(final_g.reshape(1, D))
    out = pl.pallas_call(
        functools.partial(_ffn_kernel, n_mix=n_mix, final=final_g is not None),
        grid=(B, S // tm),
        in_specs=in_specs,
        out_specs=row_spec(D),
        out_shape=jax.ShapeDtypeStruct((B, S, D), jnp.float32),
        compiler_params=_params(2),
        name=f"ffn_mix{n_mix}" + ("_final" if final_g is not None else ""),
    )(*args)
    return out


def _gmlp_kernel(x_ref, g_ref, win_ref, lng_ref, lnb_ref, ws_ref, bs_ref, wout_ref, o_ref):
    tm = x_ref.shape[0]
    hidden = wout_ref.shape[0]
    gcols = hidden // B_GROUPS
    subs = [slice(r0, r0 + GMLP_SUB) for r0 in range(0, tm, GMLP_SUB)]
    zs = []
    for rows in subs:
        h = _rmsnorm_f32(x_ref[rows, :], g_ref[...]).astype(jnp.bfloat16)
        zs.append(jnp.dot(h, win_ref[...], preferred_element_type=jnp.float32))
    for rows, z in zip(subs, zs):
        z = 0.5 * z * (1.0 + lax.erf(z * (2.0 ** -0.5)))
        u = z[:, :hidden]
        v = z[:, hidden:]
        mu = jnp.mean(v, axis=-1, keepdims=True)
        vc = v - mu
        var = jnp.mean(vc * vc, axis=-1, keepdims=True)
        vn = (vc * lax.rsqrt(var + LN_EPS) * lng_ref[...] + lnb_ref[...]).astype(jnp.bfloat16)
        gated = []
        for r0 in range(0, GMLP_SUB, B_CHUNK):
            parts = []
            for grp in range(B_GROUPS):
                parts.append(jnp.dot(ws_ref[grp], vn[r0:r0 + B_CHUNK, grp * gcols:(grp + 1) * gcols],
                                     preferred_element_type=jnp.float32))
            mixed = jnp.concatenate(parts, axis=-1) + bs_ref[...]
            gated.append((u[r0:r0 + B_CHUNK, :] * mixed).astype(jnp.bfloat16))
        t = jnp.concatenate(gated, axis=0)
        o_ref[rows, :] = x_ref[rows, :] + jnp.dot(t, wout_ref[...], preferred_element_type=jnp.float32)


def _gmlp(x, g, w_in, ln_g, ln_b, w_s, b_s, w_out):
    B, S, D = x.shape
    hidden = w_out.shape[0]
    T = B * S
    tm = GMLP_TM
    bs_full = jnp.repeat(jnp.transpose(b_s).astype(jnp.float32), hidden // B_GROUPS, axis=1)
    row_spec = pl.BlockSpec((tm, D), lambda i: (i, 0))
    out = pl.pallas_call(
        _gmlp_kernel,
        grid=(T // tm,),
        in_specs=[row_spec, _const_spec((1, D)), _const_spec((D, 2 * hidden)),
                  _const_spec((1, hidden)), _const_spec((1, hidden)),
                  _const_spec((B_GROUPS, B_CHUNK, B_CHUNK)), _const_spec((B_CHUNK, hidden)),
                  _const_spec((hidden, D))],
        out_specs=row_spec,
        out_shape=jax.ShapeDtypeStruct((T, D), jnp.float32),
        compiler_params=_params(1),
        name="gmlp",
    )(x.reshape(T, D), g.reshape(1, D), w_in, ln_g.reshape(1, hidden).astype(jnp.float32),
      ln_b.reshape(1, hidden).astype(jnp.float32), w_s, bs_full, w_out)
    return out.reshape(B, S, D)


def _trunk(x, p):
    depth = p["norm_mix_g"].shape[0]
    for i in range(depth):
        kind, j = i % N_MIXERS, i // N_MIXERS
        g_mix = p["norm_mix_g"][i]
        mix = None
        if kind == 0:
            (qkv,) = _qkv_projection(x, g_mix, [p["a_wqkv"][j]], [1])
            o = _band_attention(qkv, p["a_bias"], A_RADIUS, sink=p["a_sink"][j])
            mix = ([o], None, p["a_wo"][j])
        elif kind == 1:
            x = _gmlp(x, g_mix, p["b_win"][j], p["b_ln_g"][j], p["b_ln_b"][j], p["b_ws"][j],
                      p["b_bs"][j], p["b_wo"][j])
        else:
            o_list, lse_list = [], []
            dils = [dil for _, dil in C_GROUPS]
            weights = [p["c_wqkv"][j][:, gi * QKV_COLS:(gi + 1) * QKV_COLS] for gi in range(len(dils))]
            qkvs = _qkv_projection(x, g_mix, weights, dils)
            for gi, (window, dil) in enumerate(C_GROUPS):
                o, lse = _band_attention(qkvs[gi], p["c_bias"][gi], window // (2 * dil), want_lse=True)
                o_list.append(o)
                lse_list.append(lse)
            mix = (o_list, lse_list, p["c_wo"][j])
        final_g = p["final_g"] if i == depth - 1 else None
        x = _ffn(x, p["norm_ffn_g"][i], p["ffn_w1"][i], p["ffn_w2"][i], mix=mix, final_g=final_g)
    return x


def kernel(x_prompt, x_sample, rel_bias, norm_mix_g, norm_ffn_g, final_g, ffn_w1, ffn_w2, a_wqkv,
           a_sink, a_wo, b_win, b_ln_g, b_ln_b, b_ws, b_bs, b_wo, c_wqkv, c_wo):
    bf16 = jnp.bfloat16
    f32 = jnp.float32
    S = x_prompt.shape[1]
    assert x_sample.shape[1] == S
    p = {
        "norm_mix_g": norm_mix_g.astype(f32), "norm_ffn_g": norm_ffn_g.astype(f32),
        "final_g": final_g.astype(f32),
        "ffn_w1": ffn_w1.astype(bf16), "ffn_w2": ffn_w2.astype(bf16),
        "a_wqkv": a_wqkv.astype(bf16), "a_sink": a_sink, "a_wo": a_wo.astype(bf16),
        "b_win": b_win.astype(bf16), "b_ln_g": b_ln_g, "b_ln_b": b_ln_b,
        "b_ws": b_ws.astype(bf16), "b_bs": b_bs, "b_wo": b_wo.astype(bf16),
        "c_wqkv": c_wqkv.astype(bf16), "c_wo": c_wo.astype(bf16),
        "a_bias": _band_bias_table(rel_bias, ATTN_TQ, A_RADIUS, 1, S),
        "c_bias": [_band_bias_table(rel_bias, ATTN_TQ, window // (2 * dil), dil, S // dil)
                   for window, dil in C_GROUPS],
    }
    return _trunk(x_prompt, p), _trunk(x_sample, p)
```

```python
import functools
import math

import numpy as np
import jax
import jax.numpy as jnp
from jax import lax
from jax.experimental import pallas as pl
from jax.experimental.pallas import tpu as pltpu

HEAD_DIM = 64
N_MIXERS = 3
N_HEADS = 16
N_KV_HEADS = 4
A_RADIUS = 128
B_CHUNK = 128
B_GROUPS = 8
C_GROUPS = ((128, 1), (512, 4), (2048, 16))
NUM_BUCKETS = 32
REL_MAX_DISTANCE = 1024
RMS_EPS = 1e-6
LN_EPS = 1e-5

Q_COLS = N_HEADS * HEAD_DIM
KV_COLS = N_KV_HEADS * HEAD_DIM
QKV_COLS = Q_COLS + 2 * KV_COLS
LANES = 128
HEADS_PER_TILE = LANES // HEAD_DIM
Q_PER_KV = N_HEADS // N_KV_HEADS
KVX_COLS = N_KV_HEADS * HEADS_PER_TILE * LANES
MASK_VALUE = -1e30
LOG2_E = math.log2(math.e)
LN_2 = math.log(2.0)

V7X_VMEM_BYTES = 64 * 1024 * 1024
VMEM_LIMIT_BYTES = V7X_VMEM_BYTES - 8 * 1024 * 1024

ATTN_TQ = 128
ATTN_STEP_ROWS = 512
FFN_TM = 512
FFN_CHUNK = 1024
PROJ_TM = 1024
PROJ_TM_MULTI = 512
GMLP_TM = 512
GMLP_SUB = 256


def _const_spec(shape):
    zeros = (0,) * len(shape)
    return pl.BlockSpec(shape, lambda *_: zeros, pipeline_mode=pl.Buffered(1))


def _params(n_grid_dims):
    return pltpu.CompilerParams(
        dimension_semantics=("arbitrary",) * n_grid_dims,
        vmem_limit_bytes=VMEM_LIMIT_BYTES,
    )


def _rmsnorm_f32(x, g):
    ms = jnp.mean(x * x, axis=-1, keepdims=True)
    return x * lax.rsqrt(ms + RMS_EPS) * g


def _lse_lane(head):
    return head // HEADS_PER_TILE + HEAD_DIM * (head % HEADS_PER_TILE)


def _proj_kernel(*refs, dils):
    n = len(dils)
    x_ref, g_ref = refs[:2]
    w_refs = refs[2:2 + n]
    out_refs = refs[2 + n:2 + 4 * n]
    hn = _rmsnorm_f32(x_ref[...], g_ref[...])
    tm, d_model = hn.shape
    n_slabs = d_model // LANES
    if any(d > 1 for d in dils):
        slab_ref = refs[2 + 4 * n]
        for j in range(n_slabs):
            slab_ref[j] = hn[:, j * LANES:(j + 1) * LANES]
    low_half = lax.broadcasted_iota(jnp.int32, (tm, LANES), 1) < HEAD_DIM
    for gi, (dil, w_ref) in enumerate(zip(dils, w_refs)):
        q_ref, kx_ref, vx_ref = out_refs[3 * gi:3 * gi + 3]
        n_rows = tm // dil
        if dil == 1:
            h = hn
        else:
            h = jnp.concatenate(
                [jnp.concatenate([slab_ref[j, pl.ds(r, n_rows, stride=dil), :] for j in range(n_slabs)], axis=1)
                 for r in range(dil)], axis=0)
        y = jnp.dot(h.astype(jnp.bfloat16), w_ref[...], preferred_element_type=jnp.float32)
        pieces = [(q_ref, (y[:, :Q_COLS] * (HEAD_DIM ** -0.5 * LOG2_E)).astype(q_ref.dtype))]
        for o_ref, col0 in ((kx_ref, Q_COLS), (vx_ref, Q_COLS + KV_COLS)):
            tiles = []
            for t in range(KV_COLS // LANES):
                pair = y[:, col0 + t * LANES:col0 + (t + 1) * LANES]
                only_a = jnp.where(low_half, pair, 0.0)
                only_b = jnp.where(low_half, 0.0, pair)
                tiles += [only_a, pltpu.roll(only_a, HEAD_DIM, 1), pltpu.roll(only_b, HEAD_DIM, 1), only_b]
            pieces.append((o_ref, jnp.concatenate(tiles, axis=1).astype(o_ref.dtype)))
        for o_ref, val in pieces:
            for r in range(dil):
                o_ref[r] = val[r * n_rows:(r + 1) * n_rows]


def _qkv_projection(x, g, weights, dils):
    B, S, D = x.shape
    n = len(dils)
    tm = PROJ_TM if n == 1 else PROJ_TM_MULTI
    in_specs = [pl.BlockSpec((None, tm, D), lambda b, i: (b, i, 0)), _const_spec((1, D))]
    in_specs += [_const_spec((D, QKV_COLS))] * n
    out_specs, out_shape = [], []
    for dil in dils:
        for cols in (Q_COLS, KVX_COLS, KVX_COLS):
            out_specs.append(pl.BlockSpec((None, dil, tm // dil, cols), lambda b, i: (b, 0, i, 0)))
            out_shape.append(jax.ShapeDtypeStruct((B, dil, S // dil, cols), jnp.bfloat16))
    scratch = []
    if any(d > 1 for d in dils):
        scratch.append(pltpu.VMEM((D // LANES, tm, LANES), jnp.float32))
    outs = pl.pallas_call(
        functools.partial(_proj_kernel, dils=tuple(dils)),
        grid=(B, S // tm),
        in_specs=in_specs,
        out_specs=out_specs,
        out_shape=out_shape,
        scratch_shapes=scratch,
        compiler_params=_params(2),
        name="qkv_proj_" + "_".join(f"d{d}" for d in dils),
    )(x, g.reshape(1, D), *weights)
    return [tuple(outs[3 * gi:3 * gi + 3]) for gi in range(n)]


def _rel_bucket(rel):
    half = NUM_BUCKETS // 2
    max_exact = half // 2
    n = np.abs(rel)
    large = max_exact + (np.log(np.maximum(n, 1) / max_exact) / np.log(REL_MAX_DISTANCE / max_exact)
                         * (half - max_exact)).astype(np.int32)
    large = np.minimum(large, half - 1)
    return (rel > 0).astype(np.int32) * half + np.where(n < max_exact, n, large)


def _window_geometry(tq, radius, seq_len):
    width = min(tq + 2 * radius, seq_len)
    if seq_len // tq == 1:
        return width, (0,)
    return width, (0, radius, width - tq)


def _dot_heads(g, v):
    return g * Q_PER_KV + v, g * Q_PER_KV + HEADS_PER_TILE + v


def _band_bias_table(rel_bias, tq, radius, dil, seq_len):
    width, offsets = _window_geometry(tq, radius, seq_len)
    span = max(offsets) + tq - 1
    n_diag = span + width
    diag_rel = np.arange(n_diag) - span
    per_diag = jnp.transpose(rel_bias[_rel_bucket(diag_rel * dil)], (1, 0)).astype(jnp.float32) * LOG2_E
    per_diag = jnp.where(jnp.asarray(np.abs(diag_rel) <= radius)[None], per_diag, MASK_VALUE)
    padded = jnp.pad(per_diag, ((0, 0), (0, 1)))
    tiled = jnp.broadcast_to(padded[:, None, :], (N_HEADS, tq, n_diag + 1)).reshape(N_HEADS, -1)
    skewed = tiled[:, :tq * n_diag].reshape(N_HEADS, tq, n_diag)
    order = [h for g in range(N_KV_HEADS) for v in range(HEADS_PER_TILE) for h in _dot_heads(g, v)]
    skewed = skewed[np.asarray(order)]
    tables = []
    for off in offsets:
        c0 = span - off
        tables.append(skewed[:, :, c0:c0 + width].reshape(N_HEADS * tq, width))
    return jnp.stack(tables, axis=0)


def _attn_kernel(*refs, tq, radius, seq_len, dil, has_sink, want_lse):
    refs = list(refs)
    q_ref, kx_ref, vx_ref, bias_ref = refs[:4]
    pos = 4
    sink_ref = None
    if has_sink:
        sink_ref = refs[pos]
        pos += 1
    o_ref = refs[pos]
    pos += 1
    lse_ref = refs[pos] if want_lse else None

    n_res = q_ref.shape[0]
    blocks = q_ref.shape[1] // tq
    width, offsets = _window_geometry(tq, radius, seq_len)
    n_blk = seq_len // tq
    i = pl.program_id(2)
    bf16 = jnp.bfloat16

    lane = lax.broadcasted_iota(jnp.int32, (tq, LANES), 1)
    low_half2 = lax.broadcasted_iota(jnp.int32, (2 * tq, LANES), 1) < HEAD_DIM
    top_rows = lax.broadcasted_iota(jnp.int32, (2 * tq, 1), 0) < tq
    ones_lo = jnp.where(lax.broadcasted_iota(jnp.int32, (width, LANES), 1) < HEAD_DIM, 1.0, 0.0).astype(bf16)
    ones_hi = jnp.where(lax.broadcasted_iota(jnp.int32, (width, LANES), 1) < HEAD_DIM, 0.0, 1.0).astype(bf16)

    def window_start(j):
        blk = i * blocks + j
        if n_blk == 1:
            return 0
        start = jnp.clip(blk * tq - radius, 0, seq_len - width)
        return pl.multiple_of(start, HEAD_DIM)

    def scores(res, j, g):
        base = g * Q_PER_KV * HEAD_DIM
        rows = slice(j * tq, (j + 1) * tq)
        q2 = jnp.concatenate([q_ref[res, rows, base:base + LANES],
                              q_ref[res, rows, base + LANES:base + 2 * LANES]], axis=0)
        k_rows = pl.ds(window_start(j), width)
        k_win = jnp.concatenate(
            [kx_ref[res, k_rows, t * LANES:(t + 1) * LANES]
             for t in range(g * HEADS_PER_TILE, (g + 1) * HEADS_PER_TILE)], axis=0)
        return lax.dot_general(q2, k_win, (((1,), (1,)), ((), ())), preferred_element_type=jnp.float32)

    work = [(res, j, g) for res in range(n_res) for j in range(blocks) for g in range(N_KV_HEADS)]
    s_next = scores(*work[0])
    for n, (res, j, g) in enumerate(work):
        base = g * Q_PER_KV * HEAD_DIM
        rows = slice(j * tq, (j + 1) * tq)
        blk = i * blocks + j
        if n_blk == 1:
            variant = 0
        else:
            variant = jnp.where(blk == 0, 0, jnp.where(blk == n_blk - 1, 2, 1))
        if dil > 1:
            token_rows = pl.ds(blk * (tq * dil) + pl.program_id(1) * n_res + res, tq, stride=dil)
        s_all = s_next
        if n + 1 < len(work):
            s_next = scores(*work[n + 1])
        probs = []
        maxes = []
        sink_terms = []
        for v in range(HEADS_PER_TILE):
            t = g * HEADS_PER_TILE + v
            s = s_all[:, v * width:(v + 1) * width] + bias_ref[variant, t * 2 * tq:(t + 1) * 2 * tq, :]
            m = jnp.max(s, axis=-1, keepdims=True)
            if has_sink:
                h_top, h_bot = _dot_heads(g, v)
                sk = jnp.where(top_rows, sink_ref[h_top], sink_ref[h_bot])
                m = jnp.maximum(m, sk)
                sink_terms.append(jnp.exp2(sk - m))
            maxes.append(m)
            probs.append(jnp.exp2(s - m).astype(bf16))
        v_rows = pl.ds(window_start(j), width)
        v_rhs = jnp.concatenate(
            [jnp.concatenate([vx_ref[res, v_rows, (2 * g) * LANES:(2 * g + 1) * LANES], ones_lo], axis=1),
             jnp.concatenate([vx_ref[res, v_rows, (2 * g + 1) * LANES:(2 * g + 2) * LANES], ones_hi], axis=1)],
            axis=0)
        acc = jnp.dot(jnp.concatenate(probs, axis=1), v_rhs, preferred_element_type=jnp.float32)
        denom = acc[:, LANES:]
        if has_sink:
            denom = denom + jnp.where(low_half2, sink_terms[0], sink_terms[1])
        o = acc[:, :LANES] / denom
        if dil == 1:
            o_ref[rows, base:base + LANES] = o[:tq].astype(o_ref.dtype)
            o_ref[rows, base + LANES:base + 2 * LANES] = o[tq:].astype(o_ref.dtype)
        else:
            o_ref[g * HEADS_PER_TILE, token_rows, :] = o[:tq]
            o_ref[g * HEADS_PER_TILE + 1, token_rows, :] = o[tq:]
        if want_lse:
            if g == 0:
                lse_acc = jnp.zeros((tq, LANES), jnp.float32)
            lse = (jnp.where(low_half2, maxes[0], maxes[1]) + jnp.log2(denom)) * LN_2
            for half, half_rows in ((0, slice(0, tq)), (1, slice(tq, 2 * tq))):
                tile = g * HEADS_PER_TILE + half
                here = (lane == tile) | (lane == HEAD_DIM + tile)
                lse_acc = jnp.where(here, lse[half_rows], lse_acc)
            if g == N_KV_HEADS - 1:
                if dil == 1:
                    lse_ref[rows, :] = lse_acc
                else:
                    lse_ref[token_rows, :] = lse_acc


def _band_attention(qkv, bias, radius, *, sink=None, want_lse=False):
    q, kx, vx = qkv
    B, dil, L, _ = q.shape
    S = L * dil
    tq = ATTN_TQ
    ts = min(L, ATTN_STEP_ROWS)
    n_res = min(dil, ATTN_STEP_ROWS // ts)
    D = Q_COLS

    in_specs = [
        pl.BlockSpec((None, n_res, ts, Q_COLS), lambda b, r, i: (b, r, i, 0)),
        pl.BlockSpec((None, n_res, L, KVX_COLS), lambda b, r, i: (b, r, 0, 0)),
        pl.BlockSpec((None, n_res, L, KVX_COLS), lambda b, r, i: (b, r, 0, 0)),
        _const_spec(bias.shape),
    ]
    args = [q, kx, vx, bias]
    if sink is not None:
        in_specs.append(pl.BlockSpec(memory_space=pltpu.SMEM))
        args.append(sink.astype(jnp.float32) * LOG2_E)
    if dil == 1:
        out_specs = [pl.BlockSpec((None, ts, D), lambda b, r, i: (b, i, 0))]
        out_shape = [jax.ShapeDtypeStruct((B, S, D), jnp.bfloat16)]
        lse_spec = pl.BlockSpec((None, ts, LANES), lambda b, r, i: (b, i, 0))
    else:
        assert want_lse
        out_specs = [pl.BlockSpec((None, D // LANES, S, LANES), lambda b, r, i: (b, 0, 0, 0))]
        out_shape = [jax.ShapeDtypeStruct((B, D // LANES, S, LANES), jnp.float32)]
        lse_spec = pl.BlockSpec((None, S, LANES), lambda b, r, i: (b, 0, 0))
    if want_lse:
        out_specs.append(lse_spec)
        out_shape.append(jax.ShapeDtypeStruct((B, S, LANES), jnp.float32))

    kernel = functools.partial(_attn_kernel, tq=tq, radius=radius, seq_len=L, dil=dil,
                               has_sink=sink is not None, want_lse=want_lse)
    out = pl.pallas_call(
        kernel,
        grid=(B, dil // n_res, L // ts),
        in_specs=in_specs,
        out_specs=out_specs,
        out_shape=out_shape,
        compiler_params=_params(3),
        name=f"band_attn_r{radius}_d{dil}",
    )(*args)
    if want_lse:
        return out[0], out[1]
    return out[0]


def _head_expand_matrix():
    e = np.zeros((LANES, Q_COLS), np.float32)
    for h in range(N_HEADS):
        e[_lse_lane(h), h * HEAD_DIM:(h + 1) * HEAD_DIM] = 1.0
    return jnp.asarray(np.concatenate([e, e], axis=0), jnp.bfloat16)


def _read_rows_f32(o_ref):
    if len(o_ref.shape) == 2:
        return o_ref[...].astype(jnp.float32)
    return jnp.concatenate([o_ref[j] for j in range(o_ref.shape[0])], axis=1)


def _ffn_kernel(*refs, n_mix, final):
    refs = list(refs)
    x_ref = refs[0]
    pos = 1
    x = x_ref[...]
    if n_mix >= 1:
        o_refs = refs[pos:pos + n_mix]
        pos += n_mix
        if n_mix > 1:
            lse_refs = refs[pos:pos + n_mix]
            expand_ref = refs[pos + n_mix]
            pos += n_mix + 1
        wo_ref = refs[pos]
        pos += 1
        if n_mix == 1:
            o = o_refs[0][...]
        else:
            lses = [r[...] for r in lse_refs]
            top = functools.reduce(jnp.maximum, lses)
            es = [jnp.exp(l - top) for l in lses]
            inv_tot = 1.0 / functools.reduce(lambda a, b: a + b, es)
            o = None
            for e, o_ref in zip(es, o_refs):
                w = e * inv_tot
                w_hi = w.astype(jnp.bfloat16)
                w_lo = (w - w_hi.astype(jnp.float32)).astype(jnp.bfloat16)
                w_full = jnp.dot(jnp.concatenate([w_hi, w_lo], axis=1), expand_ref[...],
                                 preferred_element_type=jnp.float32)
                term = w_full * _read_rows_f32(o_ref)
                o = term if o is None else o + term
            o = o.astype(jnp.bfloat16)
        x = x + jnp.dot(o, wo_ref[...], preferred_element_type=jnp.float32)
    g_ref, w1_ref, w2_ref = refs[pos:pos + 3]
    pos += 3
    if final:
        fg_ref = refs[pos]
        pos += 1
    out_ref = refs[pos]

    h = _rmsnorm_f32(x, g_ref[...]).astype(jnp.bfloat16)
    acc = x
    d_ff = w1_ref.shape[1]
    for c in range(d_ff // FFN_CHUNK):
        cols = slice(c * FFN_CHUNK, (c + 1) * FFN_CHUNK)
        a = jnp.dot(h, w1_ref[:, cols], preferred_element_type=jnp.float32)
        a = jnp.square(jnp.maximum(a, 0.0)).astype(jnp.bfloat16)
        acc = acc + jnp.dot(a, w2_ref[cols, :], preferred_element_type=jnp.float32)
    if final:
        acc = _rmsnorm_f32(acc, fg_ref[...])
    out_ref[...] = acc


def _ffn(x, g, w1, w2, *, mix=None, final_g=None):
    B, S, D = x.shape
    d_ff = w1.shape[1]
    tm = FFN_TM
    row_spec = lambda c: pl.BlockSpec((None, tm, c), lambda b, i: (b, i, 0))
    slab_spec = lambda c: pl.BlockSpec((None, c // LANES, tm, LANES), lambda b, i: (b, 0, i, 0))
    in_specs = [row_spec(D)]
    args = [x]
    n_mix = 0
    if mix is not None:
        o_list, lse_list, w_o = mix
        n_mix = len(o_list)
        in_specs += [row_spec(D) if o.ndim == 3 else slab_spec(D) for o in o_list]
        args += o_list
        if n_mix > 1:
            in_specs += [row_spec(LANES)] * n_mix + [_const_spec((2 * LANES, Q_COLS))]
            args += lse_list + [_head_expand_matrix()]
        in_specs.append(_const_spec((D, D)))
        args.append(w_o)
    in_specs += [_const_spec((1, D)), _const_spec((D, d_ff)), _const_spec((d_ff, D))]
    args += [g.reshape(1, D), w1, w2]
    if final_g is not None:
        in_specs.append(_const_spec((1, D)))
        args.append(final_g.reshape(1, D))
    out = pl.pallas_call(
        functools.partial(_ffn_kernel, n_mix=n_mix, final=final_g is not None),
        grid=(B, S // tm),
        in_specs=in_specs,
        out_specs=row_spec(D),
        out_shape=jax.ShapeDtypeStruct((B, S, D), jnp.float32),
        compiler_params=_params(2),
        name=f"ffn_mix{n_mix}" + ("_final" if final_g is not None else ""),
    )(*args)
    return out


def _gmlp_kernel(x_ref, g_ref, win_ref, lng_ref, lnb_ref, ws_ref, bs_ref, wout_ref, o_ref):
    tm = x_ref.shape[0]
    hidden = wout_ref.shape[0]
    gcols = hidden // B_GROUPS
    subs = [slice(r0, r0 + GMLP_SUB) for r0 in range(0, tm, GMLP_SUB)]
    zs = []
    for rows in subs:
        h = _rmsnorm_f32(x_ref[rows, :], g_ref[...]).astype(jnp.bfloat16)
        zs.append(jnp.dot(h, win_ref[...], preferred_element_type=jnp.float32))
    for rows, z in zip(subs, zs):
        z = 0.5 * z * (1.0 + lax.erf(z * (2.0 ** -0.5)))
        u = z[:, :hidden]
        v = z[:, hidden:]
        mu = jnp.mean(v, axis=-1, keepdims=True)
        vc = v - mu
        var = jnp.mean(vc * vc, axis=-1, keepdims=True)
        vn = (vc * lax.rsqrt(var + LN_EPS) * lng_ref[...] + lnb_ref[...]).astype(jnp.bfloat16)
        gated = []
        for r0 in range(0, GMLP_SUB, B_CHUNK):
            parts = []
            for grp in range(B_GROUPS):
                parts.append(jnp.dot(ws_ref[grp], vn[r0:r0 + B_CHUNK, grp * gcols:(grp + 1) * gcols],
                                     preferred_element_type=jnp.float32))
            mixed = jnp.concatenate(parts, axis=-1) + bs_ref[...]
            gated.append((u[r0:r0 + B_CHUNK, :] * mixed).astype(jnp.bfloat16))
        t = jnp.concatenate(gated, axis=0)
        o_ref[rows, :] = x_ref[rows, :] + jnp.dot(t, wout_ref[...], preferred_element_type=jnp.float32)


def _gmlp(x, g, w_in, ln_g, ln_b, w_s, b_s, w_out):
    B, S, D = x.shape
    hidden = w_out.shape[0]
    T = B * S
    tm = GMLP_TM
    bs_full = jnp.repeat(jnp.transpose(b_s).astype(jnp.float32), hidden // B_GROUPS, axis=1)
    row_spec = pl.BlockSpec((tm, D), lambda i: (i, 0))
    out = pl.pallas_call(
        _gmlp_kernel,
        grid=(T // tm,),
        in_specs=[row_spec, _const_spec((1, D)), _const_spec((D, 2 * hidden)),
                  _const_spec((1, hidden)), _const_spec((1, hidden)),
                  _const_spec((B_GROUPS, B_CHUNK, B_CHUNK)), _const_spec((B_CHUNK, hidden)),
                  _const_spec((hidden, D))],
        out_specs=row_spec,
        out_shape=jax.ShapeDtypeStruct((T, D), jnp.float32),
        compiler_params=_params(1),
        name="gmlp",
    )(x.reshape(T, D), g.reshape(1, D), w_in, ln_g.reshape(1, hidden).astype(jnp.float32),
      ln_b.reshape(1, hidden).astype(jnp.float32), w_s, bs_full, w_out)
    return out.reshape(B, S, D)


def _trunk(x, p):
    depth = p["norm_mix_g"].shape[0]
    for i in range(depth):
        kind, j = i % N_MIXERS, i // N_MIXERS
        g_mix = p["norm_mix_g"][i]
        mix = None
        if kind == 0:
            (qkv,) = _qkv_projection(x, g_mix, [p["a_wqkv"][j]], [1])
            o = _band_attention(qkv, p["a_bias"], A_RADIUS, sink=p["a_sink"][j])
            mix = ([o], None, p["a_wo"][j])
        elif kind == 1:
            x = _gmlp(x, g_mix, p["b_win"][j], p["b_ln_g"][j], p["b_ln_b"][j], p["b_ws"][j],
                      p["b_bs"][j], p["b_wo"][j])
        else:
            o_list, lse_list = [], []
            dils = [dil for _, dil in C_GROUPS]
            weights = [p["c_wqkv"][j][:, gi * QKV_COLS:(gi + 1) * QKV_COLS] for gi in range(len(dils))]
            qkvs = _qkv_projection(x, g_mix, weights, dils)
            for gi, (window, dil) in enumerate(C_GROUPS):
                o, lse = _band_attention(qkvs[gi], p["c_bias"][gi], window // (2 * dil), want_lse=True)
                o_list.append(o)
                lse_list.append(lse)
            mix = (o_list, lse_list, p["c_wo"][j])
        final_g = p["final_g"] if i == depth - 1 else None
        x = _ffn(x, p["norm_ffn_g"][i], p["ffn_w1"][i], p["ffn_w2"][i], mix=mix, final_g=final_g)
    return x


def kernel(x_prompt, x_sample, rel_bias, norm_mix_g, norm_ffn_g, final_g, ffn_w1, ffn_w2, a_wqkv,
           a_sink, a_wo, b_win, b_ln_g, b_ln_b, b_ws, b_bs, b_wo, c_wqkv, c_wo):
    bf16 = jnp.bfloat16
    f32 = jnp.float32
    S = x_prompt.shape[1]
    assert x_sample.shape[1] == S
    p = {
        "norm_mix_g": norm_mix_g.astype(f32), "norm_ffn_g": norm_ffn_g.astype(f32),
        "final_g": final_g.astype(f32),
        "ffn_w1": ffn_w1.astype(bf16), "ffn_w2": ffn_w2.astype(bf16),
        "a_wqkv": a_wqkv.astype(bf16), "a_sink": a_sink, "a_wo": a_wo.astype(bf16),
        "b_win": b_win.astype(bf16), "b_ln_g": b_ln_g, "b_ln_b": b_ln_b,
        "b_ws": b_ws.astype(bf16), "b_bs": b_bs, "b_wo": b_wo.astype(bf16),
        "c_wqkv": c_wqkv.astype(bf16), "c_wo": c_wo.astype(bf16),
        "a_bias": _band_bias_table(rel_bias, ATTN_TQ, A_RADIUS, 1, S),
        "c_bias": [_band_bias_table(rel_bias, ATTN_TQ, window // (2 * dil), dil, S // dil)
                   for window, dil in C_GROUPS],
    }
    return _trunk(x_prompt, p), _trunk(x_sample, p)
```

```python
import functools
import math

import numpy as np
import jax
import jax.numpy as jnp
from jax import lax
from jax.experimental import pallas as pl
from jax.experimental.pallas import tpu as pltpu

HEAD_DIM = 64
N_MIXERS = 3
N_HEADS = 16
N_KV_HEADS = 4
A_RADIUS = 128
B_CHUNK = 128
B_GROUPS = 8
C_GROUPS = ((128, 1), (512, 4), (2048, 16))
NUM_BUCKETS = 32
REL_MAX_DISTANCE = 1024
RMS_EPS = 1e-6
LN_EPS = 1e-5

Q_COLS = N_HEADS * HEAD_DIM
KV_COLS = N_KV_HEADS * HEAD_DIM
QKV_COLS = Q_COLS + 2 * KV_COLS
LANES = 128
HEADS_PER_TILE = LANES // HEAD_DIM
Q_PER_KV = N_HEADS // N_KV_HEADS
KVX_COLS = N_KV_HEADS * HEADS_PER_TILE * LANES
MASK_VALUE = -1e30
LOG2_E = math.log2(math.e)
LN_2 = math.log(2.0)

V7X_VMEM_BYTES = 64 * 1024 * 1024
VMEM_LIMIT_BYTES = V7X_VMEM_BYTES - 8 * 1024 * 1024

ATTN_TQ = 128
ATTN_STEP_ROWS = 512
FFN_TM = 1024
FFN_TM_MERGE = 512
FFN_CHUNK = 1024
PROJ_TM = 1024
PROJ_TM_MULTI = 512
GMLP_TM = 512
GMLP_SUB = 256


def _const_spec(shape):
    zeros = (0,) * len(shape)
    return pl.BlockSpec(shape, lambda *_: zeros, pipeline_mode=pl.Buffered(1))


def _params(n_grid_dims):
    return pltpu.CompilerParams(
        dimension_semantics=("arbitrary",) * n_grid_dims,
        vmem_limit_bytes=VMEM_LIMIT_BYTES,
    )


def _rmsnorm_f32(x, g):
    ms = jnp.mean(x * x, axis=-1, keepdims=True)
    return x * lax.rsqrt(ms + RMS_EPS) * g


def _lse_lane(head):
    return head // HEADS_PER_TILE + HEAD_DIM * (head % HEADS_PER_TILE)


def _proj_kernel(*refs, dils):
    n = len(dils)
    x_ref, g_ref = refs[:2]
    w_refs = refs[2:2 + n]
    out_refs = refs[2 + n:2 + 4 * n]
    hn = _rmsnorm_f32(x_ref[...], g_ref[...])
    tm, d_model = hn.shape
    n_slabs = d_model // LANES
    if any(d > 1 for d in dils):
        slab_ref = refs[2 + 4 * n]
        for j in range(n_slabs):
            slab_ref[j] = hn[:, j * LANES:(j + 1) * LANES]
    low_half = lax.broadcasted_iota(jnp.int32, (tm, LANES), 1) < HEAD_DIM
    for gi, (dil, w_ref) in enumerate(zip(dils, w_refs)):
        q_ref, kx_ref, vx_ref = out_refs[3 * gi:3 * gi + 3]
        n_rows = tm // dil
        if dil == 1:
            h = hn
        else:
            h = jnp.concatenate(
                [jnp.concatenate([slab_ref[j, pl.ds(r, n_rows, stride=dil), :] for j in range(n_slabs)], axis=1)
                 for r in range(dil)], axis=0)
        hb = h.astype(jnp.bfloat16)
        y_kv = jnp.dot(hb, w_ref[:, Q_COLS:], preferred_element_type=jnp.float32)
        y_q = jnp.dot(hb, w_ref[:, :Q_COLS], preferred_element_type=jnp.float32)
        pieces = []
        for o_ref, col0 in ((kx_ref, 0), (vx_ref, KV_COLS)):
            tiles = []
            for t in range(KV_COLS // LANES):
                pair = y_kv[:, col0 + t * LANES:col0 + (t + 1) * LANES]
                only_a = jnp.where(low_half, pair, 0.0)
                only_b = jnp.where(low_half, 0.0, pair)
                tiles += [only_a, pltpu.roll(only_a, HEAD_DIM, 1), pltpu.roll(only_b, HEAD_DIM, 1), only_b]
            pieces.append((o_ref, jnp.concatenate(tiles, axis=1).astype(o_ref.dtype)))
        pieces.append((q_ref, (y_q * (HEAD_DIM ** -0.5 * LOG2_E)).astype(q_ref.dtype)))
        for o_ref, val in pieces:
            for r in range(dil):
                o_ref[r] = val[r * n_rows:(r + 1) * n_rows]


def _qkv_projection(x, g, weights, dils):
    B, S, D = x.shape
    n = len(dils)
    tm = PROJ_TM if n == 1 else PROJ_TM_MULTI
    in_specs = [pl.BlockSpec((None, tm, D), lambda b, i: (b, i, 0)), _const_spec((1, D))]
    in_specs += [_const_spec((D, QKV_COLS))] * n
    out_specs, out_shape = [], []
    for dil in dils:
        for cols in (Q_COLS, KVX_COLS, KVX_COLS):
            out_specs.append(pl.BlockSpec((None, dil, tm // dil, cols), lambda b, i: (b, 0, i, 0)))
            out_shape.append(jax.ShapeDtypeStruct((B, dil, S // dil, cols), jnp.bfloat16))
    scratch = []
    if any(d > 1 for d in dils):
        scratch.append(pltpu.VMEM((D // LANES, tm, LANES), jnp.float32))
    outs = pl.pallas_call(
        functools.partial(_proj_kernel, dils=tuple(dils)),
        grid=(B, S // tm),
        in_specs=in_specs,
        out_specs=out_specs,
        out_shape=out_shape,
        scratch_shapes=scratch,
        compiler_params=_params(2),
        name="qkv_proj_" + "_".join(f"d{d}" for d in dils),
    )(x, g.reshape(1, D), *weights)
    return [tuple(outs[3 * gi:3 * gi + 3]) for gi in range(n)]


def _rel_bucket(rel):
    half = NUM_BUCKETS // 2
    max_exact = half // 2
    n = np.abs(rel)
    large = max_exact + (np.log(np.maximum(n, 1) / max_exact) / np.log(REL_MAX_DISTANCE / max_exact)
                         * (half - max_exact)).astype(np.int32)
    large = np.minimum(large, half - 1)
    return (rel > 0).astype(np.int32) * half + np.where(n < max_exact, n, large)


def _window_geometry(tq, radius, seq_len):
    width = min(tq + 2 * radius, seq_len)
    if seq_len // tq == 1:
        return width, (0,)
    return width, (0, radius, width - tq)


def _dot_heads(g, v):
    return g * Q_PER_KV + v, g * Q_PER_KV + HEADS_PER_TILE + v


def _band_bias_table(rel_bias, tq, radius, dil, seq_len):
    width, offsets = _window_geometry(tq, radius, seq_len)
    span = max(offsets) + tq - 1
    n_diag = span + width
    diag_rel = np.arange(n_diag) - span
    per_diag = jnp.transpose(rel_bias[_rel_bucket(diag_rel * dil)], (1, 0)).astype(jnp.float32) * LOG2_E
    per_diag = jnp.where(jnp.asarray(np.abs(diag_rel) <= radius)[None], per_diag, MASK_VALUE)
    padded = jnp.pad(per_diag, ((0, 0), (0, 1)))
    tiled = jnp.broadcast_to(padded[:, None, :], (N_HEADS, tq, n_diag + 1)).reshape(N_HEADS, -1)
    skewed = tiled[:, :tq * n_diag].reshape(N_HEADS, tq, n_diag)
    order = [h for g in range(N_KV_HEADS) for v in range(HEADS_PER_TILE) for h in _dot_heads(g, v)]
    skewed = skewed[np.asarray(order)]
    tables = []
    for off in offsets:
        c0 = span - off
        tables.append(skewed[:, :, c0:c0 + width].reshape(N_HEADS * tq, width))
    return jnp.stack(tables, axis=0)


def _attn_kernel(*refs, tq, radius, seq_len, dil, has_sink, want_lse):
    refs = list(refs)
    q_ref, kx_ref, vx_ref, bias_ref = refs[:4]
    pos = 4
    sink_ref = None
    if has_sink:
        sink_ref = refs[pos]
        pos += 1
    o_ref = refs[pos]
    pos += 1
    lse_ref = refs[pos] if want_lse else None

    n_res = q_ref.shape[0]
    blocks = q_ref.shape[1] // tq
    width, offsets = _window_geometry(tq, radius, seq_len)
    n_blk = seq_len // tq
    i = pl.program_id(2)
    bf16 = jnp.bfloat16

    lane = lax.broadcasted_iota(jnp.int32, (tq, LANES), 1)
    low_half2 = lax.broadcasted_iota(jnp.int32, (2 * tq, LANES), 1) < HEAD_DIM
    top_rows = lax.broadcasted_iota(jnp.int32, (2 * tq, 1), 0) < tq
    ones_lo = jnp.where(lax.broadcasted_iota(jnp.int32, (width, LANES), 1) < HEAD_DIM, 1.0, 0.0).astype(bf16)
    ones_hi = jnp.where(lax.broadcasted_iota(jnp.int32, (width, LANES), 1) < HEAD_DIM, 0.0, 1.0).astype(bf16)

    def window_start(j):
        blk = i * blocks + j
        if n_blk == 1:
            return 0
        start = jnp.clip(blk * tq - radius, 0, seq_len - width)
        return pl.multiple_of(start, HEAD_DIM)

    def scores(res, j, g):
        base = g * Q_PER_KV * HEAD_DIM
        rows = slice(j * tq, (j + 1) * tq)
        q2 = jnp.concatenate([q_ref[res, rows, base:base + LANES],
                              q_ref[res, rows, base + LANES:base + 2 * LANES]], axis=0)
        k_rows = pl.ds(window_start(j), width)
        k_win = jnp.concatenate(
            [kx_ref[res, k_rows, t * LANES:(t + 1) * LANES]
             for t in range(g * HEADS_PER_TILE, (g + 1) * HEADS_PER_TILE)], axis=0)
        return lax.dot_general(q2, k_win, (((1,), (1,)), ((), ())), preferred_element_type=jnp.float32)

    work = [(res, j, g) for res in range(n_res) for j in range(blocks) for g in range(N_KV_HEADS)]
    s_next = scores(*work[0])
    for n, (res, j, g) in enumerate(work):
        base = g * Q_PER_KV * HEAD_DIM
        rows = slice(j * tq, (j + 1) * tq)
        blk = i * blocks + j
        if n_blk == 1:
            variant = 0
        else:
            variant = jnp.where(blk == 0, 0, jnp.where(blk == n_blk - 1, 2, 1))
        if dil > 1:
            token_rows = pl.ds(blk * (tq * dil) + pl.program_id(1) * n_res + res, tq, stride=dil)
        s_all = s_next
        if n + 1 < len(work):
            s_next = scores(*work[n + 1])
        probs = []
        maxes = []
        sink_terms = []
        for v in range(HEADS_PER_TILE):
            t = g * HEADS_PER_TILE + v
            s = s_all[:, v * width:(v + 1) * width] + bias_ref[variant, t * 2 * tq:(t + 1) * 2 * tq, :]
            m = jnp.max(s, axis=-1, keepdims=True)
            if has_sink:
                h_top, h_bot = _dot_heads(g, v)
                sk = jnp.where(top_rows, sink_ref[h_top], sink_ref[h_bot])
                m = jnp.maximum(m, sk)
                sink_terms.append(jnp.exp2(sk - m))
            maxes.append(m)
            probs.append(jnp.exp2(s - m).astype(bf16))
        v_rows = pl.ds(window_start(j), width)
        v_rhs = jnp.concatenate(
            [jnp.concatenate([vx_ref[res, v_rows, (2 * g) * LANES:(2 * g + 1) * LANES], ones_lo], axis=1),
             jnp.concatenate([vx_ref[res, v_rows, (2 * g + 1) * LANES:(2 * g + 2) * LANES], ones_hi], axis=1)],
            axis=0)
        acc = jnp.dot(jnp.concatenate(probs, axis=1), v_rhs, preferred_element_type=jnp.float32)
        denom = acc[:, LANES:]
        if has_sink:
            denom = denom + jnp.where(low_half2, sink_terms[0], sink_terms[1])
        o = acc[:, :LANES] / denom
        if dil == 1:
            o_ref[rows, base:base + LANES] = o[:tq].astype(o_ref.dtype)
            o_ref[rows, base + LANES:base + 2 * LANES] = o[tq:].astype(o_ref.dtype)
        else:
            o_ref[g * HEADS_PER_TILE, token_rows, :] = o[:tq]
            o_ref[g * HEADS_PER_TILE + 1, token_rows, :] = o[tq:]
        if want_lse:
            if g == 0:
                lse_acc = jnp.zeros((tq, LANES), jnp.float32)
            lse = (jnp.where(low_half2, maxes[0], maxes[1]) + jnp.log2(denom)) * LN_2
            for half, half_rows in ((0, slice(0, tq)), (1, slice(tq, 2 * tq))):
                tile = g * HEADS_PER_TILE + half
                here = (lane == tile) | (lane == HEAD_DIM + tile)
                lse_acc = jnp.where(here, lse[half_rows], lse_acc)
            if g == N_KV_HEADS - 1:
                if dil == 1:
                    lse_ref[rows, :] = lse_acc
                else:
                    lse_ref[token_rows, :] = lse_acc


def _band_attention(qkv, bias, radius, *, sink=None, want_lse=False):
    q, kx, vx = qkv
    B, dil, L, _ = q.shape
    S = L * dil
    tq = ATTN_TQ
    ts = min(L, ATTN_STEP_ROWS)
    n_res = min(dil, ATTN_STEP_ROWS // ts)
    D = Q_COLS

    in_specs = [
        pl.BlockSpec((None, n_res, ts, Q_COLS), lambda b, r, i: (b, r, i, 0)),
        pl.BlockSpec((None, n_res, L, KVX_COLS), lambda b, r, i: (b, r, 0, 0)),
        pl.BlockSpec((None, n_res, L, KVX_COLS), lambda b, r, i: (b, r, 0, 0)),
        _const_spec(bias.shape),
    ]
    args = [q, kx, vx, bias]
    if sink is not None:
        in_specs.append(pl.BlockSpec(memory_space=pltpu.SMEM))
        args.append(sink.astype(jnp.float32) * LOG2_E)
    if dil == 1:
        out_specs = [pl.BlockSpec((None, ts, D), lambda b, r, i: (b, i, 0))]
        out_shape = [jax.ShapeDtypeStruct((B, S, D), jnp.bfloat16)]
        lse_spec = pl.BlockSpec((None, ts, LANES), lambda b, r, i: (b, i, 0))
    else:
        assert want_lse
        out_specs = [pl.BlockSpec((None, D // LANES, S, LANES), lambda b, r, i: (b, 0, 0, 0))]
        out_shape = [jax.ShapeDtypeStruct((B, D // LANES, S, LANES), jnp.float32)]
        lse_spec = pl.BlockSpec((None, S, LANES), lambda b, r, i: (b, 0, 0))
    if want_lse:
        out_specs.append(lse_spec)
        out_shape.append(jax.ShapeDtypeStruct((B, S, LANES), jnp.float32))

    kernel = functools.partial(_attn_kernel, tq=tq, radius=radius, seq_len=L, dil=dil,
                               has_sink=sink is not None, want_lse=want_lse)
    out = pl.pallas_call(
        kernel,
        grid=(B, dil // n_res, L // ts),
        in_specs=in_specs,
        out_specs=out_specs,
        out_shape=out_shape,
        compiler_params=_params(3),
        name=f"band_attn_r{radius}_d{dil}",
    )(*args)
    if want_lse:
        return out[0], out[1]
    return out[0]


def _head_expand_matrix():
    e = np.zeros((LANES, Q_COLS), np.float32)
    for h in range(N_HEADS):
        e[_lse_lane(h), h * HEAD_DIM:(h + 1) * HEAD_DIM] = 1.0
    return jnp.asarray(np.concatenate([e, e], axis=0), jnp.bfloat16)


def _read_rows_f32(o_ref):
    if len(o_ref.shape) == 2:
        return o_ref[...].astype(jnp.float32)
    return jnp.concatenate([o_ref[j] for j in range(o_ref.shape[0])], axis=1)


def _ffn_kernel(*refs, n_mix, final):
    refs = list(refs)
    x_ref = refs[0]
    pos = 1
    x = x_ref[...]
    if n_mix >= 1:
        o_refs = refs[pos:pos + n_mix]
        pos += n_mix
        if n_mix > 1:
            lse_refs = refs[pos:pos + n_mix]
            expand_ref = refs[pos + n_mix]
            pos += n_mix + 1
        wo_ref = refs[pos]
        pos += 1
        if n_mix == 1:
            o = o_refs[0][...]
        else:
            lses = [r[...] for r in lse_refs]
            top = functools.reduce(jnp.maximum, lses)
            es = [jnp.exp(l - top) for l in lses]
            inv_tot = 1.0 / functools.reduce(lambda a, b: a + b, es)
            o = None
            for e, o_ref in zip(es, o_refs):
                w = e * inv_tot
                w_hi = w.astype(jnp.bfloat16)
                w_lo = (w - w_hi.astype(jnp.float32)).astype(jnp.bfloat16)
                w_full = jnp.dot(jnp.concatenate([w_hi, w_lo], axis=1), expand_ref[...],
                                 preferred_element_type=jnp.float32)
                term = w_full * _read_rows_f32(o_ref)
                o = term if o is None else o + term
            o = o.astype(jnp.bfloat16)
        x = x + jnp.dot(o, wo_ref[...], preferred_element_type=jnp.float32)
    g_ref, w1_ref, w2_ref = refs[pos:pos + 3]
    pos += 3
    if final:
        fg_ref = refs[pos]
        pos += 1
    out_ref = refs[pos]

    h = _rmsnorm_f32(x, g_ref[...]).astype(jnp.bfloat16)
    acc = x
    d_ff = w1_ref.shape[1]
    for c in range(d_ff // FFN_CHUNK):
        cols = slice(c * FFN_CHUNK, (c + 1) * FFN_CHUNK)
        a = jnp.dot(h, w1_ref[:, cols], preferred_element_type=jnp.float32)
        a = jnp.square(jnp.maximum(a, 0.0)).astype(jnp.bfloat16)
        acc = acc + jnp.dot(a, w2_ref[cols, :], preferred_element_type=jnp.float32)
    if final:
        acc = _rmsnorm_f32(acc, fg_ref[...])
    out_ref[...] = acc


def _ffn(x, g, w1, w2, *, mix=None, final_g=None):
    B, S, D = x.shape
    d_ff = w1.shape[1]
    tm = FFN_TM_MERGE if mix is not None and len(mix[0]) > 1 else FFN_TM
    row_spec = lambda c: pl.BlockSpec((None, tm, c), lambda b, i: (b, i, 0))
    slab_spec = lambda c: pl.BlockSpec((None, c // LANES, tm, LANES), lambda b, i: (b, 0, i, 0))
    in_specs = [row_spec(D)]
    args = [x]
    n_mix = 0
    if mix is not None:
        o_list, lse_list, w_o = mix
        n_mix = len(o_list)
        in_specs += [row_spec(D) if o.ndim == 3 else slab_spec(D) for o in o_list]
        args += o_list
        if n_mix > 1:
            in_specs += [row_spec(LANES)] * n_mix + [_const_spec((2 * LANES, Q_COLS))]
            args += lse_list + [_head_expand_matrix()]
        in_specs.append(_const_spec((D, D)))
        args.append(w_o)
    in_specs += [_const_spec((1, D)), _const_spec((D, d_ff)), _const_spec((d_ff, D))]
    args += [g.reshape(1, D), w1, w2]
    if final_g is not None:
        in_specs.append(_const_spec((1, D)))
        args.append(final_g.reshape(1, D))
    out = pl.pallas_call(
        functools.partial(_ffn_kernel, n_mix=n_mix, final=final_g is not None),
        grid=(B, S // tm),
        in_specs=in_specs,
        out_specs=row_spec(D),
        out_shape=jax.ShapeDtypeStruct((B, S, D), jnp.float32),
        compiler_params=_params(2),
        name=f"ffn_mix{n_mix}" + ("_final" if final_g is not None else ""),
    )(*args)
    return out


def _gmlp_kernel(x_ref, g_ref, win_ref, lng_ref, lnb_ref, ws_ref, bs_ref, wout_ref, o_ref):
    tm = x_ref.shape[0]
    hidden = wout_ref.shape[0]
    gcols = hidden // B_GROUPS
    subs = [slice(r0, r0 + GMLP_SUB) for r0 in range(0, tm, GMLP_SUB)]
    zs = []
    for rows in subs:
        h = _rmsnorm_f32(x_ref[rows, :], g_ref[...]).astype(jnp.bfloat16)
        zs.append(jnp.dot(h, win_ref[...], preferred_element_type=jnp.float32))
    for rows, z in zip(subs, zs):
        z = 0.5 * z * (1.0 + lax.erf(z * (2.0 ** -0.5)))
        u = z[:, :hidden]
        v = z[:, hidden:]
        mu = jnp.mean(v, axis=-1, keepdims=True)
        vc = v - mu
        var = jnp.mean(vc * vc, axis=-1, keepdims=True)
        vn = (vc * lax.rsqrt(var + LN_EPS) * lng_ref[...] + lnb_ref[...]).astype(jnp.bfloat16)
        gated = []
        for r0 in range(0, GMLP_SUB, B_CHUNK):
            parts = []
            for grp in range(B_GROUPS):
                parts.append(jnp.dot(ws_ref[grp], vn[r0:r0 + B_CHUNK, grp * gcols:(grp + 1) * gcols],
                                     preferred_element_type=jnp.float32))
            mixed = jnp.concatenate(parts, axis=-1) + bs_ref[...]
            gated.append((u[r0:r0 + B_CHUNK, :] * mixed).astype(jnp.bfloat16))
        t = jnp.concatenate(gated, axis=0)
        o_ref[rows, :] = x_ref[rows, :] + jnp.dot(t, wout_ref[...], preferred_element_type=jnp.float32)


def _gmlp(x, g, w_in, ln_g, ln_b, w_s, b_s, w_out):
    B, S, D = x.shape
    hidden = w_out.shape[0]
    T = B * S
    tm = GMLP_TM
    bs_full = jnp.repeat(jnp.transpose(b_s).astype(jnp.float32), hidden // B_GROUPS, axis=1)
    row_spec = pl.BlockSpec((tm, D), lambda i: (i, 0))
    out = pl.pallas_call(
        _gmlp_kernel,
        grid=(T // tm,),
        in_specs=[row_spec, _const_spec((1, D)), _const_spec((D, 2 * hidden)),
                  _const_spec((1, hidden)), _const_spec((1, hidden)),
                  _const_spec((B_GROUPS, B_CHUNK, B_CHUNK)), _const_spec((B_CHUNK, hidden)),
                  _const_spec((hidden, D))],
        out_specs=row_spec,
        out_shape=jax.ShapeDtypeStruct((T, D), jnp.float32),
        compiler_params=_params(1),
        name="gmlp",
    )(x.reshape(T, D), g.reshape(1, D), w_in, ln_g.reshape(1, hidden).astype(jnp.float32),
      ln_b.reshape(1, hidden).astype(jnp.float32), w_s, bs_full, w_out)
    return out.reshape(B, S, D)


def _trunk(x, p):
    depth = p["norm_mix_g"].shape[0]
    for i in range(depth):
        kind, j = i % N_MIXERS, i // N_MIXERS
        g_mix = p["norm_mix_g"][i]
        mix = None
        if kind == 0:
            (qkv,) = _qkv_projection(x, g_mix, [p["a_wqkv"][j]], [1])
            o = _band_attention(qkv, p["a_bias"], A_RADIUS, sink=p["a_sink"][j])
            mix = ([o], None, p["a_wo"][j])
        elif kind == 1:
            x = _gmlp(x, g_mix, p["b_win"][j], p["b_ln_g"][j], p["b_ln_b"][j], p["b_ws"][j],
                      p["b_bs"][j], p["b_wo"][j])
        else:
            o_list, lse_list = [], []
            dils = [dil for _, dil in C_GROUPS]
            weights = [p["c_wqkv"][j][:, gi * QKV_COLS:(gi + 1) * QKV_COLS] for gi in range(len(dils))]
            qkvs = _qkv_projection(x, g_mix, weights, dils)
            for gi, (window, dil) in enumerate(C_GROUPS):
                o, lse = _band_attention(qkvs[gi], p["c_bias"][gi], window // (2 * dil), want_lse=True)
                o_list.append(o)
                lse_list.append(lse)
            mix = (o_list, lse_list, p["c_wo"][j])
        final_g = p["final_g"] if i == depth - 1 else None
        x = _ffn(x, p["norm_ffn_g"][i], p["ffn_w1"][i], p["ffn_w2"][i], mix=mix, final_g=final_g)
    return x


def kernel(x_prompt, x_sample, rel_bias, norm_mix_g, norm_ffn_g, final_g, ffn_w1, ffn_w2, a_wqkv,
           a_sink, a_wo, b_win, b_ln_g, b_ln_b, b_ws, b_bs, b_wo, c_wqkv, c_wo):
    bf16 = jnp.bfloat16
    f32 = jnp.float32
    S = x_prompt.shape[1]
    assert x_sample.shape[1] == S
    p = {
        "norm_mix_g": norm_mix_g.astype(f32), "norm_ffn_g": norm_ffn_g.astype(f32),
        "final_g": final_g.astype(f32),
        "ffn_w1": ffn_w1.astype(bf16), "ffn_w2": ffn_w2.astype(bf16),
        "a_wqkv": a_wqkv.astype(bf16), "a_sink": a_sink, "a_wo": a_wo.astype(bf16),
        "b_win": b_win.astype(bf16), "b_ln_g": b_ln_g, "b_ln_b": b_ln_b,
        "b_ws": b_ws.astype(bf16), "b_bs": b_bs, "b_wo": b_wo.astype(bf16),
        "c_wqkv": c_wqkv.astype(bf16), "c_wo": c_wo.astype(bf16),
        "a_bias": _band_bias_table(rel_bias, ATTN_TQ, A_RADIUS, 1, S),
        "c_bias": [_band_bias_table(rel_bias, ATTN_TQ, window // (2 * dil), dil, S // dil)
                   for window, dil in C_GROUPS],
    }
    return _trunk(x_prompt, p), _trunk(x_sample, p)
```

```python
import functools
import math

import numpy as np
import jax
import jax.numpy as jnp
from jax import lax
from jax.experimental import pallas as pl
from jax.experimental.pallas import tpu as pltpu

HEAD_DIM = 64
N_MIXERS = 3
N_HEADS = 16
N_KV_HEADS = 4
A_RADIUS = 128
B_CHUNK = 128
B_GROUPS = 8
C_GROUPS = ((128, 1), (512, 4), (2048, 16))
NUM_BUCKETS = 32
REL_MAX_DISTANCE = 1024
RMS_EPS = 1e-6
LN_EPS = 1e-5

Q_COLS = N_HEADS * HEAD_DIM
KV_COLS = N_KV_HEADS * HEAD_DIM
QKV_COLS = Q_COLS + 2 * KV_COLS
LANES = 128
HEADS_PER_TILE = LANES // HEAD_DIM
Q_PER_KV = N_HEADS // N_KV_HEADS
KVX_COLS = N_KV_HEADS * HEADS_PER_TILE * LANES
MASK_VALUE = -1e30
LOG2_E = math.log2(math.e)
ROWS_PER_WORD = 2
LN_2 = math.log(2.0)

V7X_VMEM_BYTES = 64 * 1024 * 1024
VMEM_LIMIT_BYTES = V7X_VMEM_BYTES - 8 * 1024 * 1024

ATTN_TQ = 128
ATTN_STEP_ROWS = 512
FFN_TM = 1024
FFN_TM_MERGE = 512
FFN_CHUNK = 1024
PROJ_TM = 1024
PROJ_TM_MULTI = 512
GMLP_TM = 512
GMLP_SUB = 256


def _const_spec(shape):
    zeros = (0,) * len(shape)
    return pl.BlockSpec(shape, lambda *_: zeros, pipeline_mode=pl.Buffered(1))


def _params(n_grid_dims):
    return pltpu.CompilerParams(
        dimension_semantics=("arbitrary",) * n_grid_dims,
        vmem_limit_bytes=VMEM_LIMIT_BYTES,
    )


def _rmsnorm_f32(x, g):
    ms = jnp.mean(x * x, axis=-1, keepdims=True)
    return x * lax.rsqrt(ms + RMS_EPS) * g


def _lse_lane(head):
    return head // HEADS_PER_TILE + HEAD_DIM * (head % HEADS_PER_TILE)


def _proj_kernel(*refs, dils):
    n = len(dils)
    x_ref, g_ref = refs[:2]
    w_refs = refs[2:2 + n]
    out_refs = refs[2 + n:2 + 4 * n]
    hn = _rmsnorm_f32(x_ref[...], g_ref[...])
    tm, d_model = hn.shape
    n_slabs = d_model // LANES
    if any(d > 1 for d in dils):
        slab_ref = refs[2 + 4 * n]
        for j in range(n_slabs):
            slab_ref[j] = hn[:, j * LANES:(j + 1) * LANES]
    low_half = lax.broadcasted_iota(jnp.int32, (tm, LANES), 1) < HEAD_DIM
    for gi, (dil, w_ref) in enumerate(zip(dils, w_refs)):
        q_ref, kx_ref, vx_ref = out_refs[3 * gi:3 * gi + 3]
        n_rows = tm // dil
        if dil == 1:
            h = hn
        else:
            h = jnp.concatenate(
                [jnp.concatenate([slab_ref[j, pl.ds(r, n_rows, stride=dil), :] for j in range(n_slabs)], axis=1)
                 for r in range(dil)], axis=0)
        hb = h.astype(jnp.bfloat16)
        y_kv = jnp.dot(hb, w_ref[:, Q_COLS:], preferred_element_type=jnp.float32)
        y_q = jnp.dot(hb, w_ref[:, :Q_COLS], preferred_element_type=jnp.float32)
        pieces = []
        for o_ref, col0 in ((kx_ref, 0), (vx_ref, KV_COLS)):
            tiles = []
            for t in range(KV_COLS // LANES):
                pair = y_kv[:, col0 + t * LANES:col0 + (t + 1) * LANES]
                only_a = jnp.where(low_half, pair, 0.0)
                only_b = jnp.where(low_half, 0.0, pair)
                tiles += [only_a, pltpu.roll(only_a, HEAD_DIM, 1), pltpu.roll(only_b, HEAD_DIM, 1), only_b]
            pieces.append((o_ref, jnp.concatenate(tiles, axis=1).astype(jnp.bfloat16)))
        pieces.append((q_ref, (y_q * (HEAD_DIM ** -0.5 * LOG2_E)).astype(jnp.bfloat16)))
        for o_ref, val in pieces:
            words = pltpu.bitcast(val, o_ref.dtype)
            n_words = n_rows // ROWS_PER_WORD
            for r in range(dil):
                o_ref[r] = words[r * n_words:(r + 1) * n_words]


def _qkv_projection(x, g, weights, dils):
    B, S, D = x.shape
    n = len(dils)
    tm = PROJ_TM if n == 1 else PROJ_TM_MULTI
    in_specs = [pl.BlockSpec((None, tm, D), lambda b, i: (b, i, 0)), _const_spec((1, D))]
    in_specs += [_const_spec((D, QKV_COLS))] * n
    out_specs, out_shape = [], []
    for dil in dils:
        for cols in (Q_COLS, KVX_COLS, KVX_COLS):
            out_specs.append(pl.BlockSpec((None, dil, tm // dil // ROWS_PER_WORD, cols),
                                          lambda b, i: (b, 0, i, 0)))
            out_shape.append(jax.ShapeDtypeStruct((B, dil, S // dil // ROWS_PER_WORD, cols), jnp.uint32))
    scratch = []
    if any(d > 1 for d in dils):
        scratch.append(pltpu.VMEM((D // LANES, tm, LANES), jnp.float32))
    outs = pl.pallas_call(
        functools.partial(_proj_kernel, dils=tuple(dils)),
        grid=(B, S // tm),
        in_specs=in_specs,
        out_specs=out_specs,
        out_shape=out_shape,
        scratch_shapes=scratch,
        compiler_params=_params(2),
        name="qkv_proj_" + "_".join(f"d{d}" for d in dils),
    )(x, g.reshape(1, D), *weights)
    return [tuple(outs[3 * gi:3 * gi + 3]) for gi in range(n)]


def _rel_bucket(rel):
    half = NUM_BUCKETS // 2
    max_exact = half // 2
    n = np.abs(rel)
    large = max_exact + (np.log(np.maximum(n, 1) / max_exact) / np.log(REL_MAX_DISTANCE / max_exact)
                         * (half - max_exact)).astype(np.int32)
    large = np.minimum(large, half - 1)
    return (rel > 0).astype(np.int32) * half + np.where(n < max_exact, n, large)


def _window_geometry(tq, radius, seq_len):
    width = min(tq + 2 * radius, seq_len)
    if seq_len // tq == 1:
        return width, (0,)
    return width, (0, radius, width - tq)


def _dot_heads(g, v):
    return g * Q_PER_KV + v, g * Q_PER_KV + HEADS_PER_TILE + v


def _band_bias_table(rel_bias, tq, radius, dil, seq_len):
    width, offsets = _window_geometry(tq, radius, seq_len)
    span = max(offsets) + tq - 1
    n_diag = span + width
    diag_rel = np.arange(n_diag) - span
    per_diag = jnp.transpose(rel_bias[_rel_bucket(diag_rel * dil)], (1, 0)).astype(jnp.float32) * LOG2_E
    per_diag = jnp.where(jnp.asarray(np.abs(diag_rel) <= radius)[None], per_diag, MASK_VALUE)
    padded = jnp.pad(per_diag, ((0, 0), (0, 1)))
    tiled = jnp.broadcast_to(padded[:, None, :], (N_HEADS, tq, n_diag + 1)).reshape(N_HEADS, -1)
    skewed = tiled[:, :tq * n_diag].reshape(N_HEADS, tq, n_diag)
    order = [h for g in range(N_KV_HEADS) for v in range(HEADS_PER_TILE) for h in _dot_heads(g, v)]
    skewed = skewed[np.asarray(order)]
    tables = []
    for off in offsets:
        c0 = span - off
        tables.append(skewed[:, :, c0:c0 + width].reshape(N_HEADS * tq, width))
    return jnp.stack(tables, axis=0)


def _attn_kernel(*refs, tq, radius, seq_len, dil, has_sink, want_lse):
    refs = list(refs)
    q_ref, kx_ref, vx_ref, bias_ref = refs[:4]
    pos = 4
    sink_ref = None
    if has_sink:
        sink_ref = refs[pos]
        pos += 1
    o_ref = refs[pos]
    pos += 1
    lse_ref = refs[pos] if want_lse else None

    n_res = q_ref.shape[0]
    blocks = q_ref.shape[1] * ROWS_PER_WORD // tq
    width, offsets = _window_geometry(tq, radius, seq_len)
    n_blk = seq_len // tq
    i = pl.program_id(2)
    bf16 = jnp.bfloat16

    lane = lax.broadcasted_iota(jnp.int32, (tq, LANES), 1)
    low_half2 = lax.broadcasted_iota(jnp.int32, (2 * tq, LANES), 1) < HEAD_DIM
    top_rows = lax.broadcasted_iota(jnp.int32, (2 * tq, 1), 0) < tq
    ones_lo = jnp.where(lax.broadcasted_iota(jnp.int32, (width, LANES), 1) < HEAD_DIM, 1.0, 0.0).astype(bf16)
    ones_hi = jnp.where(lax.broadcasted_iota(jnp.int32, (width, LANES), 1) < HEAD_DIM, 0.0, 1.0).astype(bf16)

    def window_start(j):
        blk = i * blocks + j
        if n_blk == 1:
            return 0
        return jnp.clip(blk * tq - radius, 0, seq_len - width)

    def load_rows(ref, res, start, n, lanes):
        word_start = start // ROWS_PER_WORD
        if not isinstance(word_start, int):
            word_start = pl.multiple_of(word_start, HEAD_DIM // ROWS_PER_WORD)
        words = ref[res, pl.ds(word_start, n // ROWS_PER_WORD), lanes]
        return pltpu.bitcast(words, bf16)

    def scores(res, j, g):
        base = g * Q_PER_KV * HEAD_DIM
        q2 = jnp.concatenate([load_rows(q_ref, res, j * tq, tq, slice(base, base + LANES)),
                              load_rows(q_ref, res, j * tq, tq, slice(base + LANES, base + 2 * LANES))],
                             axis=0)
        k_win = jnp.concatenate(
            [load_rows(kx_ref, res, window_start(j), width, slice(t * LANES, (t + 1) * LANES))
             for t in range(g * HEADS_PER_TILE, (g + 1) * HEADS_PER_TILE)], axis=0)
        return lax.dot_general(q2, k_win, (((1,), (1,)), ((), ())), preferred_element_type=jnp.float32)

    work = [(res, j, g) for res in range(n_res) for j in range(blocks) for g in range(N_KV_HEADS)]
    s_next = scores(*work[0])
    for n, (res, j, g) in enumerate(work):
        base = g * Q_PER_KV * HEAD_DIM
        rows = slice(j * tq, (j + 1) * tq)
        blk = i * blocks + j
        if n_blk == 1:
            variant = 0
        else:
            variant = jnp.where(blk == 0, 0, jnp.where(blk == n_blk - 1, 2, 1))
        if dil > 1:
            token_rows = pl.ds(blk * (tq * dil) + pl.program_id(1) * n_res + res, tq, stride=dil)
        s_all = s_next
        if n + 1 < len(work):
            s_next = scores(*work[n + 1])
        probs = []
        maxes = []
        sink_terms = []
        for v in range(HEADS_PER_TILE):
            t = g * HEADS_PER_TILE + v
            s = s_all[:, v * width:(v + 1) * width] + bias_ref[variant, t * 2 * tq:(t + 1) * 2 * tq, :]
            m = jnp.max(s, axis=-1, keepdims=True)
            if has_sink:
                h_top, h_bot = _dot_heads(g, v)
                sk = jnp.where(top_rows, sink_ref[h_top], sink_ref[h_bot])
                m = jnp.maximum(m, sk)
                sink_terms.append(jnp.exp2(sk - m))
            maxes.append(m)
            probs.append(jnp.exp2(s - m).astype(bf16))
        v_tiles = [load_rows(vx_ref, res, window_start(j), width, slice(t * LANES, (t + 1) * LANES))
                   for t in range(g * HEADS_PER_TILE, (g + 1) * HEADS_PER_TILE)]
        v_rhs = jnp.concatenate([jnp.concatenate([v_tiles[0], ones_lo], axis=1),
                                 jnp.concatenate([v_tiles[1], ones_hi], axis=1)], axis=0)
        acc = jnp.dot(jnp.concatenate(probs, axis=1), v_rhs, preferred_element_type=jnp.float32)
        denom = acc[:, LANES:]
        if has_sink:
            denom = denom + jnp.where(low_half2, sink_terms[0], sink_terms[1])
        o = acc[:, :LANES] / denom
        if dil == 1:
            o_ref[rows, base:base + LANES] = o[:tq].astype(o_ref.dtype)
            o_ref[rows, base + LANES:base + 2 * LANES] = o[tq:].astype(o_ref.dtype)
        else:
            o_ref[g * HEADS_PER_TILE, token_rows, :] = o[:tq]
            o_ref[g * HEADS_PER_TILE + 1, token_rows, :] = o[tq:]
        if want_lse:
            if g == 0:
                lse_acc = jnp.zeros((tq, LANES), jnp.float32)
            lse = jnp.where(low_half2, maxes[0] * LN_2, maxes[1] * LN_2) + jnp.log(denom)
            for half, half_rows in ((0, slice(0, tq)), (1, slice(tq, 2 * tq))):
                tile = g * HEADS_PER_TILE + half
                here = (lane == tile) | (lane == HEAD_DIM + tile)
                lse_acc = jnp.where(here, lse[half_rows], lse_acc)
            if g == N_KV_HEADS - 1:
                if dil == 1:
                    lse_ref[rows, :] = lse_acc
                else:
                    lse_ref[token_rows, :] = lse_acc


def _band_attention(qkv, bias, radius, *, sink=None, want_lse=False):
    q, kx, vx = qkv
    B, dil, L = q.shape[0], q.shape[1], q.shape[2] * ROWS_PER_WORD
    S = L * dil
    tq = ATTN_TQ
    ts = min(L, ATTN_STEP_ROWS)
    n_res = min(dil, ATTN_STEP_ROWS // ts)
    D = Q_COLS

    in_specs = [
        pl.BlockSpec((None, n_res, ts // ROWS_PER_WORD, Q_COLS), lambda b, r, i: (b, r, i, 0)),
        pl.BlockSpec((None, n_res, L // ROWS_PER_WORD, KVX_COLS), lambda b, r, i: (b, r, 0, 0)),
        pl.BlockSpec((None, n_res, L // ROWS_PER_WORD, KVX_COLS), lambda b, r, i: (b, r, 0, 0)),
        _const_spec(bias.shape),
    ]
    args = [q, kx, vx, bias]
    if sink is not None:
        in_specs.append(pl.BlockSpec(memory_space=pltpu.SMEM))
        args.append(sink.astype(jnp.float32) * LOG2_E)
    if dil == 1:
        out_specs = [pl.BlockSpec((None, ts, D), lambda b, r, i: (b, i, 0))]
        out_shape = [jax.ShapeDtypeStruct((B, S, D), jnp.bfloat16)]
        lse_spec = pl.BlockSpec((None, ts, LANES), lambda b, r, i: (b, i, 0))
    else:
        assert want_lse
        out_specs = [pl.BlockSpec((None, D // LANES, S, LANES), lambda b, r, i: (b, 0, 0, 0))]
        out_shape = [jax.ShapeDtypeStruct((B, D // LANES, S, LANES), jnp.float32)]
        lse_spec = pl.BlockSpec((None, S, LANES), lambda b, r, i: (b, 0, 0))
    if want_lse:
        out_specs.append(lse_spec)
        out_shape.append(jax.ShapeDtypeStruct((B, S, LANES), jnp.float32))

    kernel = functools.partial(_attn_kernel, tq=tq, radius=radius, seq_len=L, dil=dil,
                               has_sink=sink is not None, want_lse=want_lse)
    out = pl.pallas_call(
        kernel,
        grid=(B, dil // n_res, L // ts),
        in_specs=in_specs,
        out_specs=out_specs,
        out_shape=out_shape,
        compiler_params=_params(3),
        name=f"band_attn_r{radius}_d{dil}",
    )(*args)
    if want_lse:
        return out[0], out[1]
    return out[0]


def _head_expand_matrix():
    e = np.zeros((LANES, Q_COLS), np.float32)
    for h in range(N_HEADS):
        e[_lse_lane(h), h * HEAD_DIM:(h + 1) * HEAD_DIM] = 1.0
    return jnp.asarray(np.concatenate([e, e], axis=0), jnp.bfloat16)


def _read_rows_f32(o_ref):
    if len(o_ref.shape) == 2:
        return o_ref[...].astype(jnp.float32)
    return jnp.concatenate([o_ref[j] for j in range(o_ref.shape[0])], axis=1)


def _ffn_kernel(*refs, n_mix, final):
    refs = list(refs)
    x_ref = refs[0]
    pos = 1
    x = x_ref[...]
    if n_mix >= 1:
        o_refs = refs[pos:pos + n_mix]
        pos += n_mix
        if n_mix > 1:
            lse_refs = refs[pos:pos + n_mix]
            expand_ref = refs[pos + n_mix]
            pos += n_mix + 1
        wo_ref = refs[pos]
        pos += 1
        if n_mix == 1:
            o = o_refs[0][...]
        else:
            lses = [r[...] for r in lse_refs]
            top = functools.reduce(jnp.maximum, lses)
            es = [jnp.exp(l - top) for l in lses]
            inv_tot = 1.0 / functools.reduce(lambda a, b: a + b, es)
            o = None
            for e, o_ref in zip(es, o_refs):
                w = e * inv_tot
                w_hi = w.astype(jnp.bfloat16)
                w_lo = (w - w_hi.astype(jnp.float32)).astype(jnp.bfloat16)
                w_full = jnp.dot(jnp.concatenate([w_hi, w_lo], axis=1), expand_ref[...],
                                 preferred_element_type=jnp.float32)
                term = w_full * _read_rows_f32(o_ref)
                o = term if o is None else o + term
            o = o.astype(jnp.bfloat16)
        x = x + jnp.dot(o, wo_ref[...], preferred_element_type=jnp.float32)
    g_ref, w1_ref, w2_ref = refs[pos:pos + 3]
    pos += 3
    if final:
        fg_ref = refs[pos]
        pos += 1
    out_ref = refs[pos]

    h = _rmsnorm_f32(x, g_ref[...]).astype(jnp.bfloat16)
    acc = x
    d_ff = w1_ref.shape[1]
    for c in range(d_ff // FFN_CHUNK):
        cols = slice(c * FFN_CHUNK, (c + 1) * FFN_CHUNK)
        a = jnp.dot(h, w1_ref[:, cols], preferred_element_type=jnp.float32)
        a = jnp.square(jnp.maximum(a, 0.0)).astype(jnp.bfloat16)
        acc = acc + jnp.dot(a, w2_ref[cols, :], preferred_element_type=jnp.float32)
    if final:
        acc = _rmsnorm_f32(acc, fg_ref[...])
    out_ref[...] = acc


def _ffn(x, g, w1, w2, *, mix=None, final_g=None):
    B, S, D = x.shape
    d_ff = w1.shape[1]
    tm = FFN_TM_MERGE if mix is not None and len(mix[0]) > 1 else FFN_TM
    row_spec = lambda c: pl.BlockSpec((None, tm, c), lambda b, i: (b, i, 0))
    slab_spec = lambda c: pl.BlockSpec((None, c // LANES, tm, LANES), lambda b, i: (b, 0, i, 0))
    in_specs = [row_spec(D)]
    args = [x]
    n_mix = 0
    if mix is not None:
        o_list, lse_list, w_o = mix
        n_mix = len(o_list)
        in_specs += [row_spec(D) if o.ndim == 3 else slab_spec(D) for o in o_list]
        args += o_list
        if n_mix > 1:
            in_specs += [row_spec(LANES)] * n_mix + [_const_spec((2 * LANES, Q_COLS))]
            args += lse_list + [_head_expand_matrix()]
        in_specs.append(_const_spec((D, D)))
        args.append(w_o)
    in_specs += [_const_spec((1, D)), _const_spec((D, d_ff)), _const_spec((d_ff, D))]
    args += [g.reshape(1, D), w1, w2]
    if final_g is not None:
        in_specs.append(_const_spec((1, D)))
        args.append(final_g.reshape(1, D))
    out = pl.pallas_call(
        functools.partial(_ffn_kernel, n_mix=n_mix, final=final_g is not None),
        grid=(B, S // tm),
        in_specs=in_specs,
        out_specs=row_spec(D),
        out_shape=jax.ShapeDtypeStruct((B, S, D), jnp.float32),
        compiler_params=_params(2),
        name=f"ffn_mix{n_mix}" + ("_final" if final_g is not None else ""),
    )(*args)
    return out


def _gmlp_kernel(x_ref, g_ref, win_ref, lng_ref, lnb_ref, ws_ref, bs_ref, wout_ref, o_ref):
    tm = x_ref.shape[0]
    hidden = wout_ref.shape[0]
    gcols = hidden // B_GROUPS
    subs = [slice(r0, r0 + GMLP_SUB) for r0 in range(0, tm, GMLP_SUB)]
    zs = []
    for rows in subs:
        h = _rmsnorm_f32(x_ref[rows, :], g_ref[...]).astype(jnp.bfloat16)
        zs.append(jnp.dot(h, win_ref[...], preferred_element_type=jnp.float32))
    for rows, z in zip(subs, zs):
        z = 0.5 * z * (1.0 + lax.erf(z * (2.0 ** -0.5)))
        u = z[:, :hidden]
        v = z[:, hidden:]
        mu = jnp.mean(v, axis=-1, keepdims=True)
        vc = v - mu
        var = jnp.mean(vc * vc, axis=-1, keepdims=True)
        vn = (vc * lax.rsqrt(var + LN_EPS) * lng_ref[...] + lnb_ref[...]).astype(jnp.bfloat16)
        gated = []
        for r0 in range(0, GMLP_SUB, B_CHUNK):
            parts = []
            for grp in range(B_GROUPS):
                parts.append(jnp.dot(ws_ref[grp], vn[r0:r0 + B_CHUNK, grp * gcols:(grp + 1) * gcols],
                                     preferred_element_type=jnp.float32))
            mixed = jnp.concatenate(parts, axis=-1) + bs_ref[...]
            gated.append((u[r0:r0 + B_CHUNK, :] * mixed).astype(jnp.bfloat16))
        t = jnp.concatenate(gated, axis=0)
        o_ref[rows, :] = x_ref[rows, :] + jnp.dot(t, wout_ref[...], preferred_element_type=jnp.float32)


def _gmlp(x, g, w_in, ln_g, ln_b, w_s, b_s, w_out):
    B, S, D = x.shape
    hidden = w_out.shape[0]
    T = B * S
    tm = GMLP_TM
    bs_full = jnp.repeat(jnp.transpose(b_s).astype(jnp.float32), hidden // B_GROUPS, axis=1)
    row_spec = pl.BlockSpec((tm, D), lambda i: (i, 0))
    out = pl.pallas_call(
        _gmlp_kernel,
        grid=(T // tm,),
        in_specs=[row_spec, _const_spec((1, D)), _const_spec((D, 2 * hidden)),
                  _const_spec((1, hidden)), _const_spec((1, hidden)),
                  _const_spec((B_GROUPS, B_CHUNK, B_CHUNK)), _const_spec((B_CHUNK, hidden)),
                  _const_spec((hidden, D))],
        out_specs=row_spec,
        out_shape=jax.ShapeDtypeStruct((T, D), jnp.float32),
        compiler_params=_params(1),
        name="gmlp",
    )(x.reshape(T, D), g.reshape(1, D), w_in, ln_g.reshape(1, hidden).astype(jnp.float32),
      ln_b.reshape(1, hidden).astype(jnp.float32), w_s, bs_full, w_out)
    return out.reshape(B, S, D)


def _trunk(x, p):
    depth = p["norm_mix_g"].shape[0]
    for i in range(depth):
        kind, j = i % N_MIXERS, i // N_MIXERS
        g_mix = p["norm_mix_g"][i]
        mix = None
        if kind == 0:
            (qkv,) = _qkv_projection(x, g_mix, [p["a_wqkv"][j]], [1])
            o = _band_attention(qkv, p["a_bias"], A_RADIUS, sink=p["a_sink"][j])
            mix = ([o], None, p["a_wo"][j])
        elif kind == 1:
            x = _gmlp(x, g_mix, p["b_win"][j], p["b_ln_g"][j], p["b_ln_b"][j], p["b_ws"][j],
                      p["b_bs"][j], p["b_wo"][j])
        else:
            o_list, lse_list = [], []
            dils = [dil for _, dil in C_GROUPS]
            weights = [p["c_wqkv"][j][:, gi * QKV_COLS:(gi + 1) * QKV_COLS] for gi in range(len(dils))]
            qkvs = _qkv_projection(x, g_mix, weights, dils)
            for gi, (window, dil) in enumerate(C_GROUPS):
                o, lse = _band_attention(qkvs[gi], p["c_bias"][gi], window // (2 * dil), want_lse=True)
                o_list.append(o)
                lse_list.append(lse)
            mix = (o_list, lse_list, p["c_wo"][j])
        final_g = p["final_g"] if i == depth - 1 else None
        x = _ffn(x, p["norm_ffn_g"][i], p["ffn_w1"][i], p["ffn_w2"][i], mix=mix, final_g=final_g)
    return x


def kernel(x_prompt, x_sample, rel_bias, norm_mix_g, norm_ffn_g, final_g, ffn_w1, ffn_w2, a_wqkv,
           a_sink, a_wo, b_win, b_ln_g, b_ln_b, b_ws, b_bs, b_wo, c_wqkv, c_wo):
    bf16 = jnp.bfloat16
    f32 = jnp.float32
    S = x_prompt.shape[1]
    assert x_sample.shape[1] == S
    p = {
        "norm_mix_g": norm_mix_g.astype(f32), "norm_ffn_g": norm_ffn_g.astype(f32),
        "final_g": final_g.astype(f32),
        "ffn_w1": ffn_w1.astype(bf16), "ffn_w2": ffn_w2.astype(bf16),
        "a_wqkv": a_wqkv.astype(bf16), "a_sink": a_sink, "a_wo": a_wo.astype(bf16),
        "b_win": b_win.astype(bf16), "b_ln_g": b_ln_g, "b_ln_b": b_ln_b,
        "b_ws": b_ws.astype(bf16), "b_bs": b_bs, "b_wo": b_wo.astype(bf16),
        "c_wqkv": c_wqkv.astype(bf16), "c_wo": c_wo.astype(bf16),
        "a_bias": _band_bias_table(rel_bias, ATTN_TQ, A_RADIUS, 1, S),
        "c_bias": [_band_bias_table(rel_bias, ATTN_TQ, window // (2 * dil), dil, S // dil)
                   for window, dil in C_GROUPS],
    }
    return _trunk(x_prompt, p), _trunk(x_sample, p)
```

```python
import functools
import math

import numpy as np
import jax
import jax.numpy as jnp
from jax import lax
from jax.experimental import pallas as pl
from jax.experimental.pallas import tpu as pltpu

HEAD_DIM = 64
N_MIXERS = 3
N_HEADS = 16
N_KV_HEADS = 4
A_RADIUS = 128
B_CHUNK = 128
B_GROUPS = 8
C_GROUPS = ((128, 1), (512, 4), (2048, 16))
NUM_BUCKETS = 32
REL_MAX_DISTANCE = 1024
RMS_EPS = 1e-6
LN_EPS = 1e-5

Q_COLS = N_HEADS * HEAD_DIM
KV_COLS = N_KV_HEADS * HEAD_DIM
QKV_COLS = Q_COLS + 2 * KV_COLS
LANES = 128
HEADS_PER_TILE = LANES // HEAD_DIM
Q_PER_KV = N_HEADS // N_KV_HEADS
KVX_COLS = N_KV_HEADS * HEADS_PER_TILE * LANES
MASK_VALUE = -1e30
LOG2_E = math.log2(math.e)
ROWS_PER_WORD = 2
LN_2 = math.log(2.0)

V7X_VMEM_BYTES = 64 * 1024 * 1024
VMEM_LIMIT_BYTES = V7X_VMEM_BYTES - 8 * 1024 * 1024

ATTN_TQ = 128
ATTN_STEP_ROWS = 1024
FFN_TM = 1024
FFN_TM_MERGE = 512
FFN_CHUNK = 1024
PROJ_TM = 1024
PROJ_TM_MULTI = 512
GMLP_TM = 512
GMLP_SUB = 256


def _const_spec(shape):
    zeros = (0,) * len(shape)
    return pl.BlockSpec(shape, lambda *_: zeros, pipeline_mode=pl.Buffered(1))


def _params(n_grid_dims):
    return pltpu.CompilerParams(
        dimension_semantics=("arbitrary",) * n_grid_dims,
        vmem_limit_bytes=VMEM_LIMIT_BYTES,
    )


def _rmsnorm_f32(x, g):
    ms = jnp.mean(x * x, axis=-1, keepdims=True)
    return x * lax.rsqrt(ms + RMS_EPS) * g


def _lse_lane(head):
    return head // HEADS_PER_TILE + HEAD_DIM * (head % HEADS_PER_TILE)


def _proj_kernel(*refs, dils):
    n = len(dils)
    x_ref, g_ref = refs[:2]
    w_refs = refs[2:2 + n]
    out_refs = refs[2 + n:2 + 4 * n]
    hn = _rmsnorm_f32(x_ref[...], g_ref[...])
    tm, d_model = hn.shape
    n_slabs = d_model // LANES
    if any(d > 1 for d in dils):
        slab_ref = refs[2 + 4 * n]
        for j in range(n_slabs):
            slab_ref[j] = hn[:, j * LANES:(j + 1) * LANES]
    low_half = lax.broadcasted_iota(jnp.int32, (tm, LANES), 1) < HEAD_DIM
    for gi, (dil, w_ref) in enumerate(zip(dils, w_refs)):
        q_ref, kx_ref, vx_ref = out_refs[3 * gi:3 * gi + 3]
        n_rows = tm // dil
        if dil == 1:
            h = hn
        else:
            h = jnp.concatenate(
                [jnp.concatenate([slab_ref[j, pl.ds(r, n_rows, stride=dil), :] for j in range(n_slabs)], axis=1)
                 for r in range(dil)], axis=0)
        hb = h.astype(jnp.bfloat16)
        y_kv = jnp.dot(hb, w_ref[:, Q_COLS:], preferred_element_type=jnp.float32)
        y_q = jnp.dot(hb, w_ref[:, :Q_COLS], preferred_element_type=jnp.float32)
        pieces = []
        for o_ref, col0 in ((kx_ref, 0), (vx_ref, KV_COLS)):
            tiles = []
            for t in range(KV_COLS // LANES):
                pair = y_kv[:, col0 + t * LANES:col0 + (t + 1) * LANES]
                only_a = jnp.where(low_half, pair, 0.0)
                only_b = jnp.where(low_half, 0.0, pair)
                tiles += [only_a, pltpu.roll(only_a, HEAD_DIM, 1), pltpu.roll(only_b, HEAD_DIM, 1), only_b]
            pieces.append((o_ref, jnp.concatenate(tiles, axis=1).astype(jnp.bfloat16)))
        pieces.append((q_ref, (y_q * (HEAD_DIM ** -0.5 * LOG2_E)).astype(jnp.bfloat16)))
        for o_ref, val in pieces:
            words = pltpu.bitcast(val, o_ref.dtype)
            n_words = n_rows // ROWS_PER_WORD
            for r in range(dil):
                o_ref[r] = words[r * n_words:(r + 1) * n_words]


def _qkv_projection(x, g, weights, dils):
    B, S, D = x.shape
    n = len(dils)
    tm = PROJ_TM if n == 1 else PROJ_TM_MULTI
    in_specs = [pl.BlockSpec((None, tm, D), lambda b, i: (b, i, 0)), _const_spec((1, D))]
    in_specs += [_const_spec((D, QKV_COLS))] * n
    out_specs, out_shape = [], []
    for dil in dils:
        for cols in (Q_COLS, KVX_COLS, KVX_COLS):
            out_specs.append(pl.BlockSpec((None, dil, tm // dil // ROWS_PER_WORD, cols),
                                          lambda b, i: (b, 0, i, 0)))
            out_shape.append(jax.ShapeDtypeStruct((B, dil, S // dil // ROWS_PER_WORD, cols), jnp.uint32))
    scratch = []
    if any(d > 1 for d in dils):
        scratch.append(pltpu.VMEM((D // LANES, tm, LANES), jnp.float32))
    outs = pl.pallas_call(
        functools.partial(_proj_kernel, dils=tuple(dils)),
        grid=(B, S // tm),
        in_specs=in_specs,
        out_specs=out_specs,
        out_shape=out_shape,
        scratch_shapes=scratch,
        compiler_params=_params(2),
        name="qkv_proj_" + "_".join(f"d{d}" for d in dils),
    )(x, g.reshape(1, D), *weights)
    return [tuple(outs[3 * gi:3 * gi + 3]) for gi in range(n)]


def _rel_bucket(rel):
    half = NUM_BUCKETS // 2
    max_exact = half // 2
    n = np.abs(rel)
    large = max_exact + (np.log(np.maximum(n, 1) / max_exact) / np.log(REL_MAX_DISTANCE / max_exact)
                         * (half - max_exact)).astype(np.int32)
    large = np.minimum(large, half - 1)
    return (rel > 0).astype(np.int32) * half + np.where(n < max_exact, n, large)


def _window_geometry(tq, radius, seq_len):
    width = min(tq + 2 * radius, seq_len)
    if seq_len // tq == 1:
        return width, (0,)
    return width, (0, radius, width - tq)


def _dot_heads(g, v):
    return g * Q_PER_KV + v, g * Q_PER_KV + HEADS_PER_TILE + v


def _band_bias_table(rel_bias, tq, radius, dil, seq_len):
    width, offsets = _window_geometry(tq, radius, seq_len)
    span = max(offsets) + tq - 1
    n_diag = span + width
    diag_rel = np.arange(n_diag) - span
    per_diag = jnp.transpose(rel_bias[_rel_bucket(diag_rel * dil)], (1, 0)).astype(jnp.float32) * LOG2_E
    per_diag = jnp.where(jnp.asarray(np.abs(diag_rel) <= radius)[None], per_diag, MASK_VALUE)
    padded = jnp.pad(per_diag, ((0, 0), (0, 1)))
    tiled = jnp.broadcast_to(padded[:, None, :], (N_HEADS, tq, n_diag + 1)).reshape(N_HEADS, -1)
    skewed = tiled[:, :tq * n_diag].reshape(N_HEADS, tq, n_diag)
    order = [h for g in range(N_KV_HEADS) for v in range(HEADS_PER_TILE) for h in _dot_heads(g, v)]
    skewed = skewed[np.asarray(order)]
    tables = []
    for off in offsets:
        c0 = span - off
        tables.append(skewed[:, :, c0:c0 + width].reshape(N_HEADS * tq, width))
    return jnp.stack(tables, axis=0)


def _attn_kernel(*refs, tq, radius, seq_len, dil, has_sink, want_lse):
    refs = list(refs)
    q_ref, kx_ref, vx_ref, bias_ref = refs[:4]
    pos = 4
    sink_ref = None
    if has_sink:
        sink_ref = refs[pos]
        pos += 1
    o_ref = refs[pos]
    pos += 1
    lse_ref = refs[pos] if want_lse else None

    n_res = q_ref.shape[0]
    blocks = q_ref.shape[1] * ROWS_PER_WORD // tq
    width, offsets = _window_geometry(tq, radius, seq_len)
    n_blk = seq_len // tq
    i = pl.program_id(2)
    bf16 = jnp.bfloat16

    lane = lax.broadcasted_iota(jnp.int32, (tq, LANES), 1)
    low_half2 = lax.broadcasted_iota(jnp.int32, (2 * tq, LANES), 1) < HEAD_DIM
    top_rows = lax.broadcasted_iota(jnp.int32, (2 * tq, 1), 0) < tq
    ones_lo = jnp.where(lax.broadcasted_iota(jnp.int32, (width, LANES), 1) < HEAD_DIM, 1.0, 0.0).astype(bf16)
    ones_hi = jnp.where(lax.broadcasted_iota(jnp.int32, (width, LANES), 1) < HEAD_DIM, 0.0, 1.0).astype(bf16)

    def window_start(j):
        blk = i * blocks + j
        if n_blk == 1:
            return 0
        return jnp.clip(blk * tq - radius, 0, seq_len - width)

    def load_rows(ref, res, start, n, lanes):
        word_start = start // ROWS_PER_WORD
        if not isinstance(word_start, int):
            word_start = pl.multiple_of(word_start, HEAD_DIM // ROWS_PER_WORD)
        words = ref[res, pl.ds(word_start, n // ROWS_PER_WORD), lanes]
        return pltpu.bitcast(words, bf16)

    def scores(res, j, g):
        base = g * Q_PER_KV * HEAD_DIM
        q2 = jnp.concatenate([load_rows(q_ref, res, j * tq, tq, slice(base, base + LANES)),
                              load_rows(q_ref, res, j * tq, tq, slice(base + LANES, base + 2 * LANES))],
                             axis=0)
        k_win = jnp.concatenate(
            [load_rows(kx_ref, res, window_start(j), width, slice(t * LANES, (t + 1) * LANES))
             for t in range(g * HEADS_PER_TILE, (g + 1) * HEADS_PER_TILE)], axis=0)
        return lax.dot_general(q2, k_win, (((1,), (1,)), ((), ())), preferred_element_type=jnp.float32)

    work = [(res, j, g) for res in range(n_res) for j in range(blocks) for g in range(N_KV_HEADS)]
    s_next = scores(*work[0])
    for n, (res, j, g) in enumerate(work):
        base = g * Q_PER_KV * HEAD_DIM
        rows = slice(j * tq, (j + 1) * tq)
        blk = i * blocks + j
        if n_blk == 1:
            variant = 0
        else:
            variant = jnp.where(blk == 0, 0, jnp.where(blk == n_blk - 1, 2, 1))
        if dil > 1:
            token_rows = pl.ds(blk * (tq * dil) + pl.program_id(1) * n_res + res, tq, stride=dil)
        s_all = s_next
        if n + 1 < len(work):
            s_next = scores(*work[n + 1])
        probs = []
        maxes = []
        sink_terms = []
        for v in range(HEADS_PER_TILE):
            t = g * HEADS_PER_TILE + v
            s = s_all[:, v * width:(v + 1) * width] + bias_ref[variant, t * 2 * tq:(t + 1) * 2 * tq, :]
            m = jnp.max(s, axis=-1, keepdims=True)
            if has_sink:
                h_top, h_bot = _dot_heads(g, v)
                sk = jnp.where(top_rows, sink_ref[h_top], sink_ref[h_bot])
                m = jnp.maximum(m, sk)
                sink_terms.append(jnp.exp2(sk - m))
            maxes.append(m)
            probs.append(jnp.exp2(s - m).astype(bf16))
        v_tiles = [load_rows(vx_ref, res, window_start(j), width, slice(t * LANES, (t + 1) * LANES))
                   for t in range(g * HEADS_PER_TILE, (g + 1) * HEADS_PER_TILE)]
        v_rhs = jnp.concatenate([jnp.concatenate([v_tiles[0], ones_lo], axis=1),
                                 jnp.concatenate([v_tiles[1], ones_hi], axis=1)], axis=0)
        acc = jnp.dot(jnp.concatenate(probs, axis=1), v_rhs, preferred_element_type=jnp.float32)
        denom = acc[:, LANES:]
        if has_sink:
            denom = denom + jnp.where(low_half2, sink_terms[0], sink_terms[1])
        o = acc[:, :LANES] / denom
        if dil == 1:
            o_ref[rows, base:base + LANES] = o[:tq].astype(o_ref.dtype)
            o_ref[rows, base + LANES:base + 2 * LANES] = o[tq:].astype(o_ref.dtype)
        else:
            o_ref[g * HEADS_PER_TILE, token_rows, :] = o[:tq]
            o_ref[g * HEADS_PER_TILE + 1, token_rows, :] = o[tq:]
        if want_lse:
            if g == 0:
                lse_acc = jnp.zeros((tq, LANES), jnp.float32)
            lse = jnp.where(low_half2, maxes[0] * LN_2, maxes[1] * LN_2) + jnp.log(denom)
            for half, half_rows in ((0, slice(0, tq)), (1, slice(tq, 2 * tq))):
                tile = g * HEADS_PER_TILE + half
                here = (lane == tile) | (lane == HEAD_DIM + tile)
                lse_acc = jnp.where(here, lse[half_rows], lse_acc)
            if g == N_KV_HEADS - 1:
                if dil == 1:
                    lse_ref[rows, :] = lse_acc
                else:
                    lse_ref[token_rows, :] = lse_acc


def _band_attention(qkv, bias, radius, *, sink=None, want_lse=False):
    q, kx, vx = qkv
    B, dil, L = q.shape[0], q.shape[1], q.shape[2] * ROWS_PER_WORD
    S = L * dil
    tq = ATTN_TQ
    ts = min(L, ATTN_STEP_ROWS)
    n_res = min(dil, ATTN_STEP_ROWS // ts)
    D = Q_COLS

    in_specs = [
        pl.BlockSpec((None, n_res, ts // ROWS_PER_WORD, Q_COLS), lambda b, r, i: (b, r, i, 0)),
        pl.BlockSpec((None, n_res, L // ROWS_PER_WORD, KVX_COLS), lambda b, r, i: (b, r, 0, 0)),
        pl.BlockSpec((None, n_res, L // ROWS_PER_WORD, KVX_COLS), lambda b, r, i: (b, r, 0, 0)),
        _const_spec(bias.shape),
    ]
    args = [q, kx, vx, bias]
    if sink is not None:
        in_specs.append(pl.BlockSpec(memory_space=pltpu.SMEM))
        args.append(sink.astype(jnp.float32) * LOG2_E)
    if dil == 1:
        out_specs = [pl.BlockSpec((None, ts, D), lambda b, r, i: (b, i, 0))]
        out_shape = [jax.ShapeDtypeStruct((B, S, D), jnp.bfloat16)]
        lse_spec = pl.BlockSpec((None, ts, LANES), lambda b, r, i: (b, i, 0))
    else:
        assert want_lse
        out_specs = [pl.BlockSpec((None, D // LANES, S, LANES), lambda b, r, i: (b, 0, 0, 0))]
        out_shape = [jax.ShapeDtypeStruct((B, D // LANES, S, LANES), jnp.float32)]
        lse_spec = pl.BlockSpec((None, S, LANES), lambda b, r, i: (b, 0, 0))
    if want_lse:
        out_specs.append(lse_spec)
        out_shape.append(jax.ShapeDtypeStruct((B, S, LANES), jnp.float32))

    kernel = functools.partial(_attn_kernel, tq=tq, radius=radius, seq_len=L, dil=dil,
                               has_sink=sink is not None, want_lse=want_lse)
    out = pl.pallas_call(
        kernel,
        grid=(B, dil // n_res, L // ts),
        in_specs=in_specs,
        out_specs=out_specs,
        out_shape=out_shape,
        compiler_params=_params(3),
        name=f"band_attn_r{radius}_d{dil}",
    )(*args)
    if want_lse:
        return out[0], out[1]
    return out[0]


def _head_expand_matrix():
    e = np.zeros((LANES, Q_COLS), np.float32)
    for h in range(N_HEADS):
        e[_lse_lane(h), h * HEAD_DIM:(h + 1) * HEAD_DIM] = 1.0
    return jnp.asarray(np.concatenate([e, e], axis=0), jnp.bfloat16)


def _read_rows_f32(o_ref):
    if len(o_ref.shape) == 2:
        return o_ref[...].astype(jnp.float32)
    return jnp.concatenate([o_ref[j] for j in range(o_ref.shape[0])], axis=1)


def _ffn_kernel(*refs, n_mix, final):
    refs = list(refs)
    x_ref = refs[0]
    pos = 1
    x = x_ref[...]
    if n_mix >= 1:
        o_refs = refs[pos:pos + n_mix]
        pos += n_mix
        if n_mix > 1:
            lse_refs = refs[pos:pos + n_mix]
            expand_ref = refs[pos + n_mix]
            pos += n_mix + 1
        wo_ref = refs[pos]
        pos += 1
        if n_mix == 1:
            o = o_refs[0][...]
        else:
            lses = [r[...] for r in lse_refs]
            top = functools.reduce(jnp.maximum, lses)
            es = [jnp.exp(l - top) for l in lses]
            inv_tot = 1.0 / functools.reduce(lambda a, b: a + b, es)
            o = None
            for e, o_ref in zip(es, o_refs):
                w = e * inv_tot
                w_hi = w.astype(jnp.bfloat16)
                w_lo = (w - w_hi.astype(jnp.float32)).astype(jnp.bfloat16)
                w_full = jnp.dot(jnp.concatenate([w_hi, w_lo], axis=1), expand_ref[...],
                                 preferred_element_type=jnp.float32)
                term = w_full * _read_rows_f32(o_ref)
                o = term if o is None else o + term
            o = o.astype(jnp.bfloat16)
        x = x + jnp.dot(o, wo_ref[...], preferred_element_type=jnp.float32)
    g_ref, w1_ref, w2_ref = refs[pos:pos + 3]
    pos += 3
    if final:
        fg_ref = refs[pos]
        pos += 1
    out_ref = refs[pos]

    h = _rmsnorm_f32(x, g_ref[...]).astype(jnp.bfloat16)
    acc = x
    d_ff = w1_ref.shape[1]
    for c in range(d_ff // FFN_CHUNK):
        cols = slice(c * FFN_CHUNK, (c + 1) * FFN_CHUNK)
        a = jnp.dot(h, w1_ref[:, cols], preferred_element_type=jnp.float32)
        a = jnp.square(jnp.maximum(a, 0.0)).astype(jnp.bfloat16)
        acc = acc + jnp.dot(a, w2_ref[cols, :], preferred_element_type=jnp.float32)
    if final:
        acc = _rmsnorm_f32(acc, fg_ref[...])
    out_ref[...] = acc


def _ffn(x, g, w1, w2, *, mix=None, final_g=None):
    B, S, D = x.shape
    d_ff = w1.shape[1]
    tm = FFN_TM_MERGE if mix is not None and len(mix[0]) > 1 else FFN_TM
    row_spec = lambda c: pl.BlockSpec((None, tm, c), lambda b, i: (b, i, 0))
    slab_spec = lambda c: pl.BlockSpec((None, c // LANES, tm, LANES), lambda b, i: (b, 0, i, 0))
    in_specs = [row_spec(D)]
    args = [x]
    n_mix = 0
    if mix is not None:
        o_list, lse_list, w_o = mix
        n_mix = len(o_list)
        in_specs += [row_spec(D) if o.ndim == 3 else slab_spec(D) for o in o_list]
        args += o_list
        if n_mix > 1:
            in_specs += [row_spec(LANES)] * n_mix + [_const_spec((2 * LANES, Q_COLS))]
            args += lse_list + [_head_expand_matrix()]
        in_specs.append(_const_spec((D, D)))
        args.append(w_o)
    in_specs += [_const_spec((1, D)), _const_spec((D, d_ff)), _const_spec((d_ff, D))]
    args += [g.reshape(1, D), w1, w2]
    if final_g is not None:
        in_specs.append(_const_spec((1, D)))
        args.append(final_g.reshape(1, D))
    out = pl.pallas_call(
        functools.partial(_ffn_kernel, n_mix=n_mix, final=final_g is not None),
        grid=(B, S // tm),
        in_specs=in_specs,
        out_specs=row_spec(D),
        out_shape=jax.ShapeDtypeStruct((B, S, D), jnp.float32),
        compiler_params=_params(2),
        name=f"ffn_mix{n_mix}" + ("_final" if final_g is not None else ""),
    )(*args)
    return out


def _gmlp_kernel(x_ref, g_ref, win_ref, lng_ref, lnb_ref, ws_ref, bs_ref, wout_ref, o_ref):
    tm = x_ref.shape[0]
    hidden = wout_ref.shape[0]
    gcols = hidden // B_GROUPS
    subs = [slice(r0, r0 + GMLP_SUB) for r0 in range(0, tm, GMLP_SUB)]
    zs = []
    for rows in subs:
        h = _rmsnorm_f32(x_ref[rows, :], g_ref[...]).astype(jnp.bfloat16)
        zs.append(jnp.dot(h, win_ref[...], preferred_element_type=jnp.float32))
    for rows, z in zip(subs, zs):
        z = 0.5 * z * (1.0 + lax.erf(z * (2.0 ** -0.5)))
        u = z[:, :hidden]
        v = z[:, hidden:]
        mu = jnp.mean(v, axis=-1, keepdims=True)
        vc = v - mu
        var = jnp.mean(vc * vc, axis=-1, keepdims=True)
        vn = (vc * lax.rsqrt(var + LN_EPS) * lng_ref[...] + lnb_ref[...]).astype(jnp.bfloat16)
        gated = []
        for r0 in range(0, GMLP_SUB, B_CHUNK):
            parts = []
            for grp in range(B_GROUPS):
                parts.append(jnp.dot(ws_ref[grp], vn[r0:r0 + B_CHUNK, grp * gcols:(grp + 1) * gcols],
                                     preferred_element_type=jnp.float32))
            mixed = jnp.concatenate(parts, axis=-1) + bs_ref[...]
            gated.append((u[r0:r0 + B_CHUNK, :] * mixed).astype(jnp.bfloat16))
        t = jnp.concatenate(gated, axis=0)
        o_ref[rows, :] = x_ref[rows, :] + jnp.dot(t, wout_ref[...], preferred_element_type=jnp.float32)


def _gmlp(x, g, w_in, ln_g, ln_b, w_s, b_s, w_out):
    B, S, D = x.shape
    hidden = w_out.shape[0]
    T = B * S
    tm = GMLP_TM
    bs_full = jnp.repeat(jnp.transpose(b_s).astype(jnp.float32), hidden // B_GROUPS, axis=1)
    row_spec = pl.BlockSpec((tm, D), lambda i: (i, 0))
    out = pl.pallas_call(
        _gmlp_kernel,
        grid=(T // tm,),
        in_specs=[row_spec, _const_spec((1, D)), _const_spec((D, 2 * hidden)),
                  _const_spec((1, hidden)), _const_spec((1, hidden)),
                  _const_spec((B_GROUPS, B_CHUNK, B_CHUNK)), _const_spec((B_CHUNK, hidden)),
                  _const_spec((hidden, D))],
        out_specs=row_spec,
        out_shape=jax.ShapeDtypeStruct((T, D), jnp.float32),
        compiler_params=_params(1),
        name="gmlp",
    )(x.reshape(T, D), g.reshape(1, D), w_in, ln_g.reshape(1, hidden).astype(jnp.float32),
      ln_b.reshape(1, hidden).astype(jnp.float32), w_s, bs_full, w_out)
    return out.reshape(B, S, D)


def _trunk(x, p):
    depth = p["norm_mix_g"].shape[0]
    for i in range(depth):
        kind, j = i % N_MIXERS, i // N_MIXERS
        g_mix = p["norm_mix_g"][i]
        mix = None
        if kind == 0:
            (qkv,) = _qkv_projection(x, g_mix, [p["a_wqkv"][j]], [1])
            o = _band_attention(qkv, p["a_bias"], A_RADIUS, sink=p["a_sink"][j])
            mix = ([o], None, p["a_wo"][j])
        elif kind == 1:
            x = _gmlp(x, g_mix, p["b_win"][j], p["b_ln_g"][j], p["b_ln_b"][j], p["b_ws"][j],
                      p["b_bs"][j], p["b_wo"][j])
        else:
            o_list, lse_list = [], []
            dils = [dil for _, dil in C_GROUPS]
            weights = [p["c_wqkv"][j][:, gi * QKV_COLS:(gi + 1) * QKV_COLS] for gi in range(len(dils))]
            qkvs = _qkv_projection(x, g_mix, weights, dils)
            for gi, (window, dil) in enumerate(C_GROUPS):
                o, lse = _band_attention(qkvs[gi], p["c_bias"][gi], window // (2 * dil), want_lse=True)
                o_list.append(o)
                lse_list.append(lse)
            mix = (o_list, lse_list, p["c_wo"][j])
        final_g = p["final_g"] if i == depth - 1 else None
        x = _ffn(x, p["norm_ffn_g"][i], p["ffn_w1"][i], p["ffn_w2"][i], mix=mix, final_g=final_g)
    return x


def kernel(x_prompt, x_sample, rel_bias, norm_mix_g, norm_ffn_g, final_g, ffn_w1, ffn_w2, a_wqkv,
           a_sink, a_wo, b_win, b_ln_g, b_ln_b, b_ws, b_bs, b_wo, c_wqkv, c_wo):
    bf16 = jnp.bfloat16
    f32 = jnp.float32
    S = x_prompt.shape[1]
    assert x_sample.shape[1] == S
    p = {
        "norm_mix_g": norm_mix_g.astype(f32), "norm_ffn_g": norm_ffn_g.astype(f32),
        "final_g": final_g.astype(f32),
        "ffn_w1": ffn_w1.astype(bf16), "ffn_w2": ffn_w2.astype(bf16),
        "a_wqkv": a_wqkv.astype(bf16), "a_sink": a_sink, "a_wo": a_wo.astype(bf16),
        "b_win": b_win.astype(bf16), "b_ln_g": b_ln_g, "b_ln_b": b_ln_b,
        "b_ws": b_ws.astype(bf16), "b_bs": b_bs, "b_wo": b_wo.astype(bf16),
        "c_wqkv": c_wqkv.astype(bf16), "c_wo": c_wo.astype(bf16),
        "a_bias": _band_bias_table(rel_bias, ATTN_TQ, A_RADIUS, 1, S),
        "c_bias": [_band_bias_table(rel_bias, ATTN_TQ, window // (2 * dil), dil, S // dil)
                   for window, dil in C_GROUPS],
    }
    return _trunk(x_prompt, p), _trunk(x_sample, p)
```

```python
import functools
import math

import numpy as np
import jax
import jax.numpy as jnp
from jax import lax
from jax.experimental import pallas as pl
from jax.experimental.pallas import tpu as pltpu

HEAD_DIM = 64
N_MIXERS = 3
N_HEADS = 16
N_KV_HEADS = 4
A_RADIUS = 128
B_CHUNK = 128
B_GROUPS = 8
C_GROUPS = ((128, 1), (512, 4), (2048, 16))
NUM_BUCKETS = 32
REL_MAX_DISTANCE = 1024
RMS_EPS = 1e-6
LN_EPS = 1e-5

Q_COLS = N_HEADS * HEAD_DIM
KV_COLS = N_KV_HEADS * HEAD_DIM
QKV_COLS = Q_COLS + 2 * KV_COLS
LANES = 128
HEADS_PER_TILE = LANES // HEAD_DIM
Q_PER_KV = N_HEADS // N_KV_HEADS
KVX_COLS = N_KV_HEADS * HEADS_PER_TILE * LANES
MASK_VALUE = -1e30
LOG2_E = math.log2(math.e)
ROWS_PER_WORD = 2
LN_2 = math.log(2.0)

V7X_VMEM_BYTES = 64 * 1024 * 1024
VMEM_LIMIT_BYTES = V7X_VMEM_BYTES - 8 * 1024 * 1024

ATTN_TQ = 128
ATTN_STEP_ROWS = 2048
FFN_TM = 1024
FFN_TM_MERGE = 512
FFN_CHUNK = 1024
PROJ_TM = 1024
PROJ_TM_MULTI = 512
GMLP_TM = 512
GMLP_SUB = 256


def _const_spec(shape):
    zeros = (0,) * len(shape)
    return pl.BlockSpec(shape, lambda *_: zeros, pipeline_mode=pl.Buffered(1))


def _params(n_grid_dims):
    return pltpu.CompilerParams(
        dimension_semantics=("arbitrary",) * n_grid_dims,
        vmem_limit_bytes=VMEM_LIMIT_BYTES,
    )


def _rmsnorm_f32(x, g):
    ms = jnp.mean(x * x, axis=-1, keepdims=True)
    return x * lax.rsqrt(ms + RMS_EPS) * g


def _lse_lane(head):
    return head // HEADS_PER_TILE + HEAD_DIM * (head % HEADS_PER_TILE)


def _proj_kernel(*refs, dils):
    n = len(dils)
    x_ref, g_ref = refs[:2]
    w_refs = refs[2:2 + n]
    out_refs = refs[2 + n:2 + 4 * n]
    hn = _rmsnorm_f32(x_ref[...], g_ref[...])
    tm, d_model = hn.shape
    n_slabs = d_model // LANES
    if any(d > 1 for d in dils):
        slab_ref = refs[2 + 4 * n]
        for j in range(n_slabs):
            slab_ref[j] = hn[:, j * LANES:(j + 1) * LANES]
    low_half = lax.broadcasted_iota(jnp.int32, (tm, LANES), 1) < HEAD_DIM
    for gi, (dil, w_ref) in enumerate(zip(dils, w_refs)):
        q_ref, kx_ref, vx_ref = out_refs[3 * gi:3 * gi + 3]
        n_rows = tm // dil
        if dil == 1:
            h = hn
        else:
            h = jnp.concatenate(
                [jnp.concatenate([slab_ref[j, pl.ds(r, n_rows, stride=dil), :] for j in range(n_slabs)], axis=1)
                 for r in range(dil)], axis=0)
        hb = h.astype(jnp.bfloat16)
        y_kv = jnp.dot(hb, w_ref[:, Q_COLS:], preferred_element_type=jnp.float32)
        y_q = jnp.dot(hb, w_ref[:, :Q_COLS], preferred_element_type=jnp.float32)
        pieces = []
        for o_ref, col0 in ((kx_ref, 0), (vx_ref, KV_COLS)):
            tiles = []
            for t in range(KV_COLS // LANES):
                pair = y_kv[:, col0 + t * LANES:col0 + (t + 1) * LANES]
                only_a = jnp.where(low_half, pair, 0.0)
                only_b = jnp.where(low_half, 0.0, pair)
                tiles += [only_a, pltpu.roll(only_a, HEAD_DIM, 1), pltpu.roll(only_b, HEAD_DIM, 1), only_b]
            pieces.append((o_ref, jnp.concatenate(tiles, axis=1).astype(jnp.bfloat16)))
        pieces.append((q_ref, (y_q * (HEAD_DIM ** -0.5 * LOG2_E)).astype(jnp.bfloat16)))
        for o_ref, val in pieces:
            words = pltpu.bitcast(val, o_ref.dtype)
            n_words = n_rows // ROWS_PER_WORD
            for r in range(dil):
                o_ref[r] = words[r * n_words:(r + 1) * n_words]


def _qkv_projection(x, g, weights, dils):
    B, S, D = x.shape
    n = len(dils)
    tm = PROJ_TM if n == 1 else PROJ_TM_MULTI
    in_specs = [pl.BlockSpec((None, tm, D), lambda b, i: (b, i, 0)), _const_spec((1, D))]
    in_specs += [_const_spec((D, QKV_COLS))] * n
    out_specs, out_shape = [], []
    for dil in dils:
        for cols in (Q_COLS, KVX_COLS, KVX_COLS):
            out_specs.append(pl.BlockSpec((None, dil, tm // dil // ROWS_PER_WORD, cols),
                                          lambda b, i: (b, 0, i, 0)))
            out_shape.append(jax.ShapeDtypeStruct((B, dil, S // dil // ROWS_PER_WORD, cols), jnp.uint32))
    scratch = []
    if any(d > 1 for d in dils):
        scratch.append(pltpu.VMEM((D // LANES, tm, LANES), jnp.float32))
    outs = pl.pallas_call(
        functools.partial(_proj_kernel, dils=tuple(dils)),
        grid=(B, S // tm),
        in_specs=in_specs,
        out_specs=out_specs,
        out_shape=out_shape,
        scratch_shapes=scratch,
        compiler_params=_params(2),
        name="qkv_proj_" + "_".join(f"d{d}" for d in dils),
    )(x, g.reshape(1, D), *weights)
    return [tuple(outs[3 * gi:3 * gi + 3]) for gi in range(n)]


def _rel_bucket(rel):
    half = NUM_BUCKETS // 2
    max_exact = half // 2
    n = np.abs(rel)
    large = max_exact + (np.log(np.maximum(n, 1) / max_exact) / np.log(REL_MAX_DISTANCE / max_exact)
                         * (half - max_exact)).astype(np.int32)
    large = np.minimum(large, half - 1)
    return (rel > 0).astype(np.int32) * half + np.where(n < max_exact, n, large)


def _window_geometry(tq, radius, seq_len):
    width = min(tq + 2 * radius, seq_len)
    if seq_len // tq == 1:
        return width, (0,)
    return width, (0, radius, width - tq)


def _dot_heads(g, v):
    return g * Q_PER_KV + v, g * Q_PER_KV + HEADS_PER_TILE + v


def _band_bias_table(rel_bias, tq, radius, dil, seq_len):
    width, offsets = _window_geometry(tq, radius, seq_len)
    span = max(offsets) + tq - 1
    n_diag = span + width
    diag_rel = np.arange(n_diag) - span
    per_diag = jnp.transpose(rel_bias[_rel_bucket(diag_rel * dil)], (1, 0)).astype(jnp.float32) * LOG2_E
    per_diag = jnp.where(jnp.asarray(np.abs(diag_rel) <= radius)[None], per_diag, MASK_VALUE)
    padded = jnp.pad(per_diag, ((0, 0), (0, 1)))
    tiled = jnp.broadcast_to(padded[:, None, :], (N_HEADS, tq, n_diag + 1)).reshape(N_HEADS, -1)
    skewed = tiled[:, :tq * n_diag].reshape(N_HEADS, tq, n_diag)
    order = [h for g in range(N_KV_HEADS) for v in range(HEADS_PER_TILE) for h in _dot_heads(g, v)]
    skewed = skewed[np.asarray(order)]
    tables = []
    for off in offsets:
        c0 = span - off
        tables.append(skewed[:, :, c0:c0 + width].reshape(N_HEADS * tq, width))
    return jnp.stack(tables, axis=0)


def _attn_kernel(*refs, tq, radius, seq_len, dil, has_sink, want_lse):
    refs = list(refs)
    q_ref, kx_ref, vx_ref, bias_ref = refs[:4]
    pos = 4
    sink_ref = None
    if has_sink:
        sink_ref = refs[pos]
        pos += 1
    o_ref = refs[pos]
    pos += 1
    lse_ref = refs[pos] if want_lse else None

    n_res = q_ref.shape[0]
    blocks = q_ref.shape[1] * ROWS_PER_WORD // tq
    width, offsets = _window_geometry(tq, radius, seq_len)
    n_blk = seq_len // tq
    i = pl.program_id(2)
    bf16 = jnp.bfloat16

    lane = lax.broadcasted_iota(jnp.int32, (tq, LANES), 1)
    low_half2 = lax.broadcasted_iota(jnp.int32, (2 * tq, LANES), 1) < HEAD_DIM
    top_rows = lax.broadcasted_iota(jnp.int32, (2 * tq, 1), 0) < tq
    ones_lo = jnp.where(lax.broadcasted_iota(jnp.int32, (width, LANES), 1) < HEAD_DIM, 1.0, 0.0).astype(bf16)
    ones_hi = jnp.where(lax.broadcasted_iota(jnp.int32, (width, LANES), 1) < HEAD_DIM, 0.0, 1.0).astype(bf16)

    def window_start(j):
        blk = i * blocks + j
        if n_blk == 1:
            return 0
        return jnp.clip(blk * tq - radius, 0, seq_len - width)

    def load_rows(ref, res, start, n, lanes):
        word_start = start // ROWS_PER_WORD
        if not isinstance(word_start, int):
            word_start = pl.multiple_of(word_start, HEAD_DIM // ROWS_PER_WORD)
        words = ref[res, pl.ds(word_start, n // ROWS_PER_WORD), lanes]
        return pltpu.bitcast(words, bf16)

    def scores(res, j, g):
        base = g * Q_PER_KV * HEAD_DIM
        q2 = jnp.concatenate([load_rows(q_ref, res, j * tq, tq, slice(base, base + LANES)),
                              load_rows(q_ref, res, j * tq, tq, slice(base + LANES, base + 2 * LANES))],
                             axis=0)
        k_win = jnp.concatenate(
            [load_rows(kx_ref, res, window_start(j), width, slice(t * LANES, (t + 1) * LANES))
             for t in range(g * HEADS_PER_TILE, (g + 1) * HEADS_PER_TILE)], axis=0)
        return lax.dot_general(q2, k_win, (((1,), (1,)), ((), ())), preferred_element_type=jnp.float32)

    work = [(res, j, g) for res in range(n_res) for j in range(blocks) for g in range(N_KV_HEADS)]
    s_next = scores(*work[0])
    for n, (res, j, g) in enumerate(work):
        base = g * Q_PER_KV * HEAD_DIM
        rows = slice(j * tq, (j + 1) * tq)
        blk = i * blocks + j
        if n_blk == 1:
            variant = 0
        else:
            variant = jnp.where(blk == 0, 0, jnp.where(blk == n_blk - 1, 2, 1))
        if dil > 1:
            token_rows = pl.ds(blk * (tq * dil) + pl.program_id(1) * n_res + res, tq, stride=dil)
        s_all = s_next
        if n + 1 < len(work):
            s_next = scores(*work[n + 1])
        probs = []
        maxes = []
        sink_terms = []
        for v in range(HEADS_PER_TILE):
            t = g * HEADS_PER_TILE + v
            s = s_all[:, v * width:(v + 1) * width] + bias_ref[variant, t * 2 * tq:(t + 1) * 2 * tq, :]
            m = jnp.max(s, axis=-1, keepdims=True)
            if has_sink:
                h_top, h_bot = _dot_heads(g, v)
                sk = jnp.where(top_rows, sink_ref[h_top], sink_ref[h_bot])
                m = jnp.maximum(m, sk)
                sink_terms.append(jnp.exp2(sk - m))
            maxes.append(m)
            probs.append(jnp.exp2(s - m).astype(bf16))
        v_tiles = [load_rows(vx_ref, res, window_start(j), width, slice(t * LANES, (t + 1) * LANES))
                   for t in range(g * HEADS_PER_TILE, (g + 1) * HEADS_PER_TILE)]
        v_rhs = jnp.concatenate([jnp.concatenate([v_tiles[0], ones_lo], axis=1),
                                 jnp.concatenate([v_tiles[1], ones_hi], axis=1)], axis=0)
        acc = jnp.dot(jnp.concatenate(probs, axis=1), v_rhs, preferred_element_type=jnp.float32)
        denom = acc[:, LANES:]
        if has_sink:
            denom = denom + jnp.where(low_half2, sink_terms[0], sink_terms[1])
        o = acc[:, :LANES] / denom
        if dil == 1:
            o_ref[rows, base:base + LANES] = o[:tq].astype(o_ref.dtype)
            o_ref[rows, base + LANES:base + 2 * LANES] = o[tq:].astype(o_ref.dtype)
        else:
            o_ref[g * HEADS_PER_TILE, token_rows, :] = o[:tq]
            o_ref[g * HEADS_PER_TILE + 1, token_rows, :] = o[tq:]
        if want_lse:
            if g == 0:
                lse_acc = jnp.zeros((tq, LANES), jnp.float32)
            lse = jnp.where(low_half2, maxes[0] * LN_2, maxes[1] * LN_2) + jnp.log(denom)
            for half, half_rows in ((0, slice(0, tq)), (1, slice(tq, 2 * tq))):
                tile = g * HEADS_PER_TILE + half
                here = (lane == tile) | (lane == HEAD_DIM + tile)
                lse_acc = jnp.where(here, lse[half_rows], lse_acc)
            if g == N_KV_HEADS - 1:
                if dil == 1:
                    lse_ref[rows, :] = lse_acc
                else:
                    lse_ref[token_rows, :] = lse_acc


def _band_attention(qkv, bias, radius, *, sink=None, want_lse=False):
    q, kx, vx = qkv
    B, dil, L = q.shape[0], q.shape[1], q.shape[2] * ROWS_PER_WORD
    S = L * dil
    tq = ATTN_TQ
    ts = min(L, ATTN_STEP_ROWS)
    n_res = min(dil, ATTN_STEP_ROWS // ts)
    D = Q_COLS

    in_specs = [
        pl.BlockSpec((None, n_res, ts // ROWS_PER_WORD, Q_COLS), lambda b, r, i: (b, r, i, 0)),
        pl.BlockSpec((None, n_res, L // ROWS_PER_WORD, KVX_COLS), lambda b, r, i: (b, r, 0, 0)),
        pl.BlockSpec((None, n_res, L // ROWS_PER_WORD, KVX_COLS), lambda b, r, i: (b, r, 0, 0)),
        _const_spec(bias.shape),
    ]
    args = [q, kx, vx, bias]
    if sink is not None:
        in_specs.append(pl.BlockSpec(memory_space=pltpu.SMEM))
        args.append(sink.astype(jnp.float32) * LOG2_E)
    if dil == 1:
        out_specs = [pl.BlockSpec((None, ts, D), lambda b, r, i: (b, i, 0))]
        out_shape = [jax.ShapeDtypeStruct((B, S, D), jnp.bfloat16)]
        lse_spec = pl.BlockSpec((None, ts, LANES), lambda b, r, i: (b, i, 0))
    else:
        assert want_lse
        out_specs = [pl.BlockSpec((None, D // LANES, S, LANES), lambda b, r, i: (b, 0, 0, 0))]
        out_shape = [jax.ShapeDtypeStruct((B, D // LANES, S, LANES), jnp.float32)]
        lse_spec = pl.BlockSpec((None, S, LANES), lambda b, r, i: (b, 0, 0))
    if want_lse:
        out_specs.append(lse_spec)
        out_shape.append(jax.ShapeDtypeStruct((B, S, LANES), jnp.float32))

    kernel = functools.partial(_attn_kernel, tq=tq, radius=radius, seq_len=L, dil=dil,
                               has_sink=sink is not None, want_lse=want_lse)
    out = pl.pallas_call(
        kernel,
        grid=(B, dil // n_res, L // ts),
        in_specs=in_specs,
        out_specs=out_specs,
        out_shape=out_shape,
        compiler_params=_params(3),
        name=f"band_attn_r{radius}_d{dil}",
    )(*args)
    if want_lse:
        return out[0], out[1]
    return out[0]


def _head_expand_matrix():
    e = np.zeros((LANES, Q_COLS), np.float32)
    for h in range(N_HEADS):
        e[_lse_lane(h), h * HEAD_DIM:(h + 1) * HEAD_DIM] = 1.0
    return jnp.asarray(np.concatenate([e, e], axis=0), jnp.bfloat16)


def _read_rows_f32(o_ref):
    if len(o_ref.shape) == 2:
        return o_ref[...].astype(jnp.float32)
    return jnp.concatenate([o_ref[j] for j in range(o_ref.shape[0])], axis=1)


def _ffn_kernel(*refs, n_mix, final):
    refs = list(refs)
    x_ref = refs[0]
    pos = 1
    x = x_ref[...]
    if n_mix >= 1:
        o_refs = refs[pos:pos + n_mix]
        pos += n_mix
        if n_mix > 1:
            lse_refs = refs[pos:pos + n_mix]
            expand_ref = refs[pos + n_mix]
            pos += n_mix + 1
        wo_ref = refs[pos]
        pos += 1
        if n_mix == 1:
            o = o_refs[0][...]
        else:
            lses = [r[...] for r in lse_refs]
            top = functools.reduce(jnp.maximum, lses)
            es = [jnp.exp(l - top) for l in lses]
            inv_tot = 1.0 / functools.reduce(lambda a, b: a + b, es)
            o = None
            for e, o_ref in zip(es, o_refs):
                w = e * inv_tot
                w_hi = w.astype(jnp.bfloat16)
                w_lo = (w - w_hi.astype(jnp.float32)).astype(jnp.bfloat16)
                w_full = jnp.dot(jnp.concatenate([w_hi, w_lo], axis=1), expand_ref[...],
                                 preferred_element_type=jnp.float32)
                term = w_full * _read_rows_f32(o_ref)
                o = term if o is None else o + term
            o = o.astype(jnp.bfloat16)
        x = x + jnp.dot(o, wo_ref[...], preferred_element_type=jnp.float32)
    g_ref, w1_ref, w2_ref = refs[pos:pos + 3]
    pos += 3
    if final:
        fg_ref = refs[pos]
        pos += 1
    out_ref = refs[pos]

    h = _rmsnorm_f32(x, g_ref[...]).astype(jnp.bfloat16)
    acc = x
    d_ff = w1_ref.shape[1]
    for c in range(d_ff // FFN_CHUNK):
        cols = slice(c * FFN_CHUNK, (c + 1) * FFN_CHUNK)
        a = jnp.dot(h, w1_ref[:, cols], preferred_element_type=jnp.float32)
        a = jnp.square(jnp.maximum(a, 0.0)).astype(jnp.bfloat16)
        acc = acc + jnp.dot(a, w2_ref[cols, :], preferred_element_type=jnp.float32)
    if final:
        acc = _rmsnorm_f32(acc, fg_ref[...])
    out_ref[...] = acc


def _ffn(x, g, w1, w2, *, mix=None, final_g=None):
    B, S, D = x.shape
    d_ff = w1.shape[1]
    tm = FFN_TM_MERGE if mix is not None and len(mix[0]) > 1 else FFN_TM
    row_spec = lambda c: pl.BlockSpec((None, tm, c), lambda b, i: (b, i, 0))
    slab_spec = lambda c: pl.BlockSpec((None, c // LANES, tm, LANES), lambda b, i: (b, 0, i, 0))
    in_specs = [row_spec(D)]
    args = [x]
    n_mix = 0
    if mix is not None:
        o_list, lse_list, w_o = mix
        n_mix = len(o_list)
        in_specs += [row_spec(D) if o.ndim == 3 else slab_spec(D) for o in o_list]
        args += o_list
        if n_mix > 1:
            in_specs += [row_spec(LANES)] * n_mix + [_const_spec((2 * LANES, Q_COLS))]
            args += lse_list + [_head_expand_matrix()]
        in_specs.append(_const_spec((D, D)))
        args.append(w_o)
    in_specs += [_const_spec((1, D)), _const_spec((D, d_ff)), _const_spec((d_ff, D))]
    args += [g.reshape(1, D), w1, w2]
    if final_g is not None:
        in_specs.append(_const_spec((1, D)))
        args.append(final_g.reshape(1, D))
    out = pl.pallas_call(
        functools.partial(_ffn_kernel, n_mix=n_mix, final=final_g is not None),
        grid=(B, S // tm),
        in_specs=in_specs,
        out_specs=row_spec(D),
        out_shape=jax.ShapeDtypeStruct((B, S, D), jnp.float32),
        compiler_params=_params(2),
        name=f"ffn_mix{n_mix}" + ("_final" if final_g is not None else ""),
    )(*args)
    return out


def _gmlp_kernel(x_ref, g_ref, win_ref, lng_ref, lnb_ref, ws_ref, bs_ref, wout_ref, o_ref):
    tm = x_ref.shape[0]
    hidden = wout_ref.shape[0]
    gcols = hidden // B_GROUPS
    subs = [slice(r0, r0 + GMLP_SUB) for r0 in range(0, tm, GMLP_SUB)]
    zs = []
    for rows in subs:
        h = _rmsnorm_f32(x_ref[rows, :], g_ref[...]).astype(jnp.bfloat16)
        zs.append(jnp.dot(h, win_ref[...], preferred_element_type=jnp.float32))
    for rows, z in zip(subs, zs):
        z = 0.5 * z * (1.0 + lax.erf(z * (2.0 ** -0.5)))
        u = z[:, :hidden]
        v = z[:, hidden:]
        mu = jnp.mean(v, axis=-1, keepdims=True)
        vc = v - mu
        var = jnp.mean(vc * vc, axis=-1, keepdims=True)
        vn = (vc * lax.rsqrt(var + LN_EPS) * lng_ref[...] + lnb_ref[...]).astype(jnp.bfloat16)
        gated = []
        for r0 in range(0, GMLP_SUB, B_CHUNK):
            parts = []
            for grp in range(B_GROUPS):
                parts.append(jnp.dot(ws_ref[grp], vn[r0:r0 + B_CHUNK, grp * gcols:(grp + 1) * gcols],
                                     preferred_element_type=jnp.float32))
            mixed = jnp.concatenate(parts, axis=-1) + bs_ref[...]
            gated.append((u[r0:r0 + B_CHUNK, :] * mixed).astype(jnp.bfloat16))
        t = jnp.concatenate(gated, axis=0)
        o_ref[rows, :] = x_ref[rows, :] + jnp.dot(t, wout_ref[...], preferred_element_type=jnp.float32)


def _gmlp(x, g, w_in, ln_g, ln_b, w_s, b_s, w_out):
    B, S, D = x.shape
    hidden = w_out.shape[0]
    T = B * S
    tm = GMLP_TM
    bs_full = jnp.repeat(jnp.transpose(b_s).astype(jnp.float32), hidden // B_GROUPS, axis=1)
    row_spec = pl.BlockSpec((tm, D), lambda i: (i, 0))
    out = pl.pallas_call(
        _gmlp_kernel,
        grid=(T // tm,),
        in_specs=[row_spec, _const_spec((1, D)), _const_spec((D, 2 * hidden)),
                  _const_spec((1, hidden)), _const_spec((1, hidden)),
                  _const_spec((B_GROUPS, B_CHUNK, B_CHUNK)), _const_spec((B_CHUNK, hidden)),
                  _const_spec((hidden, D))],
        out_specs=row_spec,
        out_shape=jax.ShapeDtypeStruct((T, D), jnp.float32),
        compiler_params=_params(1),
        name="gmlp",
    )(x.reshape(T, D), g.reshape(1, D), w_in, ln_g.reshape(1, hidden).astype(jnp.float32),
      ln_b.reshape(1, hidden).astype(jnp.float32), w_s, bs_full, w_out)
    return out.reshape(B, S, D)


def _trunk(x, p):
    depth = p["norm_mix_g"].shape[0]
    for i in range(depth):
        kind, j = i % N_MIXERS, i // N_MIXERS
        g_mix = p["norm_mix_g"][i]
        mix = None
        if kind == 0:
            (qkv,) = _qkv_projection(x, g_mix, [p["a_wqkv"][j]], [1])
            o = _band_attention(qkv, p["a_bias"], A_RADIUS, sink=p["a_sink"][j])
            mix = ([o], None, p["a_wo"][j])
        elif kind == 1:
            x = _gmlp(x, g_mix, p["b_win"][j], p["b_ln_g"][j], p["b_ln_b"][j], p["b_ws"][j],
                      p["b_bs"][j], p["b_wo"][j])
        else:
            o_list, lse_list = [], []
            dils = [dil for _, dil in C_GROUPS]
            weights = [p["c_wqkv"][j][:, gi * QKV_COLS:(gi + 1) * QKV_COLS] for gi in range(len(dils))]
            qkvs = _qkv_projection(x, g_mix, weights, dils)
            for gi, (window, dil) in enumerate(C_GROUPS):
                o, lse = _band_attention(qkvs[gi], p["c_bias"][gi], window // (2 * dil), want_lse=True)
                o_list.append(o)
                lse_list.append(lse)
            mix = (o_list, lse_list, p["c_wo"][j])
        final_g = p["final_g"] if i == depth - 1 else None
        x = _ffn(x, p["norm_ffn_g"][i], p["ffn_w1"][i], p["ffn_w2"][i], mix=mix, final_g=final_g)
    return x


def kernel(x_prompt, x_sample, rel_bias, norm_mix_g, norm_ffn_g, final_g, ffn_w1, ffn_w2, a_wqkv,
           a_sink, a_wo, b_win, b_ln_g, b_ln_b, b_ws, b_bs, b_wo, c_wqkv, c_wo):
    bf16 = jnp.bfloat16
    f32 = jnp.float32
    S = x_prompt.shape[1]
    assert x_sample.shape[1] == S
    p = {
        "norm_mix_g": norm_mix_g.astype(f32), "norm_ffn_g": norm_ffn_g.astype(f32),
        "final_g": final_g.astype(f32),
        "ffn_w1": ffn_w1.astype(bf16), "ffn_w2": ffn_w2.astype(bf16),
        "a_wqkv": a_wqkv.astype(bf16), "a_sink": a_sink, "a_wo": a_wo.astype(bf16),
        "b_win": b_win.astype(bf16), "b_ln_g": b_ln_g, "b_ln_b": b_ln_b,
        "b_ws": b_ws.astype(bf16), "b_bs": b_bs, "b_wo": b_wo.astype(bf16),
        "c_wqkv": c_wqkv.astype(bf16), "c_wo": c_wo.astype(bf16),
        "a_bias": _band_bias_table(rel_bias, ATTN_TQ, A_RADIUS, 1, S),
        "c_bias": [_band_bias_table(rel_bias, ATTN_TQ, window // (2 * dil), dil, S // dil)
                   for window, dil in C_GROUPS],
    }
    return _trunk(x_prompt, p), _trunk(x_sample, p)
```

```python
import functools
import math

import numpy as np
import jax
import jax.numpy as jnp
from jax import lax
from jax.experimental import pallas as pl
from jax.experimental.pallas import tpu as pltpu

HEAD_DIM = 64
N_MIXERS = 3
N_HEADS = 16
N_KV_HEADS = 4
A_RADIUS = 128
B_CHUNK = 128
B_GROUPS = 8
C_GROUPS = ((128, 1), (512, 4), (2048, 16))
NUM_BUCKETS = 32
REL_MAX_DISTANCE = 1024
RMS_EPS = 1e-6
LN_EPS = 1e-5

Q_COLS = N_HEADS * HEAD_DIM
KV_COLS = N_KV_HEADS * HEAD_DIM
QKV_COLS = Q_COLS + 2 * KV_COLS
LANES = 128
HEADS_PER_TILE = LANES // HEAD_DIM
Q_PER_KV = N_HEADS // N_KV_HEADS
KVX_COLS = N_KV_HEADS * HEADS_PER_TILE * LANES
MASK_VALUE = -1e30
LOG2_E = math.log2(math.e)
ROWS_PER_WORD = 2
LN_2 = math.log(2.0)

V7X_VMEM_BYTES = 64 * 1024 * 1024
VMEM_LIMIT_BYTES = V7X_VMEM_BYTES - 8 * 1024 * 1024

ATTN_TQ = 128
ATTN_STEP_ROWS = 2048
FFN_TM = 1024
FFN_TM_MERGE = 512
FFN_CHUNK = 1024
PROJ_TM = 1024
PROJ_TM_MULTI = 512
GMLP_TM = 512
GMLP_SUB = 256


def _const_spec(shape):
    zeros = (0,) * len(shape)
    return pl.BlockSpec(shape, lambda *_: zeros, pipeline_mode=pl.Buffered(1))


def _params(n_grid_dims):
    return pltpu.CompilerParams(
        dimension_semantics=("arbitrary",) * n_grid_dims,
        vmem_limit_bytes=VMEM_LIMIT_BYTES,
    )


def _rmsnorm_f32(x, g):
    ms = jnp.mean(x * x, axis=-1, keepdims=True)
    return x * lax.rsqrt(ms + RMS_EPS) * g


def _lse_lane(head):
    return head // HEADS_PER_TILE + HEAD_DIM * (head % HEADS_PER_TILE)


def _proj_kernel(*refs, dils):
    n = len(dils)
    x_ref, g_ref = refs[:2]
    w_refs = refs[2:2 + n]
    out_refs = refs[2 + n:2 + 4 * n]
    hn = _rmsnorm_f32(x_ref[...], g_ref[...])
    tm, d_model = hn.shape
    n_slabs = d_model // LANES
    if any(d > 1 for d in dils):
        slab_ref = refs[2 + 4 * n]
        for j in range(n_slabs):
            slab_ref[j] = hn[:, j * LANES:(j + 1) * LANES]
    low_half = lax.broadcasted_iota(jnp.int32, (tm, LANES), 1) < HEAD_DIM
    for gi, (dil, w_ref) in enumerate(zip(dils, w_refs)):
        q_ref, kx_ref, vx_ref = out_refs[3 * gi:3 * gi + 3]
        n_rows = tm // dil
        if dil == 1:
            h = hn
        else:
            h = jnp.concatenate(
                [jnp.concatenate([slab_ref[j, pl.ds(r, n_rows, stride=dil), :] for j in range(n_slabs)], axis=1)
                 for r in range(dil)], axis=0)
        hb = h.astype(jnp.bfloat16)
        y_kv = jnp.dot(hb, w_ref[:, Q_COLS:], preferred_element_type=jnp.float32)
        y_q = jnp.dot(hb, w_ref[:, :Q_COLS], preferred_element_type=jnp.float32)
        pieces = []
        for o_ref, col0 in ((kx_ref, 0), (vx_ref, KV_COLS)):
            tiles = []
            for t in range(KV_COLS // LANES):
                pair = y_kv[:, col0 + t * LANES:col0 + (t + 1) * LANES]
                only_a = jnp.where(low_half, pair, 0.0)
                only_b = jnp.where(low_half, 0.0, pair)
                tiles += [only_a, pltpu.roll(only_a, HEAD_DIM, 1), pltpu.roll(only_b, HEAD_DIM, 1), only_b]
            pieces.append((o_ref, jnp.concatenate(tiles, axis=1).astype(jnp.bfloat16)))
        pieces.append((q_ref, (y_q * (HEAD_DIM ** -0.5 * LOG2_E)).astype(jnp.bfloat16)))
        for o_ref, val in pieces:
            words = pltpu.bitcast(val, o_ref.dtype)
            n_words = n_rows // ROWS_PER_WORD
            for r in range(dil):
                o_ref[r] = words[r * n_words:(r + 1) * n_words]


def _qkv_projection(x, g, weights, dils):
    B, S, D = x.shape
    n = len(dils)
    tm = PROJ_TM if n == 1 else PROJ_TM_MULTI
    in_specs = [pl.BlockSpec((None, tm, D), lambda b, i: (b, i, 0)), _const_spec((1, D))]
    in_specs += [_const_spec((D, QKV_COLS))] * n
    out_specs, out_shape = [], []
    for dil in dils:
        for cols in (Q_COLS, KVX_COLS, KVX_COLS):
            out_specs.append(pl.BlockSpec((None, dil, tm // dil // ROWS_PER_WORD, cols),
                                          lambda b, i: (b, 0, i, 0)))
            out_shape.append(jax.ShapeDtypeStruct((B, dil, S // dil // ROWS_PER_WORD, cols), jnp.uint32))
    scratch = []
    if any(d > 1 for d in dils):
        scratch.append(pltpu.VMEM((D // LANES, tm, LANES), jnp.float32))
    outs = pl.pallas_call(
        functools.partial(_proj_kernel, dils=tuple(dils)),
        grid=(B, S // tm),
        in_specs=in_specs,
        out_specs=out_specs,
        out_shape=out_shape,
        scratch_shapes=scratch,
        compiler_params=_params(2),
        name="qkv_proj_" + "_".join(f"d{d}" for d in dils),
    )(x, g.reshape(1, D), *weights)
    return [tuple(outs[3 * gi:3 * gi + 3]) for gi in range(n)]


def _rel_bucket(rel):
    half = NUM_BUCKETS // 2
    max_exact = half // 2
    n = np.abs(rel)
    large = max_exact + (np.log(np.maximum(n, 1) / max_exact) / np.log(REL_MAX_DISTANCE / max_exact)
                         * (half - max_exact)).astype(np.int32)
    large = np.minimum(large, half - 1)
    return (rel > 0).astype(np.int32) * half + np.where(n < max_exact, n, large)


def _window_geometry(tq, radius, seq_len):
    width = min(tq + 2 * radius, seq_len)
    if seq_len // tq == 1:
        return width, (0,)
    return width, (0, radius, width - tq)


def _dot_heads(g, v):
    return g * Q_PER_KV + v, g * Q_PER_KV + HEADS_PER_TILE + v


def _band_bias_table(rel_bias, tq, radius, dil, seq_len):
    width, offsets = _window_geometry(tq, radius, seq_len)
    span = max(offsets) + tq - 1
    n_diag = span + width
    diag_rel = np.arange(n_diag) - span
    per_diag = jnp.transpose(rel_bias[_rel_bucket(diag_rel * dil)], (1, 0)).astype(jnp.float32) * LOG2_E
    per_diag = jnp.where(jnp.asarray(np.abs(diag_rel) <= radius)[None], per_diag, MASK_VALUE)
    padded = jnp.pad(per_diag, ((0, 0), (0, 1)))
    tiled = jnp.broadcast_to(padded[:, None, :], (N_HEADS, tq, n_diag + 1)).reshape(N_HEADS, -1)
    skewed = tiled[:, :tq * n_diag].reshape(N_HEADS, tq, n_diag)
    order = [h for g in range(N_KV_HEADS) for v in range(HEADS_PER_TILE) for h in _dot_heads(g, v)]
    skewed = skewed[np.asarray(order)]
    tables = []
    for off in offsets:
        c0 = span - off
        tables.append(skewed[:, :, c0:c0 + width].reshape(N_HEADS * tq, width))
    return jnp.stack(tables, axis=0)


def _attn_kernel(*refs, tq, radius, seq_len, dil, has_sink, want_lse):
    refs = list(refs)
    q_ref, kx_ref, vx_ref, bias_ref = refs[:4]
    pos = 4
    sink_ref = None
    if has_sink:
        sink_ref = refs[pos]
        pos += 1
    o_ref = refs[pos]
    pos += 1
    lse_ref = refs[pos] if want_lse else None

    n_res = q_ref.shape[0]
    blocks = q_ref.shape[1] * ROWS_PER_WORD // tq
    width, offsets = _window_geometry(tq, radius, seq_len)
    n_blk = seq_len // tq
    i = pl.program_id(2)
    bf16 = jnp.bfloat16

    lane = lax.broadcasted_iota(jnp.int32, (tq, LANES), 1)
    low_half2 = lax.broadcasted_iota(jnp.int32, (2 * tq, LANES), 1) < HEAD_DIM
    top_rows = lax.broadcasted_iota(jnp.int32, (2 * tq, 1), 0) < tq
    ones_lo = jnp.where(lax.broadcasted_iota(jnp.int32, (width, LANES), 1) < HEAD_DIM, 1.0, 0.0).astype(bf16)
    ones_hi = jnp.where(lax.broadcasted_iota(jnp.int32, (width, LANES), 1) < HEAD_DIM, 0.0, 1.0).astype(bf16)

    def window_start(j):
        blk = i * blocks + j
        if n_blk == 1:
            return 0
        return jnp.clip(blk * tq - radius, 0, seq_len - width)

    def load_rows(ref, res, start, n, lanes):
        word_start = start // ROWS_PER_WORD
        if not isinstance(word_start, int):
            word_start = pl.multiple_of(word_start, HEAD_DIM // ROWS_PER_WORD)
        words = ref[res, pl.ds(word_start, n // ROWS_PER_WORD), lanes]
        return pltpu.bitcast(words, bf16)

    def scores(res, j, g):
        base = g * Q_PER_KV * HEAD_DIM
        q2 = jnp.concatenate([load_rows(q_ref, res, j * tq, tq, slice(base, base + LANES)),
                              load_rows(q_ref, res, j * tq, tq, slice(base + LANES, base + 2 * LANES))],
                             axis=0)
        k_win = jnp.concatenate(
            [load_rows(kx_ref, res, window_start(j), width, slice(t * LANES, (t + 1) * LANES))
             for t in range(g * HEADS_PER_TILE, (g + 1) * HEADS_PER_TILE)], axis=0)
        return lax.dot_general(q2, k_win, (((1,), (1,)), ((), ())), preferred_element_type=jnp.float32)

    work = [(res, j, g) for res in range(n_res) for j in range(blocks) for g in range(N_KV_HEADS)]
    s_next = scores(*work[0])
    for n, (res, j, g) in enumerate(work):
        base = g * Q_PER_KV * HEAD_DIM
        rows = slice(j * tq, (j + 1) * tq)
        blk = i * blocks + j
        if n_blk == 1:
            variant = 0
        else:
            variant = jnp.where(blk == 0, 0, jnp.where(blk == n_blk - 1, 2, 1))
        if dil > 1:
            token_rows = pl.ds(blk * (tq * dil) + pl.program_id(1) * n_res + res, tq, stride=dil)
        s_all = s_next
        if n + 1 < len(work):
            s_next = scores(*work[n + 1])
        probs = []
        maxes = []
        sink_terms = []
        for v in range(HEADS_PER_TILE):
            t = g * HEADS_PER_TILE + v
            s = s_all[:, v * width:(v + 1) * width] + bias_ref[variant, t * 2 * tq:(t + 1) * 2 * tq, :]
            m = jnp.max(s, axis=-1, keepdims=True)
            if has_sink:
                h_top, h_bot = _dot_heads(g, v)
                sk = jnp.where(top_rows, sink_ref[h_top], sink_ref[h_bot])
                m = jnp.maximum(m, sk)
                sink_terms.append(jnp.exp2(sk - m))
            maxes.append(m)
            probs.append(jnp.exp2(s - m).astype(bf16))
        v_tiles = [load_rows(vx_ref, res, window_start(j), width, slice(t * LANES, (t + 1) * LANES))
                   for t in range(g * HEADS_PER_TILE, (g + 1) * HEADS_PER_TILE)]
        v_rhs = jnp.concatenate([jnp.concatenate([v_tiles[0], ones_lo], axis=1),
                                 jnp.concatenate([v_tiles[1], ones_hi], axis=1)], axis=0)
        acc = jnp.dot(jnp.concatenate(probs, axis=1), v_rhs, preferred_element_type=jnp.float32)
        denom = acc[:, LANES:]
        if has_sink:
            denom = denom + jnp.where(low_half2, sink_terms[0], sink_terms[1])
        o = acc[:, :LANES] / denom
        if dil == 1:
            o_ref[rows, base:base + LANES] = o[:tq].astype(o_ref.dtype)
            o_ref[rows, base + LANES:base + 2 * LANES] = o[tq:].astype(o_ref.dtype)
        else:
            o_ref[g * HEADS_PER_TILE, token_rows, :] = o[:tq]
            o_ref[g * HEADS_PER_TILE + 1, token_rows, :] = o[tq:]
        if want_lse:
            if g == 0:
                lse_acc = jnp.zeros((tq, LANES), jnp.float32)
            lse = jnp.where(low_half2, maxes[0] * LN_2, maxes[1] * LN_2) + jnp.log(denom)
            for half, half_rows in ((0, slice(0, tq)), (1, slice(tq, 2 * tq))):
                tile = g * HEADS_PER_TILE + half
                here = (lane == tile) | (lane == HEAD_DIM + tile)
                lse_acc = jnp.where(here, lse[half_rows], lse_acc)
            if g == N_KV_HEADS - 1:
                if dil == 1:
                    lse_ref[rows, :] = lse_acc
                else:
                    lse_ref[token_rows, :] = lse_acc


def _band_attention(qkv, bias, radius, *, sink=None, want_lse=False):
    q, kx, vx = qkv
    B, dil, L = q.shape[0], q.shape[1], q.shape[2] * ROWS_PER_WORD
    S = L * dil
    tq = ATTN_TQ
    ts = min(L, ATTN_STEP_ROWS)
    n_res = min(dil, ATTN_STEP_ROWS // ts)
    D = Q_COLS

    in_specs = [
        pl.BlockSpec((None, n_res, ts // ROWS_PER_WORD, Q_COLS), lambda b, r, i: (b, r, i, 0)),
        pl.BlockSpec((None, n_res, L // ROWS_PER_WORD, KVX_COLS), lambda b, r, i: (b, r, 0, 0)),
        pl.BlockSpec((None, n_res, L // ROWS_PER_WORD, KVX_COLS), lambda b, r, i: (b, r, 0, 0)),
        _const_spec(bias.shape),
    ]
    args = [q, kx, vx, bias]
    if sink is not None:
        in_specs.append(pl.BlockSpec(memory_space=pltpu.SMEM))
        args.append(sink.astype(jnp.float32) * LOG2_E)
    if dil == 1:
        out_specs = [pl.BlockSpec((None, ts, D), lambda b, r, i: (b, i, 0))]
        out_shape = [jax.ShapeDtypeStruct((B, S, D), jnp.bfloat16)]
        lse_spec = pl.BlockSpec((None, ts, LANES), lambda b, r, i: (b, i, 0))
    else:
        assert want_lse
        out_specs = [pl.BlockSpec((None, D // LANES, S, LANES), lambda b, r, i: (b, 0, 0, 0))]
        out_shape = [jax.ShapeDtypeStruct((B, D // LANES, S, LANES), jnp.float32)]
        lse_spec = pl.BlockSpec((None, S, LANES), lambda b, r, i: (b, 0, 0))
    if want_lse:
        out_specs.append(lse_spec)
        out_shape.append(jax.ShapeDtypeStruct((B, S, LANES), jnp.float32))

    kernel = functools.partial(_attn_kernel, tq=tq, radius=radius, seq_len=L, dil=dil,
                               has_sink=sink is not None, want_lse=want_lse)
    out = pl.pallas_call(
        kernel,
        grid=(B, dil // n_res, L // ts),
        in_specs=in_specs,
        out_specs=out_specs,
        out_shape=out_shape,
        compiler_params=_params(3),
        name=f"band_attn_r{radius}_d{dil}",
    )(*args)
    if want_lse:
        return out[0], out[1]
    return out[0]


def _head_expand_matrix():
    e = np.zeros((LANES, Q_COLS), np.float32)
    for h in range(N_HEADS):
        e[_lse_lane(h), h * HEAD_DIM:(h + 1) * HEAD_DIM] = 1.0
    return jnp.asarray(np.concatenate([e, e], axis=0), jnp.bfloat16)


def _read_rows_f32(o_ref):
    if len(o_ref.shape) == 2:
        return o_ref[...].astype(jnp.float32)
    return jnp.concatenate([o_ref[j] for j in range(o_ref.shape[0])], axis=1)


def _mlp_rows(x, g_ref, w1_ref, w2_ref):
    h = _rmsnorm_f32(x, g_ref[...]).astype(jnp.bfloat16)
    acc = x
    d_ff = w1_ref.shape[1]
    for c in range(d_ff // FFN_CHUNK):
        cols = slice(c * FFN_CHUNK, (c + 1) * FFN_CHUNK)
        a = jnp.dot(h, w1_ref[:, cols], preferred_element_type=jnp.float32)
        a = jnp.square(jnp.maximum(a, 0.0)).astype(jnp.bfloat16)
        acc = acc + jnp.dot(a, w2_ref[cols, :], preferred_element_type=jnp.float32)
    return acc


def _ffn_kernel(*refs, n_mix, final):
    refs = list(refs)
    x_ref = refs[0]
    pos = 1
    x = x_ref[...]
    if n_mix >= 1:
        o_refs = refs[pos:pos + n_mix]
        pos += n_mix
        if n_mix > 1:
            lse_refs = refs[pos:pos + n_mix]
            expand_ref = refs[pos + n_mix]
            pos += n_mix + 1
        wo_ref = refs[pos]
        pos += 1
        if n_mix == 1:
            o = o_refs[0][...]
        else:
            lses = [r[...] for r in lse_refs]
            top = functools.reduce(jnp.maximum, lses)
            es = [jnp.exp(l - top) for l in lses]
            inv_tot = 1.0 / functools.reduce(lambda a, b: a + b, es)
            o = None
            for e, o_ref in zip(es, o_refs):
                w = e * inv_tot
                w_hi = w.astype(jnp.bfloat16)
                w_lo = (w - w_hi.astype(jnp.float32)).astype(jnp.bfloat16)
                w_full = jnp.dot(jnp.concatenate([w_hi, w_lo], axis=1), expand_ref[...],
                                 preferred_element_type=jnp.float32)
                term = w_full * _read_rows_f32(o_ref)
                o = term if o is None else o + term
            o = o.astype(jnp.bfloat16)
        x = x + jnp.dot(o, wo_ref[...], preferred_element_type=jnp.float32)
    g_ref, w1_ref, w2_ref = refs[pos:pos + 3]
    pos += 3
    if final:
        fg_ref = refs[pos]
        pos += 1
    out_ref = refs[pos]

    acc = _mlp_rows(x, g_ref, w1_ref, w2_ref)
    if final:
        acc = _rmsnorm_f32(acc, fg_ref[...])
    out_ref[...] = acc


def _ffn(x, g, w1, w2, *, mix=None, final_g=None):
    B, S, D = x.shape
    d_ff = w1.shape[1]
    tm = FFN_TM_MERGE if mix is not None and len(mix[0]) > 1 else FFN_TM
    row_spec = lambda c: pl.BlockSpec((None, tm, c), lambda b, i: (b, i, 0))
    slab_spec = lambda c: pl.BlockSpec((None, c // LANES, tm, LANES), lambda b, i: (b, 0, i, 0))
    in_specs = [row_spec(D)]
    args = [x]
    n_mix = 0
    if mix is not None:
        o_list, lse_list, w_o = mix
        n_mix = len(o_list)
        in_specs += [row_spec(D) if o.ndim == 3 else slab_spec(D) for o in o_list]
        args += o_list
        if n_mix > 1:
            in_specs += [row_spec(LANES)] * n_mix + [_const_spec((2 * LANES, Q_COLS))]
            args += lse_list + [_head_expand_matrix()]
        in_specs.append(_const_spec((D, D)))
        args.append(w_o)
    in_specs += [_const_spec((1, D)), _const_spec((D, d_ff)), _const_spec((d_ff, D))]
    args += [g.reshape(1, D), w1, w2]
    if final_g is not None:
        in_specs.append(_const_spec((1, D)))
        args.append(final_g.reshape(1, D))
    out = pl.pallas_call(
        functools.partial(_ffn_kernel, n_mix=n_mix, final=final_g is not None),
        grid=(B, S // tm),
        in_specs=in_specs,
        out_specs=row_spec(D),
        out_shape=jax.ShapeDtypeStruct((B, S, D), jnp.float32),
        compiler_params=_params(2),
        name=f"ffn_mix{n_mix}" + ("_final" if final_g is not None else ""),
    )(*args)
    return out


def _gmlp_ffn_kernel(x_ref, g_ref, win_ref, lng_ref, lnb_ref, ws_ref, bs_ref, wout_ref,
                     g2_ref, w1_ref, w2_ref, o_ref):
    tm = x_ref.shape[0]
    hidden = wout_ref.shape[0]
    gcols = hidden // B_GROUPS
    subs = [slice(r0, r0 + GMLP_SUB) for r0 in range(0, tm, GMLP_SUB)]
    zs = []
    for rows in subs:
        h = _rmsnorm_f32(x_ref[rows, :], g_ref[...]).astype(jnp.bfloat16)
        zs.append(jnp.dot(h, win_ref[...], preferred_element_type=jnp.float32))
    for rows, z in zip(subs, zs):
        z = 0.5 * z * (1.0 + lax.erf(z * (2.0 ** -0.5)))
        u = z[:, :hidden]
        v = z[:, hidden:]
        mu = jnp.mean(v, axis=-1, keepdims=True)
        vc = v - mu
        var = jnp.mean(vc * vc, axis=-1, keepdims=True)
        vn = (vc * lax.rsqrt(var + LN_EPS) * lng_ref[...] + lnb_ref[...]).astype(jnp.bfloat16)
        gated = []
        for r0 in range(0, GMLP_SUB, B_CHUNK):
            parts = []
            for grp in range(B_GROUPS):
                parts.append(jnp.dot(ws_ref[grp], vn[r0:r0 + B_CHUNK, grp * gcols:(grp + 1) * gcols],
                                     preferred_element_type=jnp.float32))
            mixed = jnp.concatenate(parts, axis=-1) + bs_ref[...]
            gated.append((u[r0:r0 + B_CHUNK, :] * mixed).astype(jnp.bfloat16))
        t = jnp.concatenate(gated, axis=0)
        x_mid = x_ref[rows, :] + jnp.dot(t, wout_ref[...], preferred_element_type=jnp.float32)
        o_ref[rows, :] = _mlp_rows(x_mid, g2_ref, w1_ref, w2_ref)


def _gmlp_ffn(x, g, w_in, ln_g, ln_b, w_s, b_s, w_out, g_ffn, w1, w2):
    B, S, D = x.shape
    hidden = w_out.shape[0]
    d_ff = w1.shape[1]
    T = B * S
    tm = GMLP_TM
    bs_full = jnp.repeat(jnp.transpose(b_s).astype(jnp.float32), hidden // B_GROUPS, axis=1)
    row_spec = pl.BlockSpec((tm, D), lambda i: (i, 0))
    out = pl.pallas_call(
        _gmlp_ffn_kernel,
        grid=(T // tm,),
        in_specs=[row_spec, _const_spec((1, D)), _const_spec((D, 2 * hidden)),
                  _const_spec((1, hidden)), _const_spec((1, hidden)),
                  _const_spec((B_GROUPS, B_CHUNK, B_CHUNK)), _const_spec((B_CHUNK, hidden)),
                  _const_spec((hidden, D)),
                  _const_spec((1, D)), _const_spec((D, d_ff)), _const_spec((d_ff, D))],
        out_specs=row_spec,
        out_shape=jax.ShapeDtypeStruct((T, D), jnp.float32),
        compiler_params=_params(1),
        name="gmlp_ffn",
    )(x.reshape(T, D), g.reshape(1, D), w_in, ln_g.reshape(1, hidden).astype(jnp.float32),
      ln_b.reshape(1, hidden).astype(jnp.float32), w_s, bs_full, w_out,
      g_ffn.reshape(1, D), w1, w2)
    return out.reshape(B, S, D)


def _trunk(x, p):
    depth = p["norm_mix_g"].shape[0]
    for i in range(depth):
        kind, j = i % N_MIXERS, i // N_MIXERS
        g_mix = p["norm_mix_g"][i]
        mix = None
        if kind == 0:
            (qkv,) = _qkv_projection(x, g_mix, [p["a_wqkv"][j]], [1])
            o = _band_attention(qkv, p["a_bias"], A_RADIUS, sink=p["a_sink"][j])
            mix = ([o], None, p["a_wo"][j])
        elif kind == 1:
            assert i < depth - 1, "the fused gMLP + MLP kernel has no final-norm variant"
            x = _gmlp_ffn(x, g_mix, p["b_win"][j], p["b_ln_g"][j], p["b_ln_b"][j], p["b_ws"][j],
                          p["b_bs"][j], p["b_wo"][j], p["norm_ffn_g"][i], p["ffn_w1"][i], p["ffn_w2"][i])
            continue
        else:
            o_list, lse_list = [], []
            dils = [dil for _, dil in C_GROUPS]
            weights = [p["c_wqkv"][j][:, gi * QKV_COLS:(gi + 1) * QKV_COLS] for gi in range(len(dils))]
            qkvs = _qkv_projection(x, g_mix, weights, dils)
            for gi, (window, dil) in enumerate(C_GROUPS):
                o, lse = _band_attention(qkvs[gi], p["c_bias"][gi], window // (2 * dil), want_lse=True)
                o_list.append(o)
                lse_list.append(lse)
            mix = (o_list, lse_list, p["c_wo"][j])
        final_g = p["final_g"] if i == depth - 1 else None
        x = _ffn(x, p["norm_ffn_g"][i], p["ffn_w1"][i], p["ffn_w2"][i], mix=mix, final_g=final_g)
    return x


def kernel(x_prompt, x_sample, rel_bias, norm_mix_g, norm_ffn_g, final_g, ffn_w1, ffn_w2, a_wqkv,
           a_sink, a_wo, b_win, b_ln_g, b_ln_b, b_ws, b_bs, b_wo, c_wqkv, c_wo):
    bf16 = jnp.bfloat16
    f32 = jnp.float32
    S = x_prompt.shape[1]
    assert x_sample.shape[1] == S
    p = {
        "norm_mix_g": norm_mix_g.astype(f32), "norm_ffn_g": norm_ffn_g.astype(f32),
        "final_g": final_g.astype(f32),
        "ffn_w1": ffn_w1.astype(bf16), "ffn_w2": ffn_w2.astype(bf16),
        "a_wqkv": a_wqkv.astype(bf16), "a_sink": a_sink, "a_wo": a_wo.astype(bf16),
        "b_win": b_win.astype(bf16), "b_ln_g": b_ln_g, "b_ln_b": b_ln_b,
        "b_ws": b_ws.astype(bf16), "b_bs": b_bs, "b_wo": b_wo.astype(bf16),
        "c_wqkv": c_wqkv.astype(bf16), "c_wo": c_wo.astype(bf16),
        "a_bias": _band_bias_table(rel_bias, ATTN_TQ, A_RADIUS, 1, S),
        "c_bias": [_band_bias_table(rel_bias, ATTN_TQ, window // (2 * dil), dil, S // dil)
                   for window, dil in C_GROUPS],
    }
    return _trunk(x_prompt, p), _trunk(x_sample, p)
```

```python
import functools
import math

import numpy as np
import jax
import jax.numpy as jnp
from jax import lax
from jax.experimental import pallas as pl
from jax.experimental.pallas import tpu as pltpu

HEAD_DIM = 64
N_MIXERS = 3
N_HEADS = 16
N_KV_HEADS = 4
A_RADIUS = 128
B_CHUNK = 128
B_GROUPS = 8
C_GROUPS = ((128, 1), (512, 4), (2048, 16))
NUM_BUCKETS = 32
REL_MAX_DISTANCE = 1024
RMS_EPS = 1e-6
LN_EPS = 1e-5

Q_COLS = N_HEADS * HEAD_DIM
KV_COLS = N_KV_HEADS * HEAD_DIM
QKV_COLS = Q_COLS + 2 * KV_COLS
LANES = 128
HEADS_PER_TILE = LANES // HEAD_DIM
Q_PER_KV = N_HEADS // N_KV_HEADS
KVX_COLS = N_KV_HEADS * HEADS_PER_TILE * LANES
MASK_VALUE = -1e30
LOG2_E = math.log2(math.e)
ROWS_PER_WORD = 2
LN_2 = math.log(2.0)

V7X_VMEM_BYTES = 64 * 1024 * 1024
VMEM_LIMIT_BYTES = V7X_VMEM_BYTES - 8 * 1024 * 1024

ATTN_TQ = 128
ATTN_STEP_ROWS = 2048
FFN_TM = 1024
FFN_TM_MERGE = 512
FFN_CHUNK = 1024
PROJ_TM = 1024
PROJ_TM_MULTI = 512
GMLP_TM = 1024
GMLP_SUB = 256


def _const_spec(shape):
    zeros = (0,) * len(shape)
    return pl.BlockSpec(shape, lambda *_: zeros, pipeline_mode=pl.Buffered(1))


def _params(n_grid_dims):
    return pltpu.CompilerParams(
        dimension_semantics=("arbitrary",) * n_grid_dims,
        vmem_limit_bytes=VMEM_LIMIT_BYTES,
    )


def _rmsnorm_f32(x, g):
    ms = jnp.mean(x * x, axis=-1, keepdims=True)
    return x * lax.rsqrt(ms + RMS_EPS) * g


def _lse_lane(head):
    return head // HEADS_PER_TILE + HEAD_DIM * (head % HEADS_PER_TILE)


def _proj_kernel(*refs, dils):
    n = len(dils)
    x_ref, g_ref = refs[:2]
    w_refs = refs[2:2 + n]
    out_refs = refs[2 + n:2 + 4 * n]
    hn = _rmsnorm_f32(x_ref[...], g_ref[...])
    tm, d_model = hn.shape
    n_slabs = d_model // LANES
    if any(d > 1 for d in dils):
        slab_ref = refs[2 + 4 * n]
        for j in range(n_slabs):
            slab_ref[j] = hn[:, j * LANES:(j + 1) * LANES]
    low_half = lax.broadcasted_iota(jnp.int32, (tm, LANES), 1) < HEAD_DIM
    for gi, (dil, w_ref) in enumerate(zip(dils, w_refs)):
        q_ref, kx_ref, vx_ref = out_refs[3 * gi:3 * gi + 3]
        n_rows = tm // dil
        if dil == 1:
            h = hn
        else:
            h = jnp.concatenate(
                [jnp.concatenate([slab_ref[j, pl.ds(r, n_rows, stride=dil), :] for j in range(n_slabs)], axis=1)
                 for r in range(dil)], axis=0)
        hb = h.astype(jnp.bfloat16)
        y_kv = jnp.dot(hb, w_ref[:, Q_COLS:], preferred_element_type=jnp.float32)
        y_q = jnp.dot(hb, w_ref[:, :Q_COLS], preferred_element_type=jnp.float32)
        pieces = []
        for o_ref, col0 in ((kx_ref, 0), (vx_ref, KV_COLS)):
            tiles = []
            for t in range(KV_COLS // LANES):
                pair = y_kv[:, col0 + t * LANES:col0 + (t + 1) * LANES]
                only_a = jnp.where(low_half, pair, 0.0)
                only_b = jnp.where(low_half, 0.0, pair)
                tiles += [only_a, pltpu.roll(only_a, HEAD_DIM, 1), pltpu.roll(only_b, HEAD_DIM, 1), only_b]
            pieces.append((o_ref, jnp.concatenate(tiles, axis=1).astype(jnp.bfloat16)))
        pieces.append((q_ref, (y_q * (HEAD_DIM ** -0.5 * LOG2_E)).astype(jnp.bfloat16)))
        for o_ref, val in pieces:
            words = pltpu.bitcast(val, o_ref.dtype)
            n_words = n_rows // ROWS_PER_WORD
            for r in range(dil):
                o_ref[r] = words[r * n_words:(r + 1) * n_words]


def _qkv_projection(x, g, w, dils):
    B, S, D = x.shape
    n = len(dils)
    tm = PROJ_TM if n == 1 else PROJ_TM_MULTI
    in_specs = [pl.BlockSpec((None, tm, D), lambda b, i: (b, i, 0)), _const_spec((1, D))]
    in_specs += [pl.BlockSpec((D, QKV_COLS), lambda b, i, gi=gi: (0, gi), pipeline_mode=pl.Buffered(1))
                 for gi in range(n)]
    out_specs, out_shape = [], []
    for dil in dils:
        for cols in (Q_COLS, KVX_COLS, KVX_COLS):
            out_specs.append(pl.BlockSpec((None, dil, tm // dil // ROWS_PER_WORD, cols),
                                          lambda b, i: (b, 0, i, 0)))
            out_shape.append(jax.ShapeDtypeStruct((B, dil, S // dil // ROWS_PER_WORD, cols), jnp.uint32))
    scratch = []
    if any(d > 1 for d in dils):
        scratch.append(pltpu.VMEM((D // LANES, tm, LANES), jnp.float32))
    outs = pl.pallas_call(
        functools.partial(_proj_kernel, dils=tuple(dils)),
        grid=(B, S // tm),
        in_specs=in_specs,
        out_specs=out_specs,
        out_shape=out_shape,
        scratch_shapes=scratch,
        compiler_params=_params(2),
        name="qkv_proj_" + "_".join(f"d{d}" for d in dils),
    )(x, g.reshape(1, D), *([w] * n))
    return [tuple(outs[3 * gi:3 * gi + 3]) for gi in range(n)]


def _rel_bucket(rel):
    half = NUM_BUCKETS // 2
    max_exact = half // 2
    n = np.abs(rel)
    large = max_exact + (np.log(np.maximum(n, 1) / max_exact) / np.log(REL_MAX_DISTANCE / max_exact)
                         * (half - max_exact)).astype(np.int32)
    large = np.minimum(large, half - 1)
    return (rel > 0).astype(np.int32) * half + np.where(n < max_exact, n, large)


def _window_geometry(tq, radius, seq_len):
    width = min(tq + 2 * radius, seq_len)
    if seq_len // tq == 1:
        return width, (0,)
    return width, (0, radius, width - tq)


def _dot_heads(g, v):
    return g * Q_PER_KV + v, g * Q_PER_KV + HEADS_PER_TILE + v


def _band_bias_table(rel_bias, tq, radius, dil, seq_len):
    width, offsets = _window_geometry(tq, radius, seq_len)
    span = max(offsets) + tq - 1
    n_diag = span + width
    diag_rel = np.arange(n_diag) - span
    per_diag = jnp.transpose(rel_bias[_rel_bucket(diag_rel * dil)], (1, 0)).astype(jnp.float32) * LOG2_E
    per_diag = jnp.where(jnp.asarray(np.abs(diag_rel) <= radius)[None], per_diag, MASK_VALUE)
    padded = jnp.pad(per_diag, ((0, 0), (0, 1)))
    tiled = jnp.broadcast_to(padded[:, None, :], (N_HEADS, tq, n_diag + 1)).reshape(N_HEADS, -1)
    skewed = tiled[:, :tq * n_diag].reshape(N_HEADS, tq, n_diag)
    order = [h for g in range(N_KV_HEADS) for v in range(HEADS_PER_TILE) for h in _dot_heads(g, v)]
    skewed = skewed[np.asarray(order)]
    tables = []
    for off in offsets:
        c0 = span - off
        tables.append(skewed[:, :, c0:c0 + width].reshape(N_HEADS * tq, width))
    return jnp.stack(tables, axis=0)


def _attn_kernel(*refs, tq, radius, seq_len, dil, has_sink, want_lse):
    refs = list(refs)
    q_ref, kx_ref, vx_ref, bias_ref = refs[:4]
    pos = 4
    sink_ref = None
    if has_sink:
        sink_ref = refs[pos]
        pos += 1
    o_ref = refs[pos]
    pos += 1
    lse_ref = refs[pos] if want_lse else None

    n_res = q_ref.shape[0]
    blocks = q_ref.shape[1] * ROWS_PER_WORD // tq
    width, offsets = _window_geometry(tq, radius, seq_len)
    n_blk = seq_len // tq
    i = pl.program_id(2)
    bf16 = jnp.bfloat16

    lane = lax.broadcasted_iota(jnp.int32, (tq, LANES), 1)
    low_half2 = lax.broadcasted_iota(jnp.int32, (2 * tq, LANES), 1) < HEAD_DIM
    top_rows = lax.broadcasted_iota(jnp.int32, (2 * tq, 1), 0) < tq
    ones_lo = jnp.where(lax.broadcasted_iota(jnp.int32, (width, LANES), 1) < HEAD_DIM, 1.0, 0.0).astype(bf16)
    ones_hi = jnp.where(lax.broadcasted_iota(jnp.int32, (width, LANES), 1) < HEAD_DIM, 0.0, 1.0).astype(bf16)

    def window_start(j):
        blk = i * blocks + j
        if n_blk == 1:
            return 0
        return jnp.clip(blk * tq - radius, 0, seq_len - width)

    def load_rows(ref, res, start, n, lanes):
        word_start = start // ROWS_PER_WORD
        if not isinstance(word_start, int):
            word_start = pl.multiple_of(word_start, HEAD_DIM // ROWS_PER_WORD)
        words = ref[res, pl.ds(word_start, n // ROWS_PER_WORD), lanes]
        return pltpu.bitcast(words, bf16)

    def scores(res, j, g):
        base = g * Q_PER_KV * HEAD_DIM
        q2 = jnp.concatenate([load_rows(q_ref, res, j * tq, tq, slice(base, base + LANES)),
                              load_rows(q_ref, res, j * tq, tq, slice(base + LANES, base + 2 * LANES))],
                             axis=0)
        k_win = jnp.concatenate(
            [load_rows(kx_ref, res, window_start(j), width, slice(t * LANES, (t + 1) * LANES))
             for t in range(g * HEADS_PER_TILE, (g + 1) * HEADS_PER_TILE)], axis=0)
        return lax.dot_general(q2, k_win, (((1,), (1,)), ((), ())), preferred_element_type=jnp.float32)

    work = [(res, j, g) for res in range(n_res) for j in range(blocks) for g in range(N_KV_HEADS)]
    s_next = scores(*work[0])
    for n, (res, j, g) in enumerate(work):
        base = g * Q_PER_KV * HEAD_DIM
        rows = slice(j * tq, (j + 1) * tq)
        blk = i * blocks + j
        if n_blk == 1:
            variant = 0
        else:
            variant = jnp.where(blk == 0, 0, jnp.where(blk == n_blk - 1, 2, 1))
        if dil > 1:
            token_rows = pl.ds(blk * (tq * dil) + pl.program_id(1) * n_res + res, tq, stride=dil)
        s_all = s_next
        if n + 1 < len(work):
            s_next = scores(*work[n + 1])
        probs = []
        maxes = []
        sink_terms = []
        for v in range(HEADS_PER_TILE):
            t = g * HEADS_PER_TILE + v
            s = s_all[:, v * width:(v + 1) * width] + bias_ref[variant, t * 2 * tq:(t + 1) * 2 * tq, :]
            m = jnp.max(s, axis=-1, keepdims=True)
            if has_sink:
                h_top, h_bot = _dot_heads(g, v)
                sk = jnp.where(top_rows, sink_ref[h_top], sink_ref[h_bot])
                m = jnp.maximum(m, sk)
                sink_terms.append(jnp.exp2(sk - m))
            maxes.append(m)
            probs.append(jnp.exp2(s - m).astype(bf16))
        v_tiles = [load_rows(vx_ref, res, window_start(j), width, slice(t * LANES, (t + 1) * LANES))
                   for t in range(g * HEADS_PER_TILE, (g + 1) * HEADS_PER_TILE)]
        v_rhs = jnp.concatenate([jnp.concatenate([v_tiles[0], ones_lo], axis=1),
                                 jnp.concatenate([v_tiles[1], ones_hi], axis=1)], axis=0)
        acc = jnp.dot(jnp.concatenate(probs, axis=1), v_rhs, preferred_element_type=jnp.float32)
        denom = acc[:, LANES:]
        if has_sink:
            denom = denom + jnp.where(low_half2, sink_terms[0], sink_terms[1])
        o = acc[:, :LANES] / denom
        if dil == 1:
            o_ref[rows, base:base + LANES] = o[:tq].astype(o_ref.dtype)
            o_ref[rows, base + LANES:base + 2 * LANES] = o[tq:].astype(o_ref.dtype)
        else:
            o_ref[g * HEADS_PER_TILE, token_rows, :] = o[:tq]
            o_ref[g * HEADS_PER_TILE + 1, token_rows, :] = o[tq:]
        if want_lse:
            if g == 0:
                lse_acc = jnp.zeros((tq, LANES), jnp.float32)
            lse = jnp.where(low_half2, maxes[0] * LN_2, maxes[1] * LN_2) + jnp.log(denom)
            for half, half_rows in ((0, slice(0, tq)), (1, slice(tq, 2 * tq))):
                tile = g * HEADS_PER_TILE + half
                here = (lane == tile) | (lane == HEAD_DIM + tile)
                lse_acc = jnp.where(here, lse[half_rows], lse_acc)
            if g == N_KV_HEADS - 1:
                if dil == 1:
                    lse_ref[rows, :] = lse_acc
                else:
                    lse_ref[token_rows, :] = lse_acc


def _band_attention(qkv, bias, radius, *, sink=None, want_lse=False):
    q, kx, vx = qkv
    B, dil, L = q.shape[0], q.shape[1], q.shape[2] * ROWS_PER_WORD
    S = L * dil
    tq = ATTN_TQ
    ts = min(L, ATTN_STEP_ROWS)
    n_res = min(dil, ATTN_STEP_ROWS // ts)
    D = Q_COLS

    in_specs = [
        pl.BlockSpec((None, n_res, ts // ROWS_PER_WORD, Q_COLS), lambda b, r, i: (b, r, i, 0)),
        pl.BlockSpec((None, n_res, L // ROWS_PER_WORD, KVX_COLS), lambda b, r, i: (b, r, 0, 0)),
        pl.BlockSpec((None, n_res, L // ROWS_PER_WORD, KVX_COLS), lambda b, r, i: (b, r, 0, 0)),
        _const_spec(bias.shape),
    ]
    args = [q, kx, vx, bias]
    if sink is not None:
        in_specs.append(pl.BlockSpec(memory_space=pltpu.SMEM))
        args.append(sink.astype(jnp.float32) * LOG2_E)
    if dil == 1:
        out_specs = [pl.BlockSpec((None, ts, D), lambda b, r, i: (b, i, 0))]
        out_shape = [jax.ShapeDtypeStruct((B, S, D), jnp.bfloat16)]
        lse_spec = pl.BlockSpec((None, ts, LANES), lambda b, r, i: (b, i, 0))
    else:
        assert want_lse
        out_specs = [pl.BlockSpec((None, D // LANES, S, LANES), lambda b, r, i: (b, 0, 0, 0))]
        out_shape = [jax.ShapeDtypeStruct((B, D // LANES, S, LANES), jnp.float32)]
        lse_spec = pl.BlockSpec((None, S, LANES), lambda b, r, i: (b, 0, 0))
    if want_lse:
        out_specs.append(lse_spec)
        out_shape.append(jax.ShapeDtypeStruct((B, S, LANES), jnp.float32))

    kernel = functools.partial(_attn_kernel, tq=tq, radius=radius, seq_len=L, dil=dil,
                               has_sink=sink is not None, want_lse=want_lse)
    out = pl.pallas_call(
        kernel,
        grid=(B, dil // n_res, L // ts),
        in_specs=in_specs,
        out_specs=out_specs,
        out_shape=out_shape,
        compiler_params=_params(3),
        name=f"band_attn_r{radius}_d{dil}",
    )(*args)
    if want_lse:
        return out[0], out[1]
    return out[0]


def _head_expand_matrix():
    e = np.zeros((LANES, Q_COLS), np.float32)
    for h in range(N_HEADS):
        e[_lse_lane(h), h * HEAD_DIM:(h + 1) * HEAD_DIM] = 1.0
    return jnp.asarray(np.concatenate([e, e], axis=0), jnp.bfloat16)


def _read_rows_f32(o_ref):
    if len(o_ref.shape) == 2:
        return o_ref[...].astype(jnp.float32)
    return jnp.concatenate([o_ref[j] for j in range(o_ref.shape[0])], axis=1)


def _ffn_kernel(*refs, n_mix, final):
    refs = list(refs)
    x_ref = refs[0]
    pos = 1
    x = x_ref[...]
    if n_mix >= 1:
        o_refs = refs[pos:pos + n_mix]
        pos += n_mix
        if n_mix > 1:
            lse_refs = refs[pos:pos + n_mix]
            expand_ref = refs[pos + n_mix]
            pos += n_mix + 1
        wo_ref = refs[pos]
        pos += 1
        if n_mix == 1:
            o = o_refs[0][...]
        else:
            lses = [r[...] for r in lse_refs]
            top = functools.reduce(jnp.maximum, lses)
            es = [jnp.exp(l - top) for l in lses]
            inv_tot = 1.0 / functools.reduce(lambda a, b: a + b, es)
            o = None
            w_rest = None
            for gi, (e, o_ref) in enumerate(zip(es, o_refs)):
                if gi < n_mix - 1:
                    w = e * inv_tot
                    w_hi = w.astype(jnp.bfloat16)
                    w_lo = (w - w_hi.astype(jnp.float32)).astype(jnp.bfloat16)
                    w_full = jnp.dot(jnp.concatenate([w_hi, w_lo], axis=1), expand_ref[...],
                                     preferred_element_type=jnp.float32)
                    w_rest = 1.0 - w_full if w_rest is None else w_rest - w_full
                else:
                    w_full = w_rest
                term = w_full * _read_rows_f32(o_ref)
                o = term if o is None else o + term
            o = o.astype(jnp.bfloat16)
        x = x + jnp.dot(o, wo_ref[...], preferred_element_type=jnp.float32)
    g_ref, w1_ref, w2_ref = refs[pos:pos + 3]
    pos += 3
    if final:
        fg_ref = refs[pos]
        pos += 1
    out_ref = refs[pos]

    h = _rmsnorm_f32(x, g_ref[...]).astype(jnp.bfloat16)
    acc = x
    d_ff = w1_ref.shape[1]
    for c in range(d_ff // FFN_CHUNK):
        cols = slice(c * FFN_CHUNK, (c + 1) * FFN_CHUNK)
        a = jnp.dot(h, w1_ref[:, cols], preferred_element_type=jnp.float32)
        a = jnp.square(jnp.maximum(a, 0.0)).astype(jnp.bfloat16)
        acc = acc + jnp.dot(a, w2_ref[cols, :], preferred_element_type=jnp.float32)
    if final:
        acc = _rmsnorm_f32(acc, fg_ref[...])
    out_ref[...] = acc


def _ffn(x, g, w1, w2, *, mix=None, final_g=None):
    B, S, D = x.shape
    d_ff = w1.shape[1]
    tm = FFN_TM_MERGE if mix is not None and len(mix[0]) > 1 else FFN_TM
    row_spec = lambda c: pl.BlockSpec((None, tm, c), lambda b, i: (b, i, 0))
    slab_spec = lambda c: pl.BlockSpec((None, c // LANES, tm, LANES), lambda b, i: (b, 0, i, 0))
    in_specs = [row_spec(D)]
    args = [x]
    n_mix = 0
    if mix is not None:
        o_list, lse_list, w_o = mix
        n_mix = len(o_list)
        in_specs += [row_spec(D) if o.ndim == 3 else slab_spec(D) for o in o_list]
        args += o_list
        if n_mix > 1:
            in_specs += [row_spec(LANES)] * n_mix + [_const_spec((2 * LANES, Q_COLS))]
            args += lse_list + [_head_expand_matrix()]
        in_specs.append(_const_spec((D, D)))
        args.append(w_o)
    in_specs += [_const_spec((1, D)), _const_spec((D, d_ff)), _const_spec((d_ff, D))]
    args += [g.reshape(1, D), w1, w2]
    if final_g is not None:
        in_specs.append(_const_spec((1, D)))
        args.append(final_g.reshape(1, D))
    out = pl.pallas_call(
        functools.partial(_ffn_kernel, n_mix=n_mix, final=final_g is not None),
        grid=(B, S // tm),
        in_specs=in_specs,
        out_specs=row_spec(D),
        out_shape=jax.ShapeDtypeStruct((B, S, D), jnp.float32),
        compiler_params=_params(2),
        name=f"ffn_mix{n_mix}" + ("_final" if final_g is not None else ""),
    )(*args)
    return out


def _gmlp_kernel(x_ref, g_ref, win_ref, lng_ref, lnb_ref, ws_ref, bs_ref, wout_ref, o_ref):
    tm = x_ref.shape[0]
    hidden = wout_ref.shape[0]
    gcols = hidden // B_GROUPS
    subs = [slice(r0, r0 + GMLP_SUB) for r0 in range(0, tm, GMLP_SUB)]
    zs = []
    for rows in subs:
        h = _rmsnorm_f32(x_ref[rows, :], g_ref[...]).astype(jnp.bfloat16)
        zs.append(jnp.dot(h, win_ref[...], preferred_element_type=jnp.float32))
    for rows, z in zip(subs, zs):
        z = 0.5 * z * (1.0 + lax.erf(z * (2.0 ** -0.5)))
        u = z[:, :hidden]
        v = z[:, hidden:]
        mu = jnp.mean(v, axis=-1, keepdims=True)
        vc = v - mu
        var = jnp.mean(vc * vc, axis=-1, keepdims=True)
        vn = (vc * lax.rsqrt(var + LN_EPS) * lng_ref[...] + lnb_ref[...]).astype(jnp.bfloat16)
        gated = []
        for r0 in range(0, GMLP_SUB, B_CHUNK):
            parts = []
            for grp in range(B_GROUPS):
                parts.append(jnp.dot(ws_ref[grp], vn[r0:r0 + B_CHUNK, grp * gcols:(grp + 1) * gcols],
                                     preferred_element_type=jnp.float32))
            mixed = jnp.concatenate(parts, axis=-1) + bs_ref[...]
            gated.append((u[r0:r0 + B_CHUNK, :] * mixed).astype(jnp.bfloat16))
        t = jnp.concatenate(gated, axis=0)
        o_ref[rows, :] = x_ref[rows, :] + jnp.dot(t, wout_ref[...], preferred_element_type=jnp.float32)


def _gmlp(x, g, w_in, ln_g, ln_b, w_s, b_s, w_out):
    B, S, D = x.shape
    hidden = w_out.shape[0]
    T = B * S
    tm = GMLP_TM
    bs_full = jnp.repeat(jnp.transpose(b_s).astype(jnp.float32), hidden // B_GROUPS, axis=1)
    row_spec = pl.BlockSpec((tm, D), lambda i: (i, 0))
    out = pl.pallas_call(
        _gmlp_kernel,
        grid=(T // tm,),
        in_specs=[row_spec, _const_spec((1, D)), _const_spec((D, 2 * hidden)),
                  _const_spec((1, hidden)), _const_spec((1, hidden)),
                  _const_spec((B_GROUPS, B_CHUNK, B_CHUNK)), _const_spec((B_CHUNK, hidden)),
                  _const_spec((hidden, D))],
        out_specs=row_spec,
        out_shape=jax.ShapeDtypeStruct((T, D), jnp.float32),
        compiler_params=_params(1),
        name="gmlp",
    )(x.reshape(T, D), g.reshape(1, D), w_in, ln_g.reshape(1, hidden).astype(jnp.float32),
      ln_b.reshape(1, hidden).astype(jnp.float32), w_s, bs_full, w_out)
    return out.reshape(B, S, D)


def _trunk(x, p):
    depth = p["norm_mix_g"].shape[0]
    for i in range(depth):
        kind, j = i % N_MIXERS, i // N_MIXERS
        g_mix = p["norm_mix_g"][i]
        mix = None
        if kind == 0:
            (qkv,) = _qkv_projection(x, g_mix, p["a_wqkv"][j], [1])
            o = _band_attention(qkv, p["a_bias"], A_RADIUS, sink=p["a_sink"][j])
            mix = ([o], None, p["a_wo"][j])
        elif kind == 1:
            x = _gmlp(x, g_mix, p["b_win"][j], p["b_ln_g"][j], p["b_ln_b"][j], p["b_ws"][j],
                      p["b_bs"][j], p["b_wo"][j])
        else:
            o_list, lse_list = [], []
            dils = [dil for _, dil in C_GROUPS]
            qkvs = _qkv_projection(x, g_mix, p["c_wqkv"][j], dils)
            for gi, (window, dil) in enumerate(C_GROUPS):
                o, lse = _band_attention(qkvs[gi], p["c_bias"][gi], window // (2 * dil), want_lse=True)
                o_list.append(o)
                lse_list.append(lse)
            mix = (o_list, lse_list, p["c_wo"][j])
        final_g = p["final_g"] if i == depth - 1 else None
        x = _ffn(x, p["norm_ffn_g"][i], p["ffn_w1"][i], p["ffn_w2"][i], mix=mix, final_g=final_g)
    return x


def kernel(x_prompt, x_sample, rel_bias, norm_mix_g, norm_ffn_g, final_g, ffn_w1, ffn_w2, a_wqkv,
           a_sink, a_wo, b_win, b_ln_g, b_ln_b, b_ws, b_bs, b_wo, c_wqkv, c_wo):
    bf16 = jnp.bfloat16
    f32 = jnp.float32
    S = x_prompt.shape[1]
    assert x_sample.shape[1] == S
    p = {
        "norm_mix_g": norm_mix_g.astype(f32), "norm_ffn_g": norm_ffn_g.astype(f32),
        "final_g": final_g.astype(f32),
        "ffn_w1": ffn_w1.astype(bf16), "ffn_w2": ffn_w2.astype(bf16),
        "a_wqkv": a_wqkv.astype(bf16), "a_sink": a_sink, "a_wo": a_wo.astype(bf16),
        "b_win": b_win.astype(bf16), "b_ln_g": b_ln_g, "b_ln_b": b_ln_b,
        "b_ws": b_ws.astype(bf16), "b_bs": b_bs, "b_wo": b_wo.astype(bf16),
        "c_wqkv": c_wqkv.astype(bf16), "c_wo": c_wo.astype(bf16),
        "a_bias": _band_bias_table(rel_bias, ATTN_TQ, A_RADIUS, 1, S),
        "c_bias": [_band_bias_table(rel_bias, ATTN_TQ, window // (2 * dil), dil, S // dil)
                   for window, dil in C_GROUPS],
    }
    return _trunk(x_prompt, p), _trunk(x_sample, p)
```

```python
import functools
import math

import numpy as np
import jax
import jax.numpy as jnp
from jax import lax
from jax.experimental import pallas as pl
from jax.experimental.pallas import tpu as pltpu

HEAD_DIM = 64
N_MIXERS = 3
N_HEADS = 16
N_KV_HEADS = 4
A_RADIUS = 128
B_CHUNK = 128
B_GROUPS = 8
C_GROUPS = ((128, 1), (512, 4), (2048, 16))
NUM_BUCKETS = 32
REL_MAX_DISTANCE = 1024
RMS_EPS = 1e-6
LN_EPS = 1e-5

Q_COLS = N_HEADS * HEAD_DIM
KV_COLS = N_KV_HEADS * HEAD_DIM
QKV_COLS = Q_COLS + 2 * KV_COLS
LANES = 128
HEADS_PER_TILE = LANES // HEAD_DIM
Q_PER_KV = N_HEADS // N_KV_HEADS
KVX_COLS = N_KV_HEADS * HEADS_PER_TILE * LANES
MASK_VALUE = -1e30
LOG2_E = math.log2(math.e)
ROWS_PER_WORD = 2
LN_2 = math.log(2.0)

V7X_VMEM_BYTES = 64 * 1024 * 1024
VMEM_LIMIT_BYTES = V7X_VMEM_BYTES - 8 * 1024 * 1024

ATTN_TQ = 128
ATTN_STEP_ROWS = 2048
FFN_TM = 1024
FFN_TM_MERGE = 512
FFN_CHUNK = 1024
PROJ_TM = 2048
PROJ_SUB = 512
PROJ_TM_MULTI = 512
GMLP_TM = 1024
GMLP_SUB = 256


def _const_spec(shape):
    zeros = (0,) * len(shape)
    return pl.BlockSpec(shape, lambda *_: zeros, pipeline_mode=pl.Buffered(1))


def _params(n_grid_dims):
    return pltpu.CompilerParams(
        dimension_semantics=("arbitrary",) * n_grid_dims,
        vmem_limit_bytes=VMEM_LIMIT_BYTES,
    )


def _rmsnorm_f32(x, g):
    ms = jnp.mean(x * x, axis=-1, keepdims=True)
    return x * lax.rsqrt(ms + RMS_EPS) * g


def _lse_lane(head):
    return head // HEADS_PER_TILE + HEAD_DIM * (head % HEADS_PER_TILE)


def _proj_kernel(*refs, dils):
    n = len(dils)
    x_ref, g_ref = refs[:2]
    w_refs = refs[2:2 + n]
    out_refs = refs[2 + n:2 + 4 * n]
    tm, d_model = x_ref.shape
    n_slabs = d_model // LANES
    sub = min(tm, PROJ_SUB)
    low_half = lax.broadcasted_iota(jnp.int32, (sub, LANES), 1) < HEAD_DIM
    work = [(s0, gi) for s0 in range(0, tm, sub) for gi in range(n)]
    for s0, gi in work:
        dil, w_ref = dils[gi], w_refs[gi]
        q_ref, kx_ref, vx_ref = out_refs[3 * gi:3 * gi + 3]
        n_rows = sub // dil
        if gi == 0:
            hn = _rmsnorm_f32(x_ref[s0:s0 + sub, :], g_ref[...])
            if any(d > 1 for d in dils):
                slab_ref = refs[2 + 4 * n]
                for j in range(n_slabs):
                    slab_ref[j] = hn[:, j * LANES:(j + 1) * LANES]
        if dil == 1:
            h = hn
        else:
            h = jnp.concatenate(
                [jnp.concatenate([slab_ref[j, pl.ds(r, n_rows, stride=dil), :] for j in range(n_slabs)], axis=1)
                 for r in range(dil)], axis=0)
        hb = h.astype(jnp.bfloat16)
        y_kv = jnp.dot(hb, w_ref[:, Q_COLS:], preferred_element_type=jnp.float32)
        y_q = jnp.dot(hb, w_ref[:, :Q_COLS], preferred_element_type=jnp.float32)
        pieces = []
        for o_ref, col0 in ((kx_ref, 0), (vx_ref, KV_COLS)):
            tiles = []
            for t in range(KV_COLS // LANES):
                pair = y_kv[:, col0 + t * LANES:col0 + (t + 1) * LANES]
                only_a = jnp.where(low_half, pair, 0.0)
                only_b = jnp.where(low_half, 0.0, pair)
                tiles += [only_a, pltpu.roll(only_a, HEAD_DIM, 1), pltpu.roll(only_b, HEAD_DIM, 1), only_b]
            pieces.append((o_ref, jnp.concatenate(tiles, axis=1).astype(jnp.bfloat16)))
        pieces.append((q_ref, (y_q * (HEAD_DIM ** -0.5 * LOG2_E)).astype(jnp.bfloat16)))
        for o_ref, val in pieces:
            words = pltpu.bitcast(val, o_ref.dtype)
            n_words = n_rows // ROWS_PER_WORD
            w0 = s0 // dil // ROWS_PER_WORD
            for r in range(dil):
                o_ref[r, w0:w0 + n_words, :] = words[r * n_words:(r + 1) * n_words]


def _qkv_projection(x, g, w, dils):
    B, S, D = x.shape
    n = len(dils)
    tm = PROJ_TM if n == 1 else PROJ_TM_MULTI
    in_specs = [pl.BlockSpec((None, tm, D), lambda b, i: (b, i, 0)), _const_spec((1, D))]
    in_specs += [pl.BlockSpec((D, QKV_COLS), lambda b, i, gi=gi: (0, gi), pipeline_mode=pl.Buffered(1))
                 for gi in range(n)]
    out_specs, out_shape = [], []
    for dil in dils:
        for cols in (Q_COLS, KVX_COLS, KVX_COLS):
            out_specs.append(pl.BlockSpec((None, dil, tm // dil // ROWS_PER_WORD, cols),
                                          lambda b, i: (b, 0, i, 0)))
            out_shape.append(jax.ShapeDtypeStruct((B, dil, S // dil // ROWS_PER_WORD, cols), jnp.uint32))
    scratch = []
    if any(d > 1 for d in dils):
        scratch.append(pltpu.VMEM((D // LANES, min(tm, PROJ_SUB), LANES), jnp.float32))
    outs = pl.pallas_call(
        functools.partial(_proj_kernel, dils=tuple(dils)),
        grid=(B, S // tm),
        in_specs=in_specs,
        out_specs=out_specs,
        out_shape=out_shape,
        scratch_shapes=scratch,
        compiler_params=_params(2),
        name="qkv_proj_" + "_".join(f"d{d}" for d in dils),
    )(x, g.reshape(1, D), *([w] * n))
    return [tuple(outs[3 * gi:3 * gi + 3]) for gi in range(n)]


def _rel_bucket(rel):
    half = NUM_BUCKETS // 2
    max_exact = half // 2
    n = np.abs(rel)
    large = max_exact + (np.log(np.maximum(n, 1) / max_exact) / np.log(REL_MAX_DISTANCE / max_exact)
                         * (half - max_exact)).astype(np.int32)
    large = np.minimum(large, half - 1)
    return (rel > 0).astype(np.int32) * half + np.where(n < max_exact, n, large)


def _window_geometry(tq, radius, seq_len):
    width = min(tq + 2 * radius, seq_len)
    if seq_len // tq == 1:
        return width, (0,)
    return width, (0, radius, width - tq)


def _dot_heads(g, v):
    return g * Q_PER_KV + v, g * Q_PER_KV + HEADS_PER_TILE + v


def _band_bias_table(rel_bias, tq, radius, dil, seq_len):
    width, offsets = _window_geometry(tq, radius, seq_len)
    span = max(offsets) + tq - 1
    n_diag = span + width
    diag_rel = np.arange(n_diag) - span
    per_diag = jnp.transpose(rel_bias[_rel_bucket(diag_rel * dil)], (1, 0)).astype(jnp.float32) * LOG2_E
    per_diag = jnp.where(jnp.asarray(np.abs(diag_rel) <= radius)[None], per_diag, MASK_VALUE)
    padded = jnp.pad(per_diag, ((0, 0), (0, 1)))
    tiled = jnp.broadcast_to(padded[:, None, :], (N_HEADS, tq, n_diag + 1)).reshape(N_HEADS, -1)
    skewed = tiled[:, :tq * n_diag].reshape(N_HEADS, tq, n_diag)
    order = [h for g in range(N_KV_HEADS) for v in range(HEADS_PER_TILE) for h in _dot_heads(g, v)]
    skewed = skewed[np.asarray(order)]
    tables = []
    for off in offsets:
        c0 = span - off
        tables.append(skewed[:, :, c0:c0 + width].reshape(N_HEADS * tq, width))
    return jnp.stack(tables, axis=0)


def _attn_kernel(*refs, tq, radius, seq_len, dil, has_sink, want_lse):
    refs = list(refs)
    q_ref, kx_ref, vx_ref, bias_ref = refs[:4]
    pos = 4
    sink_ref = None
    if has_sink:
        sink_ref = refs[pos]
        pos += 1
    o_ref = refs[pos]
    pos += 1
    lse_ref = refs[pos] if want_lse else None

    n_res = q_ref.shape[0]
    blocks = q_ref.shape[1] * ROWS_PER_WORD // tq
    width, offsets = _window_geometry(tq, radius, seq_len)
    n_blk = seq_len // tq
    i = pl.program_id(2)
    bf16 = jnp.bfloat16

    lane = lax.broadcasted_iota(jnp.int32, (tq, LANES), 1)
    low_half2 = lax.broadcasted_iota(jnp.int32, (2 * tq, LANES), 1) < HEAD_DIM
    top_rows = lax.broadcasted_iota(jnp.int32, (2 * tq, 1), 0) < tq
    ones_lo = jnp.where(lax.broadcasted_iota(jnp.int32, (width, LANES), 1) < HEAD_DIM, 1.0, 0.0).astype(bf16)
    ones_hi = jnp.where(lax.broadcasted_iota(jnp.int32, (width, LANES), 1) < HEAD_DIM, 0.0, 1.0).astype(bf16)

    def window_start(j):
        blk = i * blocks + j
        if n_blk == 1:
            return 0
        return jnp.clip(blk * tq - radius, 0, seq_len - width)

    def load_rows(ref, res, start, n, lanes):
        word_start = start // ROWS_PER_WORD
        if not isinstance(word_start, int):
            word_start = pl.multiple_of(word_start, HEAD_DIM // ROWS_PER_WORD)
        words = ref[res, pl.ds(word_start, n // ROWS_PER_WORD), lanes]
        return pltpu.bitcast(words, bf16)

    def scores(res, j, g):
        base = g * Q_PER_KV * HEAD_DIM
        q2 = jnp.concatenate([load_rows(q_ref, res, j * tq, tq, slice(base, base + LANES)),
                              load_rows(q_ref, res, j * tq, tq, slice(base + LANES, base + 2 * LANES))],
                             axis=0)
        k_win = jnp.concatenate(
            [load_rows(kx_ref, res, window_start(j), width, slice(t * LANES, (t + 1) * LANES))
             for t in range(g * HEADS_PER_TILE, (g + 1) * HEADS_PER_TILE)], axis=0)
        return lax.dot_general(q2, k_win, (((1,), (1,)), ((), ())), preferred_element_type=jnp.float32)

    work = [(res, j, g) for res in range(n_res) for j in range(blocks) for g in range(N_KV_HEADS)]
    s_next = scores(*work[0])
    for n, (res, j, g) in enumerate(work):
        base = g * Q_PER_KV * HEAD_DIM
        rows = slice(j * tq, (j + 1) * tq)
        blk = i * blocks + j
        if n_blk == 1:
            variant = 0
        else:
            variant = jnp.where(blk == 0, 0, jnp.where(blk == n_blk - 1, 2, 1))
        if dil > 1:
            token_rows = pl.ds(blk * (tq * dil) + pl.program_id(1) * n_res + res, tq, stride=dil)
        s_all = s_next
        if n + 1 < len(work):
            s_next = scores(*work[n + 1])
        probs = []
        maxes = []
        sink_terms = []
        for v in range(HEADS_PER_TILE):
            t = g * HEADS_PER_TILE + v
            s = s_all[:, v * width:(v + 1) * width] + bias_ref[variant, t * 2 * tq:(t + 1) * 2 * tq, :]
            m = jnp.max(s, axis=-1, keepdims=True)
            if has_sink:
                h_top, h_bot = _dot_heads(g, v)
                sk = jnp.where(top_rows, sink_ref[h_top], sink_ref[h_bot])
                m = jnp.maximum(m, sk)
                sink_terms.append(jnp.exp2(sk - m))
            maxes.append(m)
            probs.append(jnp.exp2(s - m).astype(bf16))
        v_tiles = [load_rows(vx_ref, res, window_start(j), width, slice(t * LANES, (t + 1) * LANES))
                   for t in range(g * HEADS_PER_TILE, (g + 1) * HEADS_PER_TILE)]
        v_rhs = jnp.concatenate([jnp.concatenate([v_tiles[0], ones_lo], axis=1),
                                 jnp.concatenate([v_tiles[1], ones_hi], axis=1)], axis=0)
        acc = jnp.dot(jnp.concatenate(probs, axis=1), v_rhs, preferred_element_type=jnp.float32)
        denom = acc[:, LANES:]
        if has_sink:
            denom = denom + jnp.where(low_half2, sink_terms[0], sink_terms[1])
        o = acc[:, :LANES] / denom
        if dil == 1:
            o_ref[rows, base:base + LANES] = o[:tq].astype(o_ref.dtype)
            o_ref[rows, base + LANES:base + 2 * LANES] = o[tq:].astype(o_ref.dtype)
        else:
            o_ref[g * HEADS_PER_TILE, token_rows, :] = o[:tq]
            o_ref[g * HEADS_PER_TILE + 1, token_rows, :] = o[tq:]
        if want_lse:
            if g == 0:
                lse_acc = jnp.zeros((tq, LANES), jnp.float32)
            lse = jnp.where(low_half2, maxes[0] * LN_2, maxes[1] * LN_2) + jnp.log(denom)
            for half, half_rows in ((0, slice(0, tq)), (1, slice(tq, 2 * tq))):
                tile = g * HEADS_PER_TILE + half
                here = (lane == tile) | (lane == HEAD_DIM + tile)
                lse_acc = jnp.where(here, lse[half_rows], lse_acc)
            if g == N_KV_HEADS - 1:
                if dil == 1:
                    lse_ref[rows, :] = lse_acc
                else:
                    lse_ref[token_rows, :] = lse_acc


def _band_attention(qkv, bias, radius, *, sink=None, want_lse=False):
    q, kx, vx = qkv
    B, dil, L = q.shape[0], q.shape[1], q.shape[2] * ROWS_PER_WORD
    S = L * dil
    tq = ATTN_TQ
    ts = min(L, ATTN_STEP_ROWS)
    n_res = min(dil, ATTN_STEP_ROWS // ts)
    D = Q_COLS

    in_specs = [
        pl.BlockSpec((None, n_res, ts // ROWS_PER_WORD, Q_COLS), lambda b, r, i: (b, r, i, 0)),
        pl.BlockSpec((None, n_res, L // ROWS_PER_WORD, KVX_COLS), lambda b, r, i: (b, r, 0, 0)),
        pl.BlockSpec((None, n_res, L // ROWS_PER_WORD, KVX_COLS), lambda b, r, i: (b, r, 0, 0)),
        _const_spec(bias.shape),
    ]
    args = [q, kx, vx, bias]
    if sink is not None:
        in_specs.append(pl.BlockSpec(memory_space=pltpu.SMEM))
        args.append(sink.astype(jnp.float32) * LOG2_E)
    if dil == 1:
        out_specs = [pl.BlockSpec((None, ts, D), lambda b, r, i: (b, i, 0))]
        out_shape = [jax.ShapeDtypeStruct((B, S, D), jnp.bfloat16)]
        lse_spec = pl.BlockSpec((None, ts, LANES), lambda b, r, i: (b, i, 0))
    else:
        assert want_lse
        out_specs = [pl.BlockSpec((None, D // LANES, S, LANES), lambda b, r, i: (b, 0, 0, 0))]
        out_shape = [jax.ShapeDtypeStruct((B, D // LANES, S, LANES), jnp.float32)]
        lse_spec = pl.BlockSpec((None, S, LANES), lambda b, r, i: (b, 0, 0))
    if want_lse:
        out_specs.append(lse_spec)
        out_shape.append(jax.ShapeDtypeStruct((B, S, LANES), jnp.float32))

    kernel = functools.partial(_attn_kernel, tq=tq, radius=radius, seq_len=L, dil=dil,
                               has_sink=sink is not None, want_lse=want_lse)
    out = pl.pallas_call(
        kernel,
        grid=(B, dil // n_res, L // ts),
        in_specs=in_specs,
        out_specs=out_specs,
        out_shape=out_shape,
        compiler_params=_params(3),
        name=f"band_attn_r{radius}_d{dil}",
    )(*args)
    if want_lse:
        return out[0], out[1]
    return out[0]


def _head_expand_matrix():
    e = np.zeros((LANES, Q_COLS), np.float32)
    for h in range(N_HEADS):
        e[_lse_lane(h), h * HEAD_DIM:(h + 1) * HEAD_DIM] = 1.0
    return jnp.asarray(np.concatenate([e, e], axis=0), jnp.bfloat16)


def _read_rows_f32(o_ref):
    if len(o_ref.shape) == 2:
        return o_ref[...].astype(jnp.float32)
    return jnp.concatenate([o_ref[j] for j in range(o_ref.shape[0])], axis=1)


def _ffn_kernel(*refs, n_mix, final):
    refs = list(refs)
    x_ref = refs[0]
    pos = 1
    x = x_ref[...]
    if n_mix >= 1:
        o_refs = refs[pos:pos + n_mix]
        pos += n_mix
        if n_mix > 1:
            lse_refs = refs[pos:pos + n_mix]
            expand_ref = refs[pos + n_mix]
            pos += n_mix + 1
        wo_ref = refs[pos]
        pos += 1
        if n_mix == 1:
            o = o_refs[0][...]
        else:
            lses = [r[...] for r in lse_refs]
            top = functools.reduce(jnp.maximum, lses)
            es = [jnp.exp(l - top) for l in lses]
            inv_tot = 1.0 / functools.reduce(lambda a, b: a + b, es)
            o = None
            w_rest = None
            for gi, (e, o_ref) in enumerate(zip(es, o_refs)):
                if gi < n_mix - 1:
                    w = e * inv_tot
                    w_hi = w.astype(jnp.bfloat16)
                    w_lo = (w - w_hi.astype(jnp.float32)).astype(jnp.bfloat16)
                    w_full = jnp.dot(jnp.concatenate([w_hi, w_lo], axis=1), expand_ref[...],
                                     preferred_element_type=jnp.float32)
                    w_rest = 1.0 - w_full if w_rest is None else w_rest - w_full
                else:
                    w_full = w_rest
                term = w_full * _read_rows_f32(o_ref)
                o = term if o is None else o + term
            o = o.astype(jnp.bfloat16)
        x = x + jnp.dot(o, wo_ref[...], preferred_element_type=jnp.float32)
    g_ref, w1_ref, w2_ref = refs[pos:pos + 3]
    pos += 3
    if final:
        fg_ref = refs[pos]
        pos += 1
    out_ref = refs[pos]

    h = _rmsnorm_f32(x, g_ref[...]).astype(jnp.bfloat16)
    acc = x
    d_ff = w1_ref.shape[1]
    for c in range(d_ff // FFN_CHUNK):
        cols = slice(c * FFN_CHUNK, (c + 1) * FFN_CHUNK)
        a = jnp.dot(h, w1_ref[:, cols], preferred_element_type=jnp.float32)
        a = jnp.square(jnp.maximum(a, 0.0)).astype(jnp.bfloat16)
        acc = acc + jnp.dot(a, w2_ref[cols, :], preferred_element_type=jnp.float32)
    if final:
        acc = _rmsnorm_f32(acc, fg_ref[...])
    out_ref[...] = acc


def _ffn(x, g, w1, w2, *, mix=None, final_g=None):
    B, S, D = x.shape
    d_ff = w1.shape[1]
    tm = FFN_TM_MERGE if mix is not None and len(mix[0]) > 1 else FFN_TM
    row_spec = lambda c: pl.BlockSpec((None, tm, c), lambda b, i: (b, i, 0))
    slab_spec = lambda c: pl.BlockSpec((None, c // LANES, tm, LANES), lambda b, i: (b, 0, i, 0))
    in_specs = [row_spec(D)]
    args = [x]
    n_mix = 0
    if mix is not None:
        o_list, lse_list, w_o = mix
        n_mix = len(o_list)
        in_specs += [row_spec(D) if o.ndim == 3 else slab_spec(D) for o in o_list]
        args += o_list
        if n_mix > 1:
            in_specs += [row_spec(LANES)] * n_mix + [_const_spec((2 * LANES, Q_COLS))]
            args += lse_list + [_head_expand_matrix()]
        in_specs.append(_const_spec((D, D)))
        args.append(w_o)
    in_specs += [_const_spec((1, D)), _const_spec((D, d_ff)), _const_spec((d_ff, D))]
    args += [g.reshape(1, D), w1, w2]
    if final_g is not None:
        in_specs.append(_const_spec((1, D)))
        args.append(final_g.reshape(1, D))
    out = pl.pallas_call(
        functools.partial(_ffn_kernel, n_mix=n_mix, final=final_g is not None),
        grid=(B, S // tm),
        in_specs=in_specs,
        out_specs=row_spec(D),
        out_shape=jax.ShapeDtypeStruct((B, S, D), jnp.float32),
        compiler_params=_params(2),
        name=f"ffn_mix{n_mix}" + ("_final" if final_g is not None else ""),
    )(*args)
    return out


def _gmlp_kernel(x_ref, g_ref, win_ref, lng_ref, lnb_ref, ws_ref, bs_ref, wout_ref, o_ref):
    tm = x_ref.shape[0]
    hidden = wout_ref.shape[0]
    gcols = hidden // B_GROUPS
    subs = [slice(r0, r0 + GMLP_SUB) for r0 in range(0, tm, GMLP_SUB)]
    zs = []
    for rows in subs:
        h = _rmsnorm_f32(x_ref[rows, :], g_ref[...]).astype(jnp.bfloat16)
        zs.append(jnp.dot(h, win_ref[...], preferred_element_type=jnp.float32))
    for rows, z in zip(subs, zs):
        z = 0.5 * z * (1.0 + lax.erf(z * (2.0 ** -0.5)))
        u = z[:, :hidden]
        v = z[:, hidden:]
        mu = jnp.mean(v, axis=-1, keepdims=True)
        vc = v - mu
        var = jnp.mean(vc * vc, axis=-1, keepdims=True)
        vn = (vc * lax.rsqrt(var + LN_EPS) * lng_ref[...] + lnb_ref[...]).astype(jnp.bfloat16)
        gated = []
        for r0 in range(0, GMLP_SUB, B_CHUNK):
            parts = []
            for grp in range(B_GROUPS):
                parts.append(jnp.dot(ws_ref[grp], vn[r0:r0 + B_CHUNK, grp * gcols:(grp + 1) * gcols],
                                     preferred_element_type=jnp.float32))
            mixed = jnp.concatenate(parts, axis=-1) + bs_ref[...]
            gated.append((u[r0:r0 + B_CHUNK, :] * mixed).astype(jnp.bfloat16))
        t = jnp.concatenate(gated, axis=0)
        o_ref[rows, :] = x_ref[rows, :] + jnp.dot(t, wout_ref[...], preferred_element_type=jnp.float32)


def _gmlp(x, g, w_in, ln_g, ln_b, w_s, b_s, w_out):
    B, S, D = x.shape
    hidden = w_out.shape[0]
    T = B * S
    tm = GMLP_TM
    bs_full = jnp.repeat(jnp.transpose(b_s).astype(jnp.float32), hidden // B_GROUPS, axis=1)
    row_spec = pl.BlockSpec((tm, D), lambda i: (i, 0))
    out = pl.pallas_call(
        _gmlp_kernel,
        grid=(T // tm,),
        in_specs=[row_spec, _const_spec((1, D)), _const_spec((D, 2 * hidden)),
                  _const_spec((1, hidden)), _const_spec((1, hidden)),
                  _const_spec((B_GROUPS, B_CHUNK, B_CHUNK)), _const_spec((B_CHUNK, hidden)),
                  _const_spec((hidden, D))],
        out_specs=row_spec,
        out_shape=jax.ShapeDtypeStruct((T, D), jnp.float32),
        compiler_params=_params(1),
        name="gmlp",
    )(x.reshape(T, D), g.reshape(1, D), w_in, ln_g.reshape(1, hidden).astype(jnp.float32),
      ln_b.reshape(1, hidden).astype(jnp.float32), w_s, bs_full, w_out)
    return out.reshape(B, S, D)


def _trunk(x, p):
    depth = p["norm_mix_g"].shape[0]
    for i in range(depth):
        kind, j = i % N_MIXERS, i // N_MIXERS
        g_mix = p["norm_mix_g"][i]
        mix = None
        if kind == 0:
            (qkv,) = _qkv_projection(x, g_mix, p["a_wqkv"][j], [1])
            o = _band_attention(qkv, p["a_bias"], A_RADIUS, sink=p["a_sink"][j])
            mix = ([o], None, p["a_wo"][j])
        elif kind == 1:
            x = _gmlp(x, g_mix, p["b_win"][j], p["b_ln_g"][j], p["b_ln_b"][j], p["b_ws"][j],
                      p["b_bs"][j], p["b_wo"][j])
        else:
            o_list, lse_list = [], []
            dils = [dil for _, dil in C_GROUPS]
            qkvs = _qkv_projection(x, g_mix, p["c_wqkv"][j], dils)
            for gi, (window, dil) in enumerate(C_GROUPS):
                o, lse = _band_attention(qkvs[gi], p["c_bias"][gi], window // (2 * dil), want_lse=True)
                o_list.append(o)
                lse_list.append(lse)
            mix = (o_list, lse_list, p["c_wo"][j])
        final_g = p["final_g"] if i == depth - 1 else None
        x = _ffn(x, p["norm_ffn_g"][i], p["ffn_w1"][i], p["ffn_w2"][i], mix=mix, final_g=final_g)
    return x


def kernel(x_prompt, x_sample, rel_bias, norm_mix_g, norm_ffn_g, final_g, ffn_w1, ffn_w2, a_wqkv,
           a_sink, a_wo, b_win, b_ln_g, b_ln_b, b_ws, b_bs, b_wo, c_wqkv, c_wo):
    bf16 = jnp.bfloat16
    f32 = jnp.float32
    S = x_prompt.shape[1]
    assert x_sample.shape[1] == S
    p = {
        "norm_mix_g": norm_mix_g.astype(f32), "norm_ffn_g": norm_ffn_g.astype(f32),
        "final_g": final_g.astype(f32),
        "ffn_w1": ffn_w1.astype(bf16), "ffn_w2": ffn_w2.astype(bf16),
        "a_wqkv": a_wqkv.astype(bf16), "a_sink": a_sink, "a_wo": a_wo.astype(bf16),
        "b_win": b_win.astype(bf16), "b_ln_g": b_ln_g, "b_ln_b": b_ln_b,
        "b_ws": b_ws.astype(bf16), "b_bs": b_bs, "b_wo": b_wo.astype(bf16),
        "c_wqkv": c_wqkv.astype(bf16), "c_wo": c_wo.astype(bf16),
        "a_bias": _band_bias_table(rel_bias, ATTN_TQ, A_RADIUS, 1, S),
        "c_bias": [_band_bias_table(rel_bias, ATTN_TQ, window // (2 * dil), dil, S // dil)
                   for window, dil in C_GROUPS],
    }
    return _trunk(x_prompt, p), _trunk(x_sample, p)
```

```python
import functools
import math

import numpy as np
import jax
import jax.numpy as jnp
from jax import lax
from jax.experimental import pallas as pl
from jax.experimental.pallas import tpu as pltpu

HEAD_DIM = 64
N_MIXERS = 3
N_HEADS = 16
N_KV_HEADS = 4
A_RADIUS = 128
B_CHUNK = 128
B_GROUPS = 8
C_GROUPS = ((128, 1), (512, 4), (2048, 16))
NUM_BUCKETS = 32
REL_MAX_DISTANCE = 1024
RMS_EPS = 1e-6
LN_EPS = 1e-5

Q_COLS = N_HEADS * HEAD_DIM
KV_COLS = N_KV_HEADS * HEAD_DIM
QKV_COLS = Q_COLS + 2 * KV_COLS
LANES = 128
HEADS_PER_TILE = LANES // HEAD_DIM
Q_PER_KV = N_HEADS // N_KV_HEADS
KVX_COLS = N_KV_HEADS * HEADS_PER_TILE * LANES
MASK_VALUE = -1e30
LOG2_E = math.log2(math.e)
ROWS_PER_WORD = 2
LN_2 = math.log(2.0)

V7X_VMEM_BYTES = 64 * 1024 * 1024
VMEM_LIMIT_BYTES = V7X_VMEM_BYTES - 8 * 1024 * 1024

ATTN_TQ = 128
ATTN_STEP_ROWS = 2048
FFN_TM = 1024
FFN_TM_MERGE = 512
FFN_CHUNK = 1024
PROJ_TM = 1024
PROJ_TM_MULTI = 512
GMLP_TM = 1024
GMLP_SUB = 256


def _const_spec(shape):
    zeros = (0,) * len(shape)
    return pl.BlockSpec(shape, lambda *_: zeros, pipeline_mode=pl.Buffered(1))


def _params(n_grid_dims):
    return pltpu.CompilerParams(
        dimension_semantics=("arbitrary",) * n_grid_dims,
        vmem_limit_bytes=VMEM_LIMIT_BYTES,
    )


def _rmsnorm_f32(x, g):
    ms = jnp.mean(x * x, axis=-1, keepdims=True)
    return x * lax.rsqrt(ms + RMS_EPS) * g


def _row_input(x, tm):
    if not isinstance(x, tuple):
        return [pl.BlockSpec((None, tm, x.shape[-1]), lambda b, i: (b, i, 0))], [x], None
    xp, xs = x
    bp, S, D = xp.shape
    last = S // tm - 1
    spec_p = pl.BlockSpec((None, tm, D), lambda b, i: (jnp.minimum(b, bp - 1), jnp.where(b < bp, i, last), 0))
    spec_s = pl.BlockSpec((None, tm, D), lambda b, i: (jnp.maximum(b - bp, 0), jnp.where(b < bp, 0, i), 0))
    return [spec_p, spec_s], [xp, xs], bp


def _read_rows(x_refs, bp):
    if bp is None:
        return x_refs[0][...]
    return jnp.where(pl.program_id(0) < bp, x_refs[0][...], x_refs[1][...])


def _batch_shape(x):
    if isinstance(x, tuple):
        return (x[0].shape[0] + x[1].shape[0],) + x[0].shape[1:]
    return x.shape


def _lse_lane(head):
    return head // HEADS_PER_TILE + HEAD_DIM * (head % HEADS_PER_TILE)


def _proj_kernel(*refs, dils, bp):
    n = len(dils)
    n_x = 1 if bp is None else 2
    x_refs, refs = refs[:n_x], refs[n_x - 1:]
    g_ref = refs[1]
    w_refs = refs[2:2 + n]
    out_refs = refs[2 + n:2 + 4 * n]
    hn = _rmsnorm_f32(_read_rows(x_refs, bp), g_ref[...])
    tm, d_model = hn.shape
    n_slabs = d_model // LANES
    if any(d > 1 for d in dils):
        slab_ref = refs[2 + 4 * n]
        for j in range(n_slabs):
            slab_ref[j] = hn[:, j * LANES:(j + 1) * LANES]
    low_half = lax.broadcasted_iota(jnp.int32, (tm, LANES), 1) < HEAD_DIM
    for gi, (dil, w_ref) in enumerate(zip(dils, w_refs)):
        q_ref, kx_ref, vx_ref = out_refs[3 * gi:3 * gi + 3]
        n_rows = tm // dil
        if dil == 1:
            h = hn
        else:
            h = jnp.concatenate(
                [jnp.concatenate([slab_ref[j, pl.ds(r, n_rows, stride=dil), :] for j in range(n_slabs)], axis=1)
                 for r in range(dil)], axis=0)
        hb = h.astype(jnp.bfloat16)
        y_kv = jnp.dot(hb, w_ref[:, Q_COLS:], preferred_element_type=jnp.float32)
        y_q = jnp.dot(hb, w_ref[:, :Q_COLS], preferred_element_type=jnp.float32)
        pieces = []
        for o_ref, col0 in ((kx_ref, 0), (vx_ref, KV_COLS)):
            tiles = []
            for t in range(KV_COLS // LANES):
                pair = y_kv[:, col0 + t * LANES:col0 + (t + 1) * LANES]
                only_a = jnp.where(low_half, pair, 0.0)
                only_b = jnp.where(low_half, 0.0, pair)
                tiles += [only_a, pltpu.roll(only_a, HEAD_DIM, 1), pltpu.roll(only_b, HEAD_DIM, 1), only_b]
            pieces.append((o_ref, jnp.concatenate(tiles, axis=1).astype(jnp.bfloat16)))
        pieces.append((q_ref, (y_q * (HEAD_DIM ** -0.5 * LOG2_E)).astype(jnp.bfloat16)))
        for o_ref, val in pieces:
            words = pltpu.bitcast(val, o_ref.dtype)
            n_words = n_rows // ROWS_PER_WORD
            for r in range(dil):
                o_ref[r] = words[r * n_words:(r + 1) * n_words]


def _qkv_projection(x, g, w, dils):
    B, S, D = _batch_shape(x)
    n = len(dils)
    tm = PROJ_TM if n == 1 else PROJ_TM_MULTI
    in_specs, x_args, bp = _row_input(x, tm)
    in_specs.append(_const_spec((1, D)))
    in_specs += [pl.BlockSpec((D, QKV_COLS), lambda b, i, gi=gi: (0, gi), pipeline_mode=pl.Buffered(1))
                 for gi in range(n)]
    out_specs, out_shape = [], []
    for dil in dils:
        for cols in (Q_COLS, KVX_COLS, KVX_COLS):
            out_specs.append(pl.BlockSpec((None, dil, tm // dil // ROWS_PER_WORD, cols),
                                          lambda b, i: (b, 0, i, 0)))
            out_shape.append(jax.ShapeDtypeStruct((B, dil, S // dil // ROWS_PER_WORD, cols), jnp.uint32))
    scratch = []
    if any(d > 1 for d in dils):
        scratch.append(pltpu.VMEM((D // LANES, tm, LANES), jnp.float32))
    outs = pl.pallas_call(
        functools.partial(_proj_kernel, dils=tuple(dils), bp=bp),
        grid=(B, S // tm),
        in_specs=in_specs,
        out_specs=out_specs,
        out_shape=out_shape,
        scratch_shapes=scratch,
        compiler_params=_params(2),
        name="qkv_proj_" + "_".join(f"d{d}" for d in dils),
    )(*x_args, g.reshape(1, D), *([w] * n))
    return [tuple(outs[3 * gi:3 * gi + 3]) for gi in range(n)]


def _rel_bucket(rel):
    half = NUM_BUCKETS // 2
    max_exact = half // 2
    n = np.abs(rel)
    large = max_exact + (np.log(np.maximum(n, 1) / max_exact) / np.log(REL_MAX_DISTANCE / max_exact)
                         * (half - max_exact)).astype(np.int32)
    large = np.minimum(large, half - 1)
    return (rel > 0).astype(np.int32) * half + np.where(n < max_exact, n, large)


def _window_geometry(tq, radius, seq_len):
    width = min(tq + 2 * radius, seq_len)
    if seq_len // tq == 1:
        return width, (0,)
    return width, (0, radius, width - tq)


def _dot_heads(g, v):
    return g * Q_PER_KV + v, g * Q_PER_KV + HEADS_PER_TILE + v


def _band_bias_table(rel_bias, tq, radius, dil, seq_len):
    width, offsets = _window_geometry(tq, radius, seq_len)
    span = max(offsets) + tq - 1
    n_diag = span + width
    diag_rel = np.arange(n_diag) - span
    per_diag = jnp.transpose(rel_bias[_rel_bucket(diag_rel * dil)], (1, 0)).astype(jnp.float32) * LOG2_E
    per_diag = jnp.where(jnp.asarray(np.abs(diag_rel) <= radius)[None], per_diag, MASK_VALUE)
    padded = jnp.pad(per_diag, ((0, 0), (0, 1)))
    tiled = jnp.broadcast_to(padded[:, None, :], (N_HEADS, tq, n_diag + 1)).reshape(N_HEADS, -1)
    skewed = tiled[:, :tq * n_diag].reshape(N_HEADS, tq, n_diag)
    order = [h for g in range(N_KV_HEADS) for v in range(HEADS_PER_TILE) for h in _dot_heads(g, v)]
    skewed = skewed[np.asarray(order)]
    tables = []
    for off in offsets:
        c0 = span - off
        tables.append(skewed[:, :, c0:c0 + width].reshape(N_HEADS * tq, width))
    return jnp.stack(tables, axis=0)


def _attn_kernel(*refs, tq, radius, seq_len, dil, has_sink, want_lse):
    refs = list(refs)
    q_ref, kx_ref, vx_ref, bias_ref = refs[:4]
    pos = 4
    sink_ref = None
    if has_sink:
        sink_ref = refs[pos]
        pos += 1
    o_ref = refs[pos]
    pos += 1
    lse_ref = refs[pos] if want_lse else None

    n_res = q_ref.shape[0]
    blocks = q_ref.shape[1] * ROWS_PER_WORD // tq
    width, offsets = _window_geometry(tq, radius, seq_len)
    n_blk = seq_len // tq
    i = pl.program_id(2)
    bf16 = jnp.bfloat16

    lane = lax.broadcasted_iota(jnp.int32, (tq, LANES), 1)
    low_half2 = lax.broadcasted_iota(jnp.int32, (2 * tq, LANES), 1) < HEAD_DIM
    top_rows = lax.broadcasted_iota(jnp.int32, (2 * tq, 1), 0) < tq
    ones_lo = jnp.where(lax.broadcasted_iota(jnp.int32, (width, LANES), 1) < HEAD_DIM, 1.0, 0.0).astype(bf16)
    ones_hi = jnp.where(lax.broadcasted_iota(jnp.int32, (width, LANES), 1) < HEAD_DIM, 0.0, 1.0).astype(bf16)

    def window_start(j):
        blk = i * blocks + j
        if n_blk == 1:
            return 0
        return jnp.clip(blk * tq - radius, 0, seq_len - width)

    def load_rows(ref, res, start, n, lanes):
        word_start = start // ROWS_PER_WORD
        if not isinstance(word_start, int):
            word_start = pl.multiple_of(word_start, HEAD_DIM // ROWS_PER_WORD)
        words = ref[res, pl.ds(word_start, n // ROWS_PER_WORD), lanes]
        return pltpu.bitcast(words, bf16)

    def scores(res, j, g):
        base = g * Q_PER_KV * HEAD_DIM
        q2 = jnp.concatenate([load_rows(q_ref, res, j * tq, tq, slice(base, base + LANES)),
                              load_rows(q_ref, res, j * tq, tq, slice(base + LANES, base + 2 * LANES))],
                             axis=0)
        k_win = jnp.concatenate(
            [load_rows(kx_ref, res, window_start(j), width, slice(t * LANES, (t + 1) * LANES))
             for t in range(g * HEADS_PER_TILE, (g + 1) * HEADS_PER_TILE)], axis=0)
        return lax.dot_general(q2, k_win, (((1,), (1,)), ((), ())), preferred_element_type=jnp.float32)

    work = [(res, j, g) for res in range(n_res) for j in range(blocks) for g in range(N_KV_HEADS)]
    s_next = scores(*work[0])
    for n, (res, j, g) in enumerate(work):
        base = g * Q_PER_KV * HEAD_DIM
        rows = slice(j * tq, (j + 1) * tq)
        blk = i * blocks + j
        if n_blk == 1:
            variant = 0
        else:
            variant = jnp.where(blk == 0, 0, jnp.where(blk == n_blk - 1, 2, 1))
        if dil > 1:
            token_rows = pl.ds(blk * (tq * dil) + pl.program_id(1) * n_res + res, tq, stride=dil)
        s_all = s_next
        if n + 1 < len(work):
            s_next = scores(*work[n + 1])
        probs = []
        maxes = []
        sink_terms = []
        for v in range(HEADS_PER_TILE):
            t = g * HEADS_PER_TILE + v
            s = s_all[:, v * width:(v + 1) * width] + bias_ref[variant, t * 2 * tq:(t + 1) * 2 * tq, :]
            m = jnp.max(s, axis=-1, keepdims=True)
            if has_sink:
                h_top, h_bot = _dot_heads(g, v)
                sk = jnp.where(top_rows, sink_ref[h_top], sink_ref[h_bot])
                m = jnp.maximum(m, sk)
                sink_terms.append(jnp.exp2(sk - m))
            maxes.append(m)
            probs.append(jnp.exp2(s - m).astype(bf16))
        v_tiles = [load_rows(vx_ref, res, window_start(j), width, slice(t * LANES, (t + 1) * LANES))
                   for t in range(g * HEADS_PER_TILE, (g + 1) * HEADS_PER_TILE)]
        v_rhs = jnp.concatenate([jnp.concatenate([v_tiles[0], ones_lo], axis=1),
                                 jnp.concatenate([v_tiles[1], ones_hi], axis=1)], axis=0)
        acc = jnp.dot(jnp.concatenate(probs, axis=1), v_rhs, preferred_element_type=jnp.float32)
        denom = acc[:, LANES:]
        if has_sink:
            denom = denom + jnp.where(low_half2, sink_terms[0], sink_terms[1])
        o = acc[:, :LANES] / denom
        if dil == 1:
            o_ref[rows, base:base + LANES] = o[:tq].astype(o_ref.dtype)
            o_ref[rows, base + LANES:base + 2 * LANES] = o[tq:].astype(o_ref.dtype)
        else:
            o_ref[g * HEADS_PER_TILE, token_rows, :] = o[:tq]
            o_ref[g * HEADS_PER_TILE + 1, token_rows, :] = o[tq:]
        if want_lse:
            if g == 0:
                lse_acc = jnp.zeros((tq, LANES), jnp.float32)
            lse = jnp.where(low_half2, maxes[0] * LN_2, maxes[1] * LN_2) + jnp.log(denom)
            for half, half_rows in ((0, slice(0, tq)), (1, slice(tq, 2 * tq))):
                tile = g * HEADS_PER_TILE + half
                here = (lane == tile) | (lane == HEAD_DIM + tile)
                lse_acc = jnp.where(here, lse[half_rows], lse_acc)
            if g == N_KV_HEADS - 1:
                if dil == 1:
                    lse_ref[rows, :] = lse_acc
                else:
                    lse_ref[token_rows, :] = lse_acc


def _band_attention(qkv, bias, radius, *, sink=None, want_lse=False):
    q, kx, vx = qkv
    B, dil, L = q.shape[0], q.shape[1], q.shape[2] * ROWS_PER_WORD
    S = L * dil
    tq = ATTN_TQ
    ts = min(L, ATTN_STEP_ROWS)
    n_res = min(dil, ATTN_STEP_ROWS // ts)
    D = Q_COLS

    in_specs = [
        pl.BlockSpec((None, n_res, ts // ROWS_PER_WORD, Q_COLS), lambda b, r, i: (b, r, i, 0)),
        pl.BlockSpec((None, n_res, L // ROWS_PER_WORD, KVX_COLS), lambda b, r, i: (b, r, 0, 0)),
        pl.BlockSpec((None, n_res, L // ROWS_PER_WORD, KVX_COLS), lambda b, r, i: (b, r, 0, 0)),
        _const_spec(bias.shape),
    ]
    args = [q, kx, vx, bias]
    if sink is not None:
        in_specs.append(pl.BlockSpec(memory_space=pltpu.SMEM))
        args.append(sink.astype(jnp.float32) * LOG2_E)
    if dil == 1:
        out_specs = [pl.BlockSpec((None, ts, D), lambda b, r, i: (b, i, 0))]
        out_shape = [jax.ShapeDtypeStruct((B, S, D), jnp.bfloat16)]
        lse_spec = pl.BlockSpec((None, ts, LANES), lambda b, r, i: (b, i, 0))
    else:
        assert want_lse
        out_specs = [pl.BlockSpec((None, D // LANES, S, LANES), lambda b, r, i: (b, 0, 0, 0))]
        out_shape = [jax.ShapeDtypeStruct((B, D // LANES, S, LANES), jnp.float32)]
        lse_spec = pl.BlockSpec((None, S, LANES), lambda b, r, i: (b, 0, 0))
    if want_lse:
        out_specs.append(lse_spec)
        out_shape.append(jax.ShapeDtypeStruct((B, S, LANES), jnp.float32))

    kernel = functools.partial(_attn_kernel, tq=tq, radius=radius, seq_len=L, dil=dil,
                               has_sink=sink is not None, want_lse=want_lse)
    out = pl.pallas_call(
        kernel,
        grid=(B, dil // n_res, L // ts),
        in_specs=in_specs,
        out_specs=out_specs,
        out_shape=out_shape,
        compiler_params=_params(3),
        name=f"band_attn_r{radius}_d{dil}",
    )(*args)
    if want_lse:
        return out[0], out[1]
    return out[0]


def _head_expand_matrix():
    e = np.zeros((LANES, Q_COLS), np.float32)
    for h in range(N_HEADS):
        e[_lse_lane(h), h * HEAD_DIM:(h + 1) * HEAD_DIM] = 1.0
    return jnp.asarray(np.concatenate([e, e], axis=0), jnp.bfloat16)


def _read_rows_f32(o_ref):
    if len(o_ref.shape) == 2:
        return o_ref[...].astype(jnp.float32)
    return jnp.concatenate([o_ref[j] for j in range(o_ref.shape[0])], axis=1)


def _ffn_kernel(*refs, n_mix, final, bp_in, bp_out):
    refs = list(refs)
    pos = 1 if bp_in is None else 2
    x = _read_rows(refs[:pos], bp_in)
    if n_mix >= 1:
        o_refs = refs[pos:pos + n_mix]
        pos += n_mix
        if n_mix > 1:
            lse_refs = refs[pos:pos + n_mix]
            expand_ref = refs[pos + n_mix]
            pos += n_mix + 1
        wo_ref = refs[pos]
        pos += 1
        if n_mix == 1:
            o = o_refs[0][...]
        else:
            lses = [r[...] for r in lse_refs]
            top = functools.reduce(jnp.maximum, lses)
            es = [jnp.exp(l - top) for l in lses]
            inv_tot = 1.0 / functools.reduce(lambda a, b: a + b, es)
            o = None
            w_rest = None
            for gi, (e, o_ref) in enumerate(zip(es, o_refs)):
                if gi < n_mix - 1:
                    w = e * inv_tot
                    w_hi = w.astype(jnp.bfloat16)
                    w_lo = (w - w_hi.astype(jnp.float32)).astype(jnp.bfloat16)
                    w_full = jnp.dot(jnp.concatenate([w_hi, w_lo], axis=1), expand_ref[...],
                                     preferred_element_type=jnp.float32)
                    w_rest = 1.0 - w_full if w_rest is None else w_rest - w_full
                else:
                    w_full = w_rest
                term = w_full * _read_rows_f32(o_ref)
                o = term if o is None else o + term
            o = o.astype(jnp.bfloat16)
        x = x + jnp.dot(o, wo_ref[...], preferred_element_type=jnp.float32)
    g_ref, w1_ref, w2_ref = refs[pos:pos + 3]
    pos += 3
    if final:
        fg_ref = refs[pos]
        pos += 1
    out_refs = refs[pos:]

    h = _rmsnorm_f32(x, g_ref[...]).astype(jnp.bfloat16)
    acc = x
    d_ff = w1_ref.shape[1]
    for c in range(d_ff // FFN_CHUNK):
        cols = slice(c * FFN_CHUNK, (c + 1) * FFN_CHUNK)
        a = jnp.dot(h, w1_ref[:, cols], preferred_element_type=jnp.float32)
        a = jnp.square(jnp.maximum(a, 0.0)).astype(jnp.bfloat16)
        acc = acc + jnp.dot(a, w2_ref[cols, :], preferred_element_type=jnp.float32)
    if final:
        acc = _rmsnorm_f32(acc, fg_ref[...])
    if bp_out is None:
        out_refs[0][...] = acc
    else:
        @pl.when(pl.program_id(0) < bp_out)
        def _():
            out_refs[0][...] = acc

        @pl.when(pl.program_id(0) >= bp_out)
        def _():
            out_refs[1][...] = acc


def _ffn(x, g, w1, w2, *, mix=None, final_g=None, split_out=None):
    B, S, D = _batch_shape(x)
    d_ff = w1.shape[1]
    many_blocks = (mix is not None and len(mix[0]) > 1) or isinstance(x, tuple) or split_out is not None
    tm = FFN_TM_MERGE if many_blocks else FFN_TM
    row_spec = lambda c: pl.BlockSpec((None, tm, c), lambda b, i: (b, i, 0))
    slab_spec = lambda c: pl.BlockSpec((None, c // LANES, tm, LANES), lambda b, i: (b, 0, i, 0))
    in_specs, args, bp_in = _row_input(x, tm)
    n_mix = 0
    if mix is not None:
        o_list, lse_list, w_o = mix
        n_mix = len(o_list)
        in_specs += [row_spec(D) if o.ndim == 3 else slab_spec(D) for o in o_list]
        args += o_list
        if n_mix > 1:
            in_specs += [row_spec(LANES)] * n_mix + [_const_spec((2 * LANES, Q_COLS))]
            args += lse_list + [_head_expand_matrix()]
        in_specs.append(_const_spec((D, D)))
        args.append(w_o)
    in_specs += [_const_spec((1, D)), _const_spec((D, d_ff)), _const_spec((d_ff, D))]
    args += [g.reshape(1, D), w1, w2]
    if final_g is not None:
        in_specs.append(_const_spec((1, D)))
        args.append(final_g.reshape(1, D))
    if split_out is None:
        out_specs = row_spec(D)
        out_shape = jax.ShapeDtypeStruct((B, S, D), jnp.float32)
    else:
        bp, last = split_out, S // tm - 1
        out_specs = [
            pl.BlockSpec((None, tm, D), lambda b, i: (jnp.minimum(b, bp - 1), jnp.where(b < bp, i, last), 0)),
            pl.BlockSpec((None, tm, D), lambda b, i: (jnp.maximum(b - bp, 0), jnp.where(b < bp, 0, i), 0))]
        out_shape = [jax.ShapeDtypeStruct((bp, S, D), jnp.float32),
                     jax.ShapeDtypeStruct((B - bp, S, D), jnp.float32)]
    out = pl.pallas_call(
        functools.partial(_ffn_kernel, n_mix=n_mix, final=final_g is not None, bp_in=bp_in,
                          bp_out=split_out),
        grid=(B, S // tm),
        in_specs=in_specs,
        out_specs=out_specs,
        out_shape=out_shape,
        compiler_params=_params(2),
        name=f"ffn_mix{n_mix}" + ("_final" if final_g is not None else ""),
    )(*args)
    return out


def _gmlp_kernel(x_ref, g_ref, win_ref, lng_ref, lnb_ref, ws_ref, bs_ref, wout_ref, o_ref):
    tm = x_ref.shape[0]
    hidden = wout_ref.shape[0]
    gcols = hidden // B_GROUPS
    subs = [slice(r0, r0 + GMLP_SUB) for r0 in range(0, tm, GMLP_SUB)]
    zs = []
    for rows in subs:
        h = _rmsnorm_f32(x_ref[rows, :], g_ref[...]).astype(jnp.bfloat16)
        zs.append(jnp.dot(h, win_ref[...], preferred_element_type=jnp.float32))
    for rows, z in zip(subs, zs):
        z = 0.5 * z * (1.0 + lax.erf(z * (2.0 ** -0.5)))
        u = z[:, :hidden]
        v = z[:, hidden:]
        mu = jnp.mean(v, axis=-1, keepdims=True)
        vc = v - mu
        var = jnp.mean(vc * vc, axis=-1, keepdims=True)
        vn = (vc * lax.rsqrt(var + LN_EPS) * lng_ref[...] + lnb_ref[...]).astype(jnp.bfloat16)
        gated = []
        for r0 in range(0, GMLP_SUB, B_CHUNK):
            parts = []
            for grp in range(B_GROUPS):
                parts.append(jnp.dot(ws_ref[grp], vn[r0:r0 + B_CHUNK, grp * gcols:(grp + 1) * gcols],
                                     preferred_element_type=jnp.float32))
            mixed = jnp.concatenate(parts, axis=-1) + bs_ref[...]
            gated.append((u[r0:r0 + B_CHUNK, :] * mixed).astype(jnp.bfloat16))
        t = jnp.concatenate(gated, axis=0)
        o_ref[rows, :] = x_ref[rows, :] + jnp.dot(t, wout_ref[...], preferred_element_type=jnp.float32)


def _gmlp(x, g, w_in, ln_g, ln_b, w_s, b_s, w_out):
    B, S, D = x.shape
    hidden = w_out.shape[0]
    T = B * S
    tm = GMLP_TM
    bs_full = jnp.repeat(jnp.transpose(b_s).astype(jnp.float32), hidden // B_GROUPS, axis=1)
    row_spec = pl.BlockSpec((tm, D), lambda i: (i, 0))
    out = pl.pallas_call(
        _gmlp_kernel,
        grid=(T // tm,),
        in_specs=[row_spec, _const_spec((1, D)), _const_spec((D, 2 * hidden)),
                  _const_spec((1, hidden)), _const_spec((1, hidden)),
                  _const_spec((B_GROUPS, B_CHUNK, B_CHUNK)), _const_spec((B_CHUNK, hidden)),
                  _const_spec((hidden, D))],
        out_specs=row_spec,
        out_shape=jax.ShapeDtypeStruct((T, D), jnp.float32),
        compiler_params=_params(1),
        name="gmlp",
    )(x.reshape(T, D), g.reshape(1, D), w_in, ln_g.reshape(1, hidden).astype(jnp.float32),
      ln_b.reshape(1, hidden).astype(jnp.float32), w_s, bs_full, w_out)
    return out.reshape(B, S, D)


def _trunk(x, p):
    depth = p["norm_mix_g"].shape[0]
    n_prompt = x[0].shape[0]
    for i in range(depth):
        kind, j = i % N_MIXERS, i // N_MIXERS
        g_mix = p["norm_mix_g"][i]
        mix = None
        if kind == 0:
            (qkv,) = _qkv_projection(x, g_mix, p["a_wqkv"][j], [1])
            o = _band_attention(qkv, p["a_bias"], A_RADIUS, sink=p["a_sink"][j])
            mix = ([o], None, p["a_wo"][j])
        elif kind == 1:
            assert not isinstance(x, tuple), "the gMLP kernel reads a single batch array"
            x = _gmlp(x, g_mix, p["b_win"][j], p["b_ln_g"][j], p["b_ln_b"][j], p["b_ws"][j],
                      p["b_bs"][j], p["b_wo"][j])
        else:
            o_list, lse_list = [], []
            dils = [dil for _, dil in C_GROUPS]
            qkvs = _qkv_projection(x, g_mix, p["c_wqkv"][j], dils)
            for gi, (window, dil) in enumerate(C_GROUPS):
                o, lse = _band_attention(qkvs[gi], p["c_bias"][gi], window // (2 * dil), want_lse=True)
                o_list.append(o)
                lse_list.append(lse)
            mix = (o_list, lse_list, p["c_wo"][j])
        last = i == depth - 1
        x = _ffn(x, p["norm_ffn_g"][i], p["ffn_w1"][i], p["ffn_w2"][i], mix=mix,
                 final_g=p["final_g"] if last else None, split_out=n_prompt if last else None)
    return tuple(x)


def kernel(x_prompt, x_sample, rel_bias, norm_mix_g, norm_ffn_g, final_g, ffn_w1, ffn_w2, a_wqkv,
           a_sink, a_wo, b_win, b_ln_g, b_ln_b, b_ws, b_bs, b_wo, c_wqkv, c_wo):
    bf16 = jnp.bfloat16
    f32 = jnp.float32
    S = x_prompt.shape[1]
    assert x_sample.shape[1] == S
    p = {
        "norm_mix_g": norm_mix_g.astype(f32), "norm_ffn_g": norm_ffn_g.astype(f32),
        "final_g": final_g.astype(f32),
        "ffn_w1": ffn_w1.astype(bf16), "ffn_w2": ffn_w2.astype(bf16),
        "a_wqkv": a_wqkv.astype(bf16), "a_sink": a_sink, "a_wo": a_wo.astype(bf16),
        "b_win": b_win.astype(bf16), "b_ln_g": b_ln_g, "b_ln_b": b_ln_b,
        "b_ws": b_ws.astype(bf16), "b_bs": b_bs, "b_wo": b_wo.astype(bf16),
        "c_wqkv": c_wqkv.astype(bf16), "c_wo": c_wo.astype(bf16),
        "a_bias": _band_bias_table(rel_bias, ATTN_TQ, A_RADIUS, 1, S),
        "c_bias": [_band_bias_table(rel_bias, ATTN_TQ, window // (2 * dil), dil, S // dil)
                   for window, dil in C_GROUPS],
    }
    return _trunk((x_prompt, x_sample), p)
```

```python
import functools
import math

import numpy as np
import jax
import jax.numpy as jnp
from jax import lax
from jax.experimental import pallas as pl
from jax.experimental.pallas import tpu as pltpu

HEAD_DIM = 64
N_MIXERS = 3
N_HEADS = 16
N_KV_HEADS = 4
A_RADIUS = 128
B_CHUNK = 128
B_GROUPS = 8
C_GROUPS = ((128, 1), (512, 4), (2048, 16))
NUM_BUCKETS = 32
REL_MAX_DISTANCE = 1024
RMS_EPS = 1e-6
LN_EPS = 1e-5

Q_COLS = N_HEADS * HEAD_DIM
KV_COLS = N_KV_HEADS * HEAD_DIM
QKV_COLS = Q_COLS + 2 * KV_COLS
LANES = 128
HEADS_PER_TILE = LANES // HEAD_DIM
Q_PER_KV = N_HEADS // N_KV_HEADS
KVX_COLS = N_KV_HEADS * HEADS_PER_TILE * LANES
MASK_VALUE = -1e30
LOG2_E = math.log2(math.e)
ROWS_PER_WORD = 2
LN_2 = math.log(2.0)

V7X_VMEM_BYTES = 64 * 1024 * 1024
VMEM_LIMIT_BYTES = V7X_VMEM_BYTES - 8 * 1024 * 1024

ATTN_TQ = 128
ATTN_STEP_ROWS = 2048
FFN_TM = 1024
FFN_TM_MERGE = 512
FFN_CHUNK = 1024
PROJ_TM = 1024
PROJ_TM_MULTI = 512
GMLP_TM = 1024
GMLP_SUB = 256


def _const_spec(shape):
    zeros = (0,) * len(shape)
    return pl.BlockSpec(shape, lambda *_: zeros, pipeline_mode=pl.Buffered(1))


def _params(n_grid_dims):
    return pltpu.CompilerParams(
        dimension_semantics=("arbitrary",) * n_grid_dims,
        vmem_limit_bytes=VMEM_LIMIT_BYTES,
    )


def _rmsnorm_f32(x, g):
    ms = jnp.mean(x * x, axis=-1, keepdims=True)
    return x * lax.rsqrt(ms + RMS_EPS) * g


def _row_input(x, tm):
    if not isinstance(x, tuple):
        return [pl.BlockSpec((None, tm, x.shape[-1]), lambda b, i: (b, i, 0))], [x], None
    xp, xs = x
    bp, S, D = xp.shape
    last = S // tm - 1
    spec_p = pl.BlockSpec((None, tm, D), lambda b, i: (jnp.minimum(b, bp - 1), jnp.where(b < bp, i, last), 0))
    spec_s = pl.BlockSpec((None, tm, D), lambda b, i: (jnp.maximum(b - bp, 0), jnp.where(b < bp, 0, i), 0))
    return [spec_p, spec_s], [xp, xs], bp


def _read_rows(x_refs, bp):
    if bp is None:
        return x_refs[0][...]
    return jnp.where(pl.program_id(0) < bp, x_refs[0][...], x_refs[1][...])


def _batch_shape(x):
    if isinstance(x, tuple):
        return (x[0].shape[0] + x[1].shape[0],) + x[0].shape[1:]
    return x.shape


def _lse_lane(head):
    return head // HEADS_PER_TILE + HEAD_DIM * (head % HEADS_PER_TILE)


def _proj_kernel(*refs, dils, bp):
    n = len(dils)
    n_x = 1 if bp is None else 2
    x_refs, refs = refs[:n_x], refs[n_x - 1:]
    g_ref = refs[1]
    w_refs = refs[2:2 + n]
    out_refs = refs[2 + n:2 + 4 * n]
    n_extra = 0 if bp is None else 1
    x_rows = _read_rows(x_refs, bp)
    if bp is not None:
        refs[2 + 4 * n][...] = x_rows
    hn = _rmsnorm_f32(x_rows, g_ref[...])
    tm, d_model = hn.shape
    n_slabs = d_model // LANES
    if any(d > 1 for d in dils):
        slab_ref = refs[2 + 4 * n + n_extra]
        for j in range(n_slabs):
            slab_ref[j] = hn[:, j * LANES:(j + 1) * LANES]
    low_half = lax.broadcasted_iota(jnp.int32, (tm, LANES), 1) < HEAD_DIM
    for gi, (dil, w_ref) in enumerate(zip(dils, w_refs)):
        q_ref, kx_ref, vx_ref = out_refs[3 * gi:3 * gi + 3]
        n_rows = tm // dil
        if dil == 1:
            h = hn
        else:
            h = jnp.concatenate(
                [jnp.concatenate([slab_ref[j, pl.ds(r, n_rows, stride=dil), :] for j in range(n_slabs)], axis=1)
                 for r in range(dil)], axis=0)
        hb = h.astype(jnp.bfloat16)
        y_kv = jnp.dot(hb, w_ref[:, Q_COLS:], preferred_element_type=jnp.float32)
        y_q = jnp.dot(hb, w_ref[:, :Q_COLS], preferred_element_type=jnp.float32)
        pieces = []
        for o_ref, col0 in ((kx_ref, 0), (vx_ref, KV_COLS)):
            tiles = []
            for t in range(KV_COLS // LANES):
                pair = y_kv[:, col0 + t * LANES:col0 + (t + 1) * LANES]
                only_a = jnp.where(low_half, pair, 0.0)
                only_b = jnp.where(low_half, 0.0, pair)
                tiles += [only_a, pltpu.roll(only_a, HEAD_DIM, 1), pltpu.roll(only_b, HEAD_DIM, 1), only_b]
            pieces.append((o_ref, jnp.concatenate(tiles, axis=1).astype(jnp.bfloat16)))
        pieces.append((q_ref, (y_q * (HEAD_DIM ** -0.5 * LOG2_E)).astype(jnp.bfloat16)))
        for o_ref, val in pieces:
            words = pltpu.bitcast(val, o_ref.dtype)
            n_words = n_rows // ROWS_PER_WORD
            for r in range(dil):
                o_ref[r] = words[r * n_words:(r + 1) * n_words]


def _qkv_projection(x, g, w, dils):
    B, S, D = _batch_shape(x)
    n = len(dils)
    tm = PROJ_TM if n == 1 else PROJ_TM_MULTI
    in_specs, x_args, bp = _row_input(x, tm)
    in_specs.append(_const_spec((1, D)))
    in_specs += [pl.BlockSpec((D, QKV_COLS), lambda b, i, gi=gi: (0, gi), pipeline_mode=pl.Buffered(1))
                 for gi in range(n)]
    out_specs, out_shape = [], []
    for dil in dils:
        for cols in (Q_COLS, KVX_COLS, KVX_COLS):
            out_specs.append(pl.BlockSpec((None, dil, tm // dil // ROWS_PER_WORD, cols),
                                          lambda b, i: (b, 0, i, 0)))
            out_shape.append(jax.ShapeDtypeStruct((B, dil, S // dil // ROWS_PER_WORD, cols), jnp.uint32))
    if bp is not None:
        out_specs.append(pl.BlockSpec((None, tm, D), lambda b, i: (b, i, 0)))
        out_shape.append(jax.ShapeDtypeStruct((B, S, D), x[0].dtype))
    scratch = []
    if any(d > 1 for d in dils):
        scratch.append(pltpu.VMEM((D // LANES, tm, LANES), jnp.float32))
    outs = pl.pallas_call(
        functools.partial(_proj_kernel, dils=tuple(dils), bp=bp),
        grid=(B, S // tm),
        in_specs=in_specs,
        out_specs=out_specs,
        out_shape=out_shape,
        scratch_shapes=scratch,
        compiler_params=_params(2),
        name="qkv_proj_" + "_".join(f"d{d}" for d in dils),
    )(*x_args, g.reshape(1, D), *([w] * n))
    groups = [tuple(outs[3 * gi:3 * gi + 3]) for gi in range(n)]
    return groups, (outs[3 * n] if bp is not None else x)


def _rel_bucket(rel):
    half = NUM_BUCKETS // 2
    max_exact = half // 2
    n = np.abs(rel)
    large = max_exact + (np.log(np.maximum(n, 1) / max_exact) / np.log(REL_MAX_DISTANCE / max_exact)
                         * (half - max_exact)).astype(np.int32)
    large = np.minimum(large, half - 1)
    return (rel > 0).astype(np.int32) * half + np.where(n < max_exact, n, large)


def _window_geometry(tq, radius, seq_len):
    width = min(tq + 2 * radius, seq_len)
    if seq_len // tq == 1:
        return width, (0,)
    return width, (0, radius, width - tq)


def _dot_heads(g, v):
    return g * Q_PER_KV + v, g * Q_PER_KV + HEADS_PER_TILE + v


def _band_bias_table(rel_bias, tq, radius, dil, seq_len):
    width, offsets = _window_geometry(tq, radius, seq_len)
    span = max(offsets) + tq - 1
    n_diag = span + width
    diag_rel = np.arange(n_diag) - span
    per_diag = jnp.transpose(rel_bias[_rel_bucket(diag_rel * dil)], (1, 0)).astype(jnp.float32) * LOG2_E
    per_diag = jnp.where(jnp.asarray(np.abs(diag_rel) <= radius)[None], per_diag, MASK_VALUE)
    padded = jnp.pad(per_diag, ((0, 0), (0, 1)))
    tiled = jnp.broadcast_to(padded[:, None, :], (N_HEADS, tq, n_diag + 1)).reshape(N_HEADS, -1)
    skewed = tiled[:, :tq * n_diag].reshape(N_HEADS, tq, n_diag)
    order = [h for g in range(N_KV_HEADS) for v in range(HEADS_PER_TILE) for h in _dot_heads(g, v)]
    skewed = skewed[np.asarray(order)]
    tables = []
    for off in offsets:
        c0 = span - off
        tables.append(skewed[:, :, c0:c0 + width].reshape(N_HEADS * tq, width))
    return jnp.stack(tables, axis=0)


def _attn_kernel(*refs, tq, radius, seq_len, dil, has_sink, want_lse):
    refs = list(refs)
    q_ref, kx_ref, vx_ref, bias_ref = refs[:4]
    pos = 4
    sink_ref = None
    if has_sink:
        sink_ref = refs[pos]
        pos += 1
    o_ref = refs[pos]
    pos += 1
    lse_ref = refs[pos] if want_lse else None

    n_res = q_ref.shape[0]
    blocks = q_ref.shape[1] * ROWS_PER_WORD // tq
    width, offsets = _window_geometry(tq, radius, seq_len)
    n_blk = seq_len // tq
    i = pl.program_id(2)
    bf16 = jnp.bfloat16

    lane = lax.broadcasted_iota(jnp.int32, (tq, LANES), 1)
    low_half2 = lax.broadcasted_iota(jnp.int32, (2 * tq, LANES), 1) < HEAD_DIM
    top_rows = lax.broadcasted_iota(jnp.int32, (2 * tq, 1), 0) < tq
    ones_lo = jnp.where(lax.broadcasted_iota(jnp.int32, (width, LANES), 1) < HEAD_DIM, 1.0, 0.0).astype(bf16)
    ones_hi = jnp.where(lax.broadcasted_iota(jnp.int32, (width, LANES), 1) < HEAD_DIM, 0.0, 1.0).astype(bf16)

    def window_start(j):
        blk = i * blocks + j
        if n_blk == 1:
            return 0
        return jnp.clip(blk * tq - radius, 0, seq_len - width)

    def load_rows(ref, res, start, n, lanes):
        word_start = start // ROWS_PER_WORD
        if not isinstance(word_start, int):
            word_start = pl.multiple_of(word_start, HEAD_DIM // ROWS_PER_WORD)
        words = ref[res, pl.ds(word_start, n // ROWS_PER_WORD), lanes]
        return pltpu.bitcast(words, bf16)

    def scores(res, j, g):
        base = g * Q_PER_KV * HEAD_DIM
        q2 = jnp.concatenate([load_rows(q_ref, res, j * tq, tq, slice(base, base + LANES)),
                              load_rows(q_ref, res, j * tq, tq, slice(base + LANES, base + 2 * LANES))],
                             axis=0)
        k_win = jnp.concatenate(
            [load_rows(kx_ref, res, window_start(j), width, slice(t * LANES, (t + 1) * LANES))
             for t in range(g * HEADS_PER_TILE, (g + 1) * HEADS_PER_TILE)], axis=0)
        return lax.dot_general(q2, k_win, (((1,), (1,)), ((), ())), preferred_element_type=jnp.float32)

    work = [(res, j, g) for res in range(n_res) for j in range(blocks) for g in range(N_KV_HEADS)]
    s_next = scores(*work[0])
    for n, (res, j, g) in enumerate(work):
        base = g * Q_PER_KV * HEAD_DIM
        rows = slice(j * tq, (j + 1) * tq)
        blk = i * blocks + j
        if n_blk == 1:
            variant = 0
        else:
            variant = jnp.where(blk == 0, 0, jnp.where(blk == n_blk - 1, 2, 1))
        if dil > 1:
            token_rows = pl.ds(blk * (tq * dil) + pl.program_id(1) * n_res + res, tq, stride=dil)
        s_all = s_next
        if n + 1 < len(work):
            s_next = scores(*work[n + 1])
        probs = []
        maxes = []
        sink_terms = []
        for v in range(HEADS_PER_TILE):
            t = g * HEADS_PER_TILE + v
            s = s_all[:, v * width:(v + 1) * width] + bias_ref[variant, t * 2 * tq:(t + 1) * 2 * tq, :]
            m = jnp.max(s, axis=-1, keepdims=True)
            if has_sink:
                h_top, h_bot = _dot_heads(g, v)
                sk = jnp.where(top_rows, sink_ref[h_top], sink_ref[h_bot])
                m = jnp.maximum(m, sk)
                sink_terms.append(jnp.exp2(sk - m))
            maxes.append(m)
            probs.append(jnp.exp2(s - m).astype(bf16))
        v_tiles = [load_rows(vx_ref, res, window_start(j), width, slice(t * LANES, (t + 1) * LANES))
                   for t in range(g * HEADS_PER_TILE, (g + 1) * HEADS_PER_TILE)]
        v_rhs = jnp.concatenate([jnp.concatenate([v_tiles[0], ones_lo], axis=1),
                                 jnp.concatenate([v_tiles[1], ones_hi], axis=1)], axis=0)
        acc = jnp.dot(jnp.concatenate(probs, axis=1), v_rhs, preferred_element_type=jnp.float32)
        denom = acc[:, LANES:]
        if has_sink:
            denom = denom + jnp.where(low_half2, sink_terms[0], sink_terms[1])
        o = acc[:, :LANES] / denom
        if dil == 1:
            o_ref[rows, base:base + LANES] = o[:tq].astype(o_ref.dtype)
            o_ref[rows, base + LANES:base + 2 * LANES] = o[tq:].astype(o_ref.dtype)
        else:
            o_ref[g * HEADS_PER_TILE, token_rows, :] = o[:tq]
            o_ref[g * HEADS_PER_TILE + 1, token_rows, :] = o[tq:]
        if want_lse:
            if g == 0:
                lse_acc = jnp.zeros((tq, LANES), jnp.float32)
            lse = jnp.where(low_half2, maxes[0] * LN_2, maxes[1] * LN_2) + jnp.log(denom)
            for half, half_rows in ((0, slice(0, tq)), (1, slice(tq, 2 * tq))):
                tile = g * HEADS_PER_TILE + half
                here = (lane == tile) | (lane == HEAD_DIM + tile)
                lse_acc = jnp.where(here, lse[half_rows], lse_acc)
            if g == N_KV_HEADS - 1:
                if dil == 1:
                    lse_ref[rows, :] = lse_acc
                else:
                    lse_ref[token_rows, :] = lse_acc


def _band_attention(qkv, bias, radius, *, sink=None, want_lse=False):
    q, kx, vx = qkv
    B, dil, L = q.shape[0], q.shape[1], q.shape[2] * ROWS_PER_WORD
    S = L * dil
    tq = ATTN_TQ
    ts = min(L, ATTN_STEP_ROWS)
    n_res = min(dil, ATTN_STEP_ROWS // ts)
    D = Q_COLS

    in_specs = [
        pl.BlockSpec((None, n_res, ts // ROWS_PER_WORD, Q_COLS), lambda b, r, i: (b, r, i, 0)),
        pl.BlockSpec((None, n_res, L // ROWS_PER_WORD, KVX_COLS), lambda b, r, i: (b, r, 0, 0)),
        pl.BlockSpec((None, n_res, L // ROWS_PER_WORD, KVX_COLS), lambda b, r, i: (b, r, 0, 0)),
        _const_spec(bias.shape),
    ]
    args = [q, kx, vx, bias]
    if sink is not None:
        in_specs.append(pl.BlockSpec(memory_space=pltpu.SMEM))
        args.append(sink.astype(jnp.float32) * LOG2_E)
    if dil == 1:
        out_specs = [pl.BlockSpec((None, ts, D), lambda b, r, i: (b, i, 0))]
        out_shape = [jax.ShapeDtypeStruct((B, S, D), jnp.bfloat16)]
        lse_spec = pl.BlockSpec((None, ts, LANES), lambda b, r, i: (b, i, 0))
    else:
        assert want_lse
        out_specs = [pl.BlockSpec((None, D // LANES, S, LANES), lambda b, r, i: (b, 0, 0, 0))]
        out_shape = [jax.ShapeDtypeStruct((B, D // LANES, S, LANES), jnp.float32)]
        lse_spec = pl.BlockSpec((None, S, LANES), lambda b, r, i: (b, 0, 0))
    if want_lse:
        out_specs.append(lse_spec)
        out_shape.append(jax.ShapeDtypeStruct((B, S, LANES), jnp.float32))

    kernel = functools.partial(_attn_kernel, tq=tq, radius=radius, seq_len=L, dil=dil,
                               has_sink=sink is not None, want_lse=want_lse)
    out = pl.pallas_call(
        kernel,
        grid=(B, dil // n_res, L // ts),
        in_specs=in_specs,
        out_specs=out_specs,
        out_shape=out_shape,
        compiler_params=_params(3),
        name=f"band_attn_r{radius}_d{dil}",
    )(*args)
    if want_lse:
        return out[0], out[1]
    return out[0]


def _head_expand_matrix():
    e = np.zeros((LANES, Q_COLS), np.float32)
    for h in range(N_HEADS):
        e[_lse_lane(h), h * HEAD_DIM:(h + 1) * HEAD_DIM] = 1.0
    return jnp.asarray(np.concatenate([e, e], axis=0), jnp.bfloat16)


def _read_rows_f32(o_ref):
    if len(o_ref.shape) == 2:
        return o_ref[...].astype(jnp.float32)
    return jnp.concatenate([o_ref[j] for j in range(o_ref.shape[0])], axis=1)


def _ffn_kernel(*refs, n_mix, final, bp_in, bp_out):
    refs = list(refs)
    pos = 1 if bp_in is None else 2
    x = _read_rows(refs[:pos], bp_in)
    if n_mix >= 1:
        o_refs = refs[pos:pos + n_mix]
        pos += n_mix
        if n_mix > 1:
            lse_refs = refs[pos:pos + n_mix]
            expand_ref = refs[pos + n_mix]
            pos += n_mix + 1
        wo_ref = refs[pos]
        pos += 1
        if n_mix == 1:
            o = o_refs[0][...]
        else:
            lses = [r[...] for r in lse_refs]
            top = functools.reduce(jnp.maximum, lses)
            es = [jnp.exp(l - top) for l in lses]
            inv_tot = 1.0 / functools.reduce(lambda a, b: a + b, es)
            o = None
            w_rest = None
            for gi, (e, o_ref) in enumerate(zip(es, o_refs)):
                if gi < n_mix - 1:
                    w = e * inv_tot
                    w_hi = w.astype(jnp.bfloat16)
                    w_lo = (w - w_hi.astype(jnp.float32)).astype(jnp.bfloat16)
                    w_full = jnp.dot(jnp.concatenate([w_hi, w_lo], axis=1), expand_ref[...],
                                     preferred_element_type=jnp.float32)
                    w_rest = 1.0 - w_full if w_rest is None else w_rest - w_full
                else:
                    w_full = w_rest
                term = w_full * _read_rows_f32(o_ref)
                o = term if o is None else o + term
            o = o.astype(jnp.bfloat16)
        x = x + jnp.dot(o, wo_ref[...], preferred_element_type=jnp.float32)
    g_ref, w1_ref, w2_ref = refs[pos:pos + 3]
    pos += 3
    if final:
        fg_ref = refs[pos]
        pos += 1
    out_refs = refs[pos:]

    h = _rmsnorm_f32(x, g_ref[...]).astype(jnp.bfloat16)
    acc = x
    d_ff = w1_ref.shape[1]
    for c in range(d_ff // FFN_CHUNK):
        cols = slice(c * FFN_CHUNK, (c + 1) * FFN_CHUNK)
        a = jnp.dot(h, w1_ref[:, cols], preferred_element_type=jnp.float32)
        a = jnp.square(jnp.maximum(a, 0.0)).astype(jnp.bfloat16)
        acc = acc + jnp.dot(a, w2_ref[cols, :], preferred_element_type=jnp.float32)
    if final:
        acc = _rmsnorm_f32(acc, fg_ref[...])
    if bp_out is None:
        out_refs[0][...] = acc
    else:
        @pl.when(pl.program_id(0) < bp_out)
        def _():
            out_refs[0][...] = acc

        @pl.when(pl.program_id(0) >= bp_out)
        def _():
            out_refs[1][...] = acc


def _ffn(x, g, w1, w2, *, mix=None, final_g=None, split_out=None):
    B, S, D = _batch_shape(x)
    d_ff = w1.shape[1]
    many_blocks = (mix is not None and len(mix[0]) > 1) or isinstance(x, tuple) or split_out is not None
    tm = FFN_TM_MERGE if many_blocks else FFN_TM
    row_spec = lambda c: pl.BlockSpec((None, tm, c), lambda b, i: (b, i, 0))
    slab_spec = lambda c: pl.BlockSpec((None, c // LANES, tm, LANES), lambda b, i: (b, 0, i, 0))
    in_specs, args, bp_in = _row_input(x, tm)
    n_mix = 0
    if mix is not None:
        o_list, lse_list, w_o = mix
        n_mix = len(o_list)
        in_specs += [row_spec(D) if o.ndim == 3 else slab_spec(D) for o in o_list]
        args += o_list
        if n_mix > 1:
            in_specs += [row_spec(LANES)] * n_mix + [_const_spec((2 * LANES, Q_COLS))]
            args += lse_list + [_head_expand_matrix()]
        in_specs.append(_const_spec((D, D)))
        args.append(w_o)
    in_specs += [_const_spec((1, D)), _const_spec((D, d_ff)), _const_spec((d_ff, D))]
    args += [g.reshape(1, D), w1, w2]
    if final_g is not None:
        in_specs.append(_const_spec((1, D)))
        args.append(final_g.reshape(1, D))
    if split_out is None:
        out_specs = row_spec(D)
        out_shape = jax.ShapeDtypeStruct((B, S, D), jnp.float32)
    else:
        bp, last = split_out, S // tm - 1
        out_specs = [
            pl.BlockSpec((None, tm, D), lambda b, i: (jnp.minimum(b, bp - 1), jnp.where(b < bp, i, last), 0)),
            pl.BlockSpec((None, tm, D), lambda b, i: (jnp.maximum(b - bp, 0), jnp.where(b < bp, 0, i), 0))]
        out_shape = [jax.ShapeDtypeStruct((bp, S, D), jnp.float32),
                     jax.ShapeDtypeStruct((B - bp, S, D), jnp.float32)]
    out = pl.pallas_call(
        functools.partial(_ffn_kernel, n_mix=n_mix, final=final_g is not None, bp_in=bp_in,
                          bp_out=split_out),
        grid=(B, S // tm),
        in_specs=in_specs,
        out_specs=out_specs,
        out_shape=out_shape,
        compiler_params=_params(2),
        name=f"ffn_mix{n_mix}" + ("_final" if final_g is not None else ""),
    )(*args)
    return out


def _gmlp_kernel(x_ref, g_ref, win_ref, lng_ref, lnb_ref, ws_ref, bs_ref, wout_ref, o_ref):
    tm = x_ref.shape[0]
    hidden = wout_ref.shape[0]
    gcols = hidden // B_GROUPS
    subs = [slice(r0, r0 + GMLP_SUB) for r0 in range(0, tm, GMLP_SUB)]
    zs = []
    for rows in subs:
        h = _rmsnorm_f32(x_ref[rows, :], g_ref[...]).astype(jnp.bfloat16)
        zs.append(jnp.dot(h, win_ref[...], preferred_element_type=jnp.float32))
    for rows, z in zip(subs, zs):
        z = 0.5 * z * (1.0 + lax.erf(z * (2.0 ** -0.5)))
        u = z[:, :hidden]
        v = z[:, hidden:]
        mu = jnp.mean(v, axis=-1, keepdims=True)
        vc = v - mu
        var = jnp.mean(vc * vc, axis=-1, keepdims=True)
        vn = (vc * lax.rsqrt(var + LN_EPS) * lng_ref[...] + lnb_ref[...]).astype(jnp.bfloat16)
        gated = []
        for r0 in range(0, GMLP_SUB, B_CHUNK):
            parts = []
            for grp in range(B_GROUPS):
                parts.append(jnp.dot(ws_ref[grp], vn[r0:r0 + B_CHUNK, grp * gcols:(grp + 1) * gcols],
                                     preferred_element_type=jnp.float32))
            mixed = jnp.concatenate(parts, axis=-1) + bs_ref[...]
            gated.append((u[r0:r0 + B_CHUNK, :] * mixed).astype(jnp.bfloat16))
        t = jnp.concatenate(gated, axis=0)
        o_ref[rows, :] = x_ref[rows, :] + jnp.dot(t, wout_ref[...], preferred_element_type=jnp.float32)


def _gmlp(x, g, w_in, ln_g, ln_b, w_s, b_s, w_out):
    B, S, D = x.shape
    hidden = w_out.shape[0]
    T = B * S
    tm = GMLP_TM
    bs_full = jnp.repeat(jnp.transpose(b_s).astype(jnp.float32), hidden // B_GROUPS, axis=1)
    row_spec = pl.BlockSpec((tm, D), lambda i: (i, 0))
    out = pl.pallas_call(
        _gmlp_kernel,
        grid=(T // tm,),
        in_specs=[row_spec, _const_spec((1, D)), _const_spec((D, 2 * hidden)),
                  _const_spec((1, hidden)), _const_spec((1, hidden)),
                  _const_spec((B_GROUPS, B_CHUNK, B_CHUNK)), _const_spec((B_CHUNK, hidden)),
                  _const_spec((hidden, D))],
        out_specs=row_spec,
        out_shape=jax.ShapeDtypeStruct((T, D), jnp.float32),
        compiler_params=_params(1),
        name="gmlp",
    )(x.reshape(T, D), g.reshape(1, D), w_in, ln_g.reshape(1, hidden).astype(jnp.float32),
      ln_b.reshape(1, hidden).astype(jnp.float32), w_s, bs_full, w_out)
    return out.reshape(B, S, D)


def _trunk(x, p):
    depth = p["norm_mix_g"].shape[0]
    n_prompt = x[0].shape[0]
    for i in range(depth):
        kind, j = i % N_MIXERS, i // N_MIXERS
        g_mix = p["norm_mix_g"][i]
        mix = None
        if kind == 0:
            (qkv,), x = _qkv_projection(x, g_mix, p["a_wqkv"][j], [1])
            o = _band_attention(qkv, p["a_bias"], A_RADIUS, sink=p["a_sink"][j])
            mix = ([o], None, p["a_wo"][j])
        elif kind == 1:
            assert not isinstance(x, tuple), "the gMLP kernel reads a single batch array"
            x = _gmlp(x, g_mix, p["b_win"][j], p["b_ln_g"][j], p["b_ln_b"][j], p["b_ws"][j],
                      p["b_bs"][j], p["b_wo"][j])
        else:
            o_list, lse_list = [], []
            dils = [dil for _, dil in C_GROUPS]
            qkvs, x = _qkv_projection(x, g_mix, p["c_wqkv"][j], dils)
            for gi, (window, dil) in enumerate(C_GROUPS):
                o, lse = _band_attention(qkvs[gi], p["c_bias"][gi], window // (2 * dil), want_lse=True)
                o_list.append(o)
                lse_list.append(lse)
            mix = (o_list, lse_list, p["c_wo"][j])
        last = i == depth - 1
        x = _ffn(x, p["norm_ffn_g"][i], p["ffn_w1"][i], p["ffn_w2"][i], mix=mix,
                 final_g=p["final_g"] if last else None, split_out=n_prompt if last else None)
    return tuple(x)


def kernel(x_prompt, x_sample, rel_bias, norm_mix_g, norm_ffn_g, final_g, ffn_w1, ffn_w2, a_wqkv,
           a_sink, a_wo, b_win, b_ln_g, b_ln_b, b_ws, b_bs, b_wo, c_wqkv, c_wo):
    bf16 = jnp.bfloat16
    f32 = jnp.float32
    S = x_prompt.shape[1]
    assert x_sample.shape[1] == S
    p = {
        "norm_mix_g": norm_mix_g.astype(f32), "norm_ffn_g": norm_ffn_g.astype(f32),
        "final_g": final_g.astype(f32),
        "ffn_w1": ffn_w1.astype(bf16), "ffn_w2": ffn_w2.astype(bf16),
        "a_wqkv": a_wqkv.astype(bf16), "a_sink": a_sink, "a_wo": a_wo.astype(bf16),
        "b_win": b_win.astype(bf16), "b_ln_g": b_ln_g, "b_ln_b": b_ln_b,
        "b_ws": b_ws.astype(bf16), "b_bs": b_bs, "b_wo": b_wo.astype(bf16),
        "c_wqkv": c_wqkv.astype(bf16), "c_wo": c_wo.astype(bf16),
        "a_bias": _band_bias_table(rel_bias, ATTN_TQ, A_RADIUS, 1, S),
        "c_bias": [_band_bias_table(rel_bias, ATTN_TQ, window // (2 * dil), dil, S // dil)
                   for window, dil in C_GROUPS],
    }
    return _trunk((x_prompt, x_sample), p)
```

```python
import functools
import math

import numpy as np
import jax
import jax.numpy as jnp
from jax import lax
from jax.experimental import pallas as pl
from jax.experimental.pallas import tpu as pltpu

HEAD_DIM = 64
N_MIXERS = 3
N_HEADS = 16
N_KV_HEADS = 4
A_RADIUS = 128
B_CHUNK = 128
B_GROUPS = 8
C_GROUPS = ((128, 1), (512, 4), (2048, 16))
NUM_BUCKETS = 32
REL_MAX_DISTANCE = 1024
RMS_EPS = 1e-6
LN_EPS = 1e-5

Q_COLS = N_HEADS * HEAD_DIM
KV_COLS = N_KV_HEADS * HEAD_DIM
QKV_COLS = Q_COLS + 2 * KV_COLS
LANES = 128
HEADS_PER_TILE = LANES // HEAD_DIM
Q_PER_KV = N_HEADS // N_KV_HEADS
KVX_COLS = N_KV_HEADS * HEADS_PER_TILE * LANES
MASK_VALUE = -1e30
LOG2_E = math.log2(math.e)
ROWS_PER_WORD = 2
LN_2 = math.log(2.0)

V7X_VMEM_BYTES = 64 * 1024 * 1024
VMEM_LIMIT_BYTES = V7X_VMEM_BYTES - 4 * 1024 * 1024

ATTN_TQ = 128
ATTN_STEP_ROWS = 2048
FFN_TM = 1024
FFN_TM_MERGE = 512
FFN_CHUNK = 1024
PROJ_TM = 1024
PROJ_TM_MULTI = 512
GMLP_TM = 1024
GMLP_SUB = 256


def _const_spec(shape):
    zeros = (0,) * len(shape)
    return pl.BlockSpec(shape, lambda *_: zeros, pipeline_mode=pl.Buffered(1))


def _params(n_grid_dims):
    return pltpu.CompilerParams(
        dimension_semantics=("arbitrary",) * n_grid_dims,
        vmem_limit_bytes=VMEM_LIMIT_BYTES,
    )


def _rmsnorm_f32(x, g):
    ms = jnp.mean(x * x, axis=-1, keepdims=True)
    return x * lax.rsqrt(ms + RMS_EPS) * g


def _row_input(x, tm):
    if not isinstance(x, tuple):
        return [pl.BlockSpec((None, tm, x.shape[-1]), lambda b, i: (b, i, 0))], [x], None
    xp, xs = x
    bp, S, D = xp.shape
    last = S // tm - 1
    spec_p = pl.BlockSpec((None, tm, D), lambda b, i: (jnp.minimum(b, bp - 1), jnp.where(b < bp, i, last), 0))
    spec_s = pl.BlockSpec((None, tm, D), lambda b, i: (jnp.maximum(b - bp, 0), jnp.where(b < bp, 0, i), 0))
    return [spec_p, spec_s], [xp, xs], bp


def _read_rows(x_refs, bp):
    if bp is None:
        return x_refs[0][...]
    return jnp.where(pl.program_id(0) < bp, x_refs[0][...], x_refs[1][...])


def _batch_shape(x):
    if isinstance(x, tuple):
        return (x[0].shape[0] + x[1].shape[0],) + x[0].shape[1:]
    return x.shape


def _lse_lane(head):
    return head // HEADS_PER_TILE + HEAD_DIM * (head % HEADS_PER_TILE)


def _proj_kernel(*refs, dils, bp):
    n = len(dils)
    n_x = 1 if bp is None else 2
    x_refs, refs = refs[:n_x], refs[n_x - 1:]
    g_ref = refs[1]
    w_refs = refs[2:2 + n]
    out_refs = refs[2 + n:2 + 4 * n]
    n_extra = 0 if bp is None else 1
    x_rows = _read_rows(x_refs, bp)
    if bp is not None:
        refs[2 + 4 * n][...] = x_rows
    hn = _rmsnorm_f32(x_rows, g_ref[...])
    tm, d_model = hn.shape
    n_slabs = d_model // LANES
    if any(d > 1 for d in dils):
        slab_ref = refs[2 + 4 * n + n_extra]
        for j in range(n_slabs):
            slab_ref[j] = hn[:, j * LANES:(j + 1) * LANES]
    low_half = lax.broadcasted_iota(jnp.int32, (tm, LANES), 1) < HEAD_DIM
    for gi, (dil, w_ref) in enumerate(zip(dils, w_refs)):
        q_ref, kx_ref, vx_ref = out_refs[3 * gi:3 * gi + 3]
        n_rows = tm // dil
        if dil == 1:
            h = hn
        else:
            h = jnp.concatenate(
                [jnp.concatenate([slab_ref[j, pl.ds(r, n_rows, stride=dil), :] for j in range(n_slabs)], axis=1)
                 for r in range(dil)], axis=0)
        hb = h.astype(jnp.bfloat16)
        y_kv = jnp.dot(hb, w_ref[:, Q_COLS:], preferred_element_type=jnp.float32)
        y_q = jnp.dot(hb, w_ref[:, :Q_COLS], preferred_element_type=jnp.float32)
        pieces = []
        for o_ref, col0 in ((kx_ref, 0), (vx_ref, KV_COLS)):
            tiles = []
            for t in range(KV_COLS // LANES):
                pair = y_kv[:, col0 + t * LANES:col0 + (t + 1) * LANES]
                only_a = jnp.where(low_half, pair, 0.0)
                only_b = jnp.where(low_half, 0.0, pair)
                tiles += [only_a, pltpu.roll(only_a, HEAD_DIM, 1), pltpu.roll(only_b, HEAD_DIM, 1), only_b]
            pieces.append((o_ref, jnp.concatenate(tiles, axis=1).astype(jnp.bfloat16)))
        pieces.append((q_ref, (y_q * (HEAD_DIM ** -0.5 * LOG2_E)).astype(jnp.bfloat16)))
        for o_ref, val in pieces:
            words = pltpu.bitcast(val, o_ref.dtype)
            n_words = n_rows // ROWS_PER_WORD
            for r in range(dil):
                o_ref[r] = words[r * n_words:(r + 1) * n_words]


def _qkv_projection(x, g, w, dils):
    B, S, D = _batch_shape(x)
    n = len(dils)
    tm = PROJ_TM if n == 1 else PROJ_TM_MULTI
    in_specs, x_args, bp = _row_input(x, tm)
    in_specs.append(_const_spec((1, D)))
    in_specs += [pl.BlockSpec((D, QKV_COLS), lambda b, i, gi=gi: (0, gi), pipeline_mode=pl.Buffered(1))
                 for gi in range(n)]
    out_specs, out_shape = [], []
    for dil in dils:
        for cols in (Q_COLS, KVX_COLS, KVX_COLS):
            out_specs.append(pl.BlockSpec((None, dil, tm // dil // ROWS_PER_WORD, cols),
                                          lambda b, i: (b, 0, i, 0)))
            out_shape.append(jax.ShapeDtypeStruct((B, dil, S // dil // ROWS_PER_WORD, cols), jnp.uint32))
    if bp is not None:
        out_specs.append(pl.BlockSpec((None, tm, D), lambda b, i: (b, i, 0)))
        out_shape.append(jax.ShapeDtypeStruct((B, S, D), x[0].dtype))
    scratch = []
    if any(d > 1 for d in dils):
        scratch.append(pltpu.VMEM((D // LANES, tm, LANES), jnp.float32))
    outs = pl.pallas_call(
        functools.partial(_proj_kernel, dils=tuple(dils), bp=bp),
        grid=(B, S // tm),
        in_specs=in_specs,
        out_specs=out_specs,
        out_shape=out_shape,
        scratch_shapes=scratch,
        compiler_params=_params(2),
        name="qkv_proj_" + "_".join(f"d{d}" for d in dils),
    )(*x_args, g.reshape(1, D), *([w] * n))
    groups = [tuple(outs[3 * gi:3 * gi + 3]) for gi in range(n)]
    return groups, (outs[3 * n] if bp is not None else x)


def _rel_bucket(rel):
    half = NUM_BUCKETS // 2
    max_exact = half // 2
    n = np.abs(rel)
    large = max_exact + (np.log(np.maximum(n, 1) / max_exact) / np.log(REL_MAX_DISTANCE / max_exact)
                         * (half - max_exact)).astype(np.int32)
    large = np.minimum(large, half - 1)
    return (rel > 0).astype(np.int32) * half + np.where(n < max_exact, n, large)


def _window_geometry(tq, radius, seq_len):
    width = min(tq + 2 * radius, seq_len)
    if seq_len // tq == 1:
        return width, (0,)
    return width, (0, radius, width - tq)


def _dot_heads(g, v):
    return g * Q_PER_KV + v, g * Q_PER_KV + HEADS_PER_TILE + v


def _band_bias_table(rel_bias, tq, radius, dil, seq_len):
    width, offsets = _window_geometry(tq, radius, seq_len)
    span = max(offsets) + tq - 1
    n_diag = span + width
    diag_rel = np.arange(n_diag) - span
    per_diag = jnp.transpose(rel_bias[_rel_bucket(diag_rel * dil)], (1, 0)).astype(jnp.float32) * LOG2_E
    per_diag = jnp.where(jnp.asarray(np.abs(diag_rel) <= radius)[None], per_diag, MASK_VALUE)
    padded = jnp.pad(per_diag, ((0, 0), (0, 1)))
    tiled = jnp.broadcast_to(padded[:, None, :], (N_HEADS, tq, n_diag + 1)).reshape(N_HEADS, -1)
    skewed = tiled[:, :tq * n_diag].reshape(N_HEADS, tq, n_diag)
    order = [h for g in range(N_KV_HEADS) for v in range(HEADS_PER_TILE) for h in _dot_heads(g, v)]
    skewed = skewed[np.asarray(order)]
    tables = []
    for off in offsets:
        c0 = span - off
        tables.append(skewed[:, :, c0:c0 + width].reshape(N_HEADS * tq, width))
    return jnp.stack(tables, axis=0)


def _attn_kernel(*refs, tq, radius, seq_len, dil, has_sink, want_lse):
    refs = list(refs)
    q_ref, kx_ref, vx_ref, bias_ref = refs[:4]
    pos = 4
    sink_ref = None
    if has_sink:
        sink_ref = refs[pos]
        pos += 1
    o_ref = refs[pos]
    pos += 1
    lse_ref = refs[pos] if want_lse else None

    n_res = q_ref.shape[0]
    blocks = q_ref.shape[1] * ROWS_PER_WORD // tq
    width, offsets = _window_geometry(tq, radius, seq_len)
    n_blk = seq_len // tq
    i = pl.program_id(2)
    bf16 = jnp.bfloat16

    lane = lax.broadcasted_iota(jnp.int32, (tq, LANES), 1)
    low_half2 = lax.broadcasted_iota(jnp.int32, (2 * tq, LANES), 1) < HEAD_DIM
    top_rows = lax.broadcasted_iota(jnp.int32, (2 * tq, 1), 0) < tq
    ones_lo = jnp.where(lax.broadcasted_iota(jnp.int32, (width, LANES), 1) < HEAD_DIM, 1.0, 0.0).astype(bf16)
    ones_hi = jnp.where(lax.broadcasted_iota(jnp.int32, (width, LANES), 1) < HEAD_DIM, 0.0, 1.0).astype(bf16)

    def window_start(j):
        blk = i * blocks + j
        if n_blk == 1:
            return 0
        return jnp.clip(blk * tq - radius, 0, seq_len - width)

    def load_rows(ref, res, start, n, lanes):
        word_start = start // ROWS_PER_WORD
        if not isinstance(word_start, int):
            word_start = pl.multiple_of(word_start, HEAD_DIM // ROWS_PER_WORD)
        words = ref[res, pl.ds(word_start, n // ROWS_PER_WORD), lanes]
        return pltpu.bitcast(words, bf16)

    def scores(res, j, g):
        base = g * Q_PER_KV * HEAD_DIM
        q2 = jnp.concatenate([load_rows(q_ref, res, j * tq, tq, slice(base, base + LANES)),
                              load_rows(q_ref, res, j * tq, tq, slice(base + LANES, base + 2 * LANES))],
                             axis=0)
        k_win = jnp.concatenate(
            [load_rows(kx_ref, res, window_start(j), width, slice(t * LANES, (t + 1) * LANES))
             for t in range(g * HEADS_PER_TILE, (g + 1) * HEADS_PER_TILE)], axis=0)
        return lax.dot_general(q2, k_win, (((1,), (1,)), ((), ())), preferred_element_type=jnp.float32)

    work = [(res, j, g) for res in range(n_res) for j in range(blocks) for g in range(N_KV_HEADS)]
    s_next = scores(*work[0])
    for n, (res, j, g) in enumerate(work):
        base = g * Q_PER_KV * HEAD_DIM
        rows = slice(j * tq, (j + 1) * tq)
        blk = i * blocks + j
        if n_blk == 1:
            variant = 0
        else:
            variant = jnp.where(blk == 0, 0, jnp.where(blk == n_blk - 1, 2, 1))
        if dil > 1:
            token_rows = pl.ds(blk * (tq * dil) + pl.program_id(1) * n_res + res, tq, stride=dil)
        s_all = s_next
        if n + 1 < len(work):
            s_next = scores(*work[n + 1])
        probs = []
        maxes = []
        sink_terms = []
        for v in range(HEADS_PER_TILE):
            t = g * HEADS_PER_TILE + v
            s = s_all[:, v * width:(v + 1) * width] + bias_ref[variant, t * 2 * tq:(t + 1) * 2 * tq, :]
            m = jnp.max(s, axis=-1, keepdims=True)
            if has_sink:
                h_top, h_bot = _dot_heads(g, v)
                sk = jnp.where(top_rows, sink_ref[h_top], sink_ref[h_bot])
                m = jnp.maximum(m, sk)
                sink_terms.append(jnp.exp2(sk - m))
            maxes.append(m)
            probs.append(jnp.exp2(s - m).astype(bf16))
        v_tiles = [load_rows(vx_ref, res, window_start(j), width, slice(t * LANES, (t + 1) * LANES))
                   for t in range(g * HEADS_PER_TILE, (g + 1) * HEADS_PER_TILE)]
        v_rhs = jnp.concatenate([jnp.concatenate([v_tiles[0], ones_lo], axis=1),
                                 jnp.concatenate([v_tiles[1], ones_hi], axis=1)], axis=0)
        acc = jnp.dot(jnp.concatenate(probs, axis=1), v_rhs, preferred_element_type=jnp.float32)
        denom = acc[:, LANES:]
        if has_sink:
            denom = denom + jnp.where(low_half2, sink_terms[0], sink_terms[1])
        o = acc[:, :LANES] / denom
        if dil == 1:
            o_ref[rows, base:base + LANES] = o[:tq].astype(o_ref.dtype)
            o_ref[rows, base + LANES:base + 2 * LANES] = o[tq:].astype(o_ref.dtype)
        else:
            o_ref[g * HEADS_PER_TILE, token_rows, :] = o[:tq]
            o_ref[g * HEADS_PER_TILE + 1, token_rows, :] = o[tq:]
        if want_lse:
            if g == 0:
                lse_acc = jnp.zeros((tq, LANES), jnp.float32)
            lse = jnp.where(low_half2, maxes[0] * LN_2, maxes[1] * LN_2) + jnp.log(denom)
            for half, half_rows in ((0, slice(0, tq)), (1, slice(tq, 2 * tq))):
                tile = g * HEADS_PER_TILE + half
                here = (lane == tile) | (lane == HEAD_DIM + tile)
                lse_acc = jnp.where(here, lse[half_rows], lse_acc)
            if g == N_KV_HEADS - 1:
                if dil == 1:
                    lse_ref[rows, :] = lse_acc
                else:
                    lse_ref[token_rows, :] = lse_acc


def _band_attention(qkv, bias, radius, *, sink=None, want_lse=False):
    q, kx, vx = qkv
    B, dil, L = q.shape[0], q.shape[1], q.shape[2] * ROWS_PER_WORD
    S = L * dil
    tq = ATTN_TQ
    ts = min(L, ATTN_STEP_ROWS)
    n_res = min(dil, ATTN_STEP_ROWS // ts)
    D = Q_COLS

    in_specs = [
        pl.BlockSpec((None, n_res, ts // ROWS_PER_WORD, Q_COLS), lambda b, r, i: (b, r, i, 0)),
        pl.BlockSpec((None, n_res, L // ROWS_PER_WORD, KVX_COLS), lambda b, r, i: (b, r, 0, 0)),
        pl.BlockSpec((None, n_res, L // ROWS_PER_WORD, KVX_COLS), lambda b, r, i: (b, r, 0, 0)),
        _const_spec(bias.shape),
    ]
    args = [q, kx, vx, bias]
    if sink is not None:
        in_specs.append(pl.BlockSpec(memory_space=pltpu.SMEM))
        args.append(sink.astype(jnp.float32) * LOG2_E)
    if dil == 1:
        out_specs = [pl.BlockSpec((None, ts, D), lambda b, r, i: (b, i, 0))]
        out_shape = [jax.ShapeDtypeStruct((B, S, D), jnp.bfloat16)]
        lse_spec = pl.BlockSpec((None, ts, LANES), lambda b, r, i: (b, i, 0))
    else:
        assert want_lse
        out_specs = [pl.BlockSpec((None, D // LANES, S, LANES), lambda b, r, i: (b, 0, 0, 0))]
        out_shape = [jax.ShapeDtypeStruct((B, D // LANES, S, LANES), jnp.float32)]
        lse_spec = pl.BlockSpec((None, S, LANES), lambda b, r, i: (b, 0, 0))
    if want_lse:
        out_specs.append(lse_spec)
        out_shape.append(jax.ShapeDtypeStruct((B, S, LANES), jnp.float32))

    kernel = functools.partial(_attn_kernel, tq=tq, radius=radius, seq_len=L, dil=dil,
                               has_sink=sink is not None, want_lse=want_lse)
    out = pl.pallas_call(
        kernel,
        grid=(B, dil // n_res, L // ts),
        in_specs=in_specs,
        out_specs=out_specs,
        out_shape=out_shape,
        compiler_params=_params(3),
        name=f"band_attn_r{radius}_d{dil}",
    )(*args)
    if want_lse:
        return out[0], out[1]
    return out[0]


def _head_expand_matrix():
    e = np.zeros((LANES, Q_COLS), np.float32)
    for h in range(N_HEADS):
        e[_lse_lane(h), h * HEAD_DIM:(h + 1) * HEAD_DIM] = 1.0
    return jnp.asarray(np.concatenate([e, e], axis=0), jnp.bfloat16)


def _read_rows_f32(o_ref):
    if len(o_ref.shape) == 2:
        return o_ref[...].astype(jnp.float32)
    return jnp.concatenate([o_ref[j] for j in range(o_ref.shape[0])], axis=1)


def _ffn_kernel(*refs, n_mix, final, bp_in, bp_out):
    refs = list(refs)
    pos = 1 if bp_in is None else 2
    x = _read_rows(refs[:pos], bp_in)
    if n_mix >= 1:
        o_refs = refs[pos:pos + n_mix]
        pos += n_mix
        if n_mix > 1:
            lse_refs = refs[pos:pos + n_mix]
            expand_ref = refs[pos + n_mix]
            pos += n_mix + 1
        wo_ref = refs[pos]
        pos += 1
        if n_mix == 1:
            o = o_refs[0][...]
        else:
            lses = [r[...] for r in lse_refs]
            top = functools.reduce(jnp.maximum, lses)
            es = [jnp.exp(l - top) for l in lses]
            inv_tot = 1.0 / functools.reduce(lambda a, b: a + b, es)
            o = None
            w_rest = None
            for gi, (e, o_ref) in enumerate(zip(es, o_refs)):
                if gi < n_mix - 1:
                    w = e * inv_tot
                    w_hi = w.astype(jnp.bfloat16)
                    w_lo = (w - w_hi.astype(jnp.float32)).astype(jnp.bfloat16)
                    w_full = jnp.dot(jnp.concatenate([w_hi, w_lo], axis=1), expand_ref[...],
                                     preferred_element_type=jnp.float32)
                    w_rest = 1.0 - w_full if w_rest is None else w_rest - w_full
                else:
                    w_full = w_rest
                term = w_full * _read_rows_f32(o_ref)
                o = term if o is None else o + term
            o = o.astype(jnp.bfloat16)
        x = x + jnp.dot(o, wo_ref[...], preferred_element_type=jnp.float32)
    g_ref, w1_ref, w2_ref = refs[pos:pos + 3]
    pos += 3
    if final:
        fg_ref = refs[pos]
        pos += 1
    out_refs = refs[pos:]

    h = _rmsnorm_f32(x, g_ref[...]).astype(jnp.bfloat16)
    acc = x
    d_ff = w1_ref.shape[1]
    for c in range(d_ff // FFN_CHUNK):
        cols = slice(c * FFN_CHUNK, (c + 1) * FFN_CHUNK)
        a = jnp.dot(h, w1_ref[:, cols], preferred_element_type=jnp.float32)
        a = jnp.square(jnp.maximum(a, 0.0)).astype(jnp.bfloat16)
        acc = acc + jnp.dot(a, w2_ref[cols, :], preferred_element_type=jnp.float32)
    if final:
        acc = _rmsnorm_f32(acc, fg_ref[...])
    if bp_out is None:
        out_refs[0][...] = acc
    else:
        @pl.when(pl.program_id(0) < bp_out)
        def _():
            out_refs[0][...] = acc

        @pl.when(pl.program_id(0) >= bp_out)
        def _():
            out_refs[1][...] = acc


def _ffn(x, g, w1, w2, *, mix=None, final_g=None, split_out=None):
    B, S, D = _batch_shape(x)
    d_ff = w1.shape[1]
    many_blocks = (mix is not None and len(mix[0]) > 1) or isinstance(x, tuple)
    tm = FFN_TM_MERGE if many_blocks else FFN_TM
    row_spec = lambda c: pl.BlockSpec((None, tm, c), lambda b, i: (b, i, 0))
    slab_spec = lambda c: pl.BlockSpec((None, c // LANES, tm, LANES), lambda b, i: (b, 0, i, 0))
    in_specs, args, bp_in = _row_input(x, tm)
    n_mix = 0
    if mix is not None:
        o_list, lse_list, w_o = mix
        n_mix = len(o_list)
        in_specs += [row_spec(D) if o.ndim == 3 else slab_spec(D) for o in o_list]
        args += o_list
        if n_mix > 1:
            in_specs += [row_spec(LANES)] * n_mix + [_const_spec((2 * LANES, Q_COLS))]
            args += lse_list + [_head_expand_matrix()]
        in_specs.append(_const_spec((D, D)))
        args.append(w_o)
    in_specs += [_const_spec((1, D)), _const_spec((D, d_ff)), _const_spec((d_ff, D))]
    args += [g.reshape(1, D), w1, w2]
    if final_g is not None:
        in_specs.append(_const_spec((1, D)))
        args.append(final_g.reshape(1, D))
    if split_out is None:
        out_specs = row_spec(D)
        out_shape = jax.ShapeDtypeStruct((B, S, D), jnp.float32)
    else:
        bp, last = split_out, S // tm - 1
        out_specs = [
            pl.BlockSpec((None, tm, D), lambda b, i: (jnp.minimum(b, bp - 1), jnp.where(b < bp, i, last), 0)),
            pl.BlockSpec((None, tm, D), lambda b, i: (jnp.maximum(b - bp, 0), jnp.where(b < bp, 0, i), 0))]
        out_shape = [jax.ShapeDtypeStruct((bp, S, D), jnp.float32),
                     jax.ShapeDtypeStruct((B - bp, S, D), jnp.float32)]
    out = pl.pallas_call(
        functools.partial(_ffn_kernel, n_mix=n_mix, final=final_g is not None, bp_in=bp_in,
                          bp_out=split_out),
        grid=(B, S // tm),
        in_specs=in_specs,
        out_specs=out_specs,
        out_shape=out_shape,
        compiler_params=_params(2),
        name=f"ffn_mix{n_mix}" + ("_final" if final_g is not None else ""),
    )(*args)
    return out


def _gmlp_kernel(x_ref, g_ref, win_ref, lng_ref, lnb_ref, ws_ref, bs_ref, wout_ref, o_ref):
    tm = x_ref.shape[0]
    hidden = wout_ref.shape[0]
    gcols = hidden // B_GROUPS
    subs = [slice(r0, r0 + GMLP_SUB) for r0 in range(0, tm, GMLP_SUB)]
    zs = []
    for rows in subs:
        h = _rmsnorm_f32(x_ref[rows, :], g_ref[...]).astype(jnp.bfloat16)
        zs.append(jnp.dot(h, win_ref[...], preferred_element_type=jnp.float32))
    for rows, z in zip(subs, zs):
        z = 0.5 * z * (1.0 + lax.erf(z * (2.0 ** -0.5)))
        u = z[:, :hidden]
        v = z[:, hidden:]
        mu = jnp.mean(v, axis=-1, keepdims=True)
        vc = v - mu
        var = jnp.mean(vc * vc, axis=-1, keepdims=True)
        vn = (vc * lax.rsqrt(var + LN_EPS) * lng_ref[...] + lnb_ref[...]).astype(jnp.bfloat16)
        gated = []
        for r0 in range(0, GMLP_SUB, B_CHUNK):
            parts = []
            for grp in range(B_GROUPS):
                parts.append(jnp.dot(ws_ref[grp], vn[r0:r0 + B_CHUNK, grp * gcols:(grp + 1) * gcols],
                                     preferred_element_type=jnp.float32))
            mixed = jnp.concatenate(parts, axis=-1) + bs_ref[...]
            gated.append((u[r0:r0 + B_CHUNK, :] * mixed).astype(jnp.bfloat16))
        t = jnp.concatenate(gated, axis=0)
        o_ref[rows, :] = x_ref[rows, :] + jnp.dot(t, wout_ref[...], preferred_element_type=jnp.float32)


def _gmlp(x, g, w_in, ln_g, ln_b, w_s, b_s, w_out):
    B, S, D = x.shape
    hidden = w_out.shape[0]
    T = B * S
    tm = GMLP_TM
    bs_full = jnp.repeat(jnp.transpose(b_s).astype(jnp.float32), hidden // B_GROUPS, axis=1)
    row_spec = pl.BlockSpec((tm, D), lambda i: (i, 0))
    out = pl.pallas_call(
        _gmlp_kernel,
        grid=(T // tm,),
        in_specs=[row_spec, _const_spec((1, D)), _const_spec((D, 2 * hidden)),
                  _const_spec((1, hidden)), _const_spec((1, hidden)),
                  _const_spec((B_GROUPS, B_CHUNK, B_CHUNK)), _const_spec((B_CHUNK, hidden)),
                  _const_spec((hidden, D))],
        out_specs=row_spec,
        out_shape=jax.ShapeDtypeStruct((T, D), jnp.float32),
        compiler_params=_params(1),
        name="gmlp",
    )(x.reshape(T, D), g.reshape(1, D), w_in, ln_g.reshape(1, hidden).astype(jnp.float32),
      ln_b.reshape(1, hidden).astype(jnp.float32), w_s, bs_full, w_out)
    return out.reshape(B, S, D)


def _trunk(x, p):
    depth = p["norm_mix_g"].shape[0]
    n_prompt = x[0].shape[0]
    for i in range(depth):
        kind, j = i % N_MIXERS, i // N_MIXERS
        g_mix = p["norm_mix_g"][i]
        mix = None
        if kind == 0:
            (qkv,), x = _qkv_projection(x, g_mix, p["a_wqkv"][j], [1])
            o = _band_attention(qkv, p["a_bias"], A_RADIUS, sink=p["a_sink"][j])
            mix = ([o], None, p["a_wo"][j])
        elif kind == 1:
            assert not isinstance(x, tuple), "the gMLP kernel reads a single batch array"
            x = _gmlp(x, g_mix, p["b_win"][j], p["b_ln_g"][j], p["b_ln_b"][j], p["b_ws"][j],
                      p["b_bs"][j], p["b_wo"][j])
        else:
            o_list, lse_list = [], []
            dils = [dil for _, dil in C_GROUPS]
            qkvs, x = _qkv_projection(x, g_mix, p["c_wqkv"][j], dils)
            for gi, (window, dil) in enumerate(C_GROUPS):
                o, lse = _band_attention(qkvs[gi], p["c_bias"][gi], window // (2 * dil), want_lse=True)
                o_list.append(o)
                lse_list.append(lse)
            mix = (o_list, lse_list, p["c_wo"][j])
        last = i == depth - 1
        x = _ffn(x, p["norm_ffn_g"][i], p["ffn_w1"][i], p["ffn_w2"][i], mix=mix,
                 final_g=p["final_g"] if last else None, split_out=n_prompt if last else None)
    return tuple(x)


def kernel(x_prompt, x_sample, rel_bias, norm_mix_g, norm_ffn_g, final_g, ffn_w1, ffn_w2, a_wqkv,
           a_sink, a_wo, b_win, b_ln_g, b_ln_b, b_ws, b_bs, b_wo, c_wqkv, c_wo):
    bf16 = jnp.bfloat16
    f32 = jnp.float32
    S = x_prompt.shape[1]
    assert x_sample.shape[1] == S
    p = {
        "norm_mix_g": norm_mix_g.astype(f32), "norm_ffn_g": norm_ffn_g.astype(f32),
        "final_g": final_g.astype(f32),
        "ffn_w1": ffn_w1.astype(bf16), "ffn_w2": ffn_w2.astype(bf16),
        "a_wqkv": a_wqkv.astype(bf16), "a_sink": a_sink, "a_wo": a_wo.astype(bf16),
        "b_win": b_win.astype(bf16), "b_ln_g": b_ln_g, "b_ln_b": b_ln_b,
        "b_ws": b_ws.astype(bf16), "b_bs": b_bs, "b_wo": b_wo.astype(bf16),
        "c_wqkv": c_wqkv.astype(bf16), "c_wo": c_wo.astype(bf16),
        "a_bias": _band_bias_table(rel_bias, ATTN_TQ, A_RADIUS, 1, S),
        "c_bias": [_band_bias_table(rel_bias, ATTN_TQ, window // (2 * dil), dil, S // dil)
                   for window, dil in C_GROUPS],
    }
    return _trunk((x_prompt, x_sample), p)
```

```python
import functools
import math

import numpy as np
import jax
import jax.numpy as jnp
from jax import lax
from jax.experimental import pallas as pl
from jax.experimental.pallas import tpu as pltpu

HEAD_DIM = 64
N_MIXERS = 3
N_HEADS = 16
N_KV_HEADS = 4
A_RADIUS = 128
B_CHUNK = 128
B_GROUPS = 8
C_GROUPS = ((128, 1), (512, 4), (2048, 16))
NUM_BUCKETS = 32
REL_MAX_DISTANCE = 1024
RMS_EPS = 1e-6
LN_EPS = 1e-5

Q_COLS = N_HEADS * HEAD_DIM
KV_COLS = N_KV_HEADS * HEAD_DIM
QKV_COLS = Q_COLS + 2 * KV_COLS
LANES = 128
HEADS_PER_TILE = LANES // HEAD_DIM
Q_PER_KV = N_HEADS // N_KV_HEADS
KVX_COLS = N_KV_HEADS * HEADS_PER_TILE * LANES
MASK_VALUE = -1e30
LOG2_E = math.log2(math.e)
ROWS_PER_WORD = 2
LN_2 = math.log(2.0)

V7X_VMEM_BYTES = 64 * 1024 * 1024
VMEM_LIMIT_BYTES = V7X_VMEM_BYTES - 8 * 1024 * 1024

ATTN_TQ = 128
ATTN_STEP_ROWS = 2048
FFN_TM = 1024
FFN_TM_MERGE = 512
FFN_CHUNK = 2048
PROJ_TM = 1024
PROJ_TM_MULTI = 512
GMLP_TM = 1024
GMLP_SUB = 256


def _const_spec(shape):
    zeros = (0,) * len(shape)
    return pl.BlockSpec(shape, lambda *_: zeros, pipeline_mode=pl.Buffered(1))


def _params(n_grid_dims):
    return pltpu.CompilerParams(
        dimension_semantics=("arbitrary",) * n_grid_dims,
        vmem_limit_bytes=VMEM_LIMIT_BYTES,
    )


def _rmsnorm_f32(x, g):
    ms = jnp.mean(x * x, axis=-1, keepdims=True)
    return x * lax.rsqrt(ms + RMS_EPS) * g


def _row_input(x, tm):
    if not isinstance(x, tuple):
        return [pl.BlockSpec((None, tm, x.shape[-1]), lambda b, i: (b, i, 0))], [x], None
    xp, xs = x
    bp, S, D = xp.shape
    last = S // tm - 1
    spec_p = pl.BlockSpec((None, tm, D), lambda b, i: (jnp.minimum(b, bp - 1), jnp.where(b < bp, i, last), 0))
    spec_s = pl.BlockSpec((None, tm, D), lambda b, i: (jnp.maximum(b - bp, 0), jnp.where(b < bp, 0, i), 0))
    return [spec_p, spec_s], [xp, xs], bp


def _read_rows(x_refs, bp):
    if bp is None:
        return x_refs[0][...]
    return jnp.where(pl.program_id(0) < bp, x_refs[0][...], x_refs[1][...])


def _batch_shape(x):
    if isinstance(x, tuple):
        return (x[0].shape[0] + x[1].shape[0],) + x[0].shape[1:]
    return x.shape


def _lse_lane(head):
    return head // HEADS_PER_TILE + HEAD_DIM * (head % HEADS_PER_TILE)


def _proj_kernel(*refs, dils, bp):
    n = len(dils)
    n_x = 1 if bp is None else 2
    x_refs, refs = refs[:n_x], refs[n_x - 1:]
    g_ref = refs[1]
    w_refs = refs[2:2 + n]
    out_refs = refs[2 + n:2 + 4 * n]
    n_extra = 0 if bp is None else 1
    x_rows = _read_rows(x_refs, bp)
    if bp is not None:
        refs[2 + 4 * n][...] = x_rows
    hn = _rmsnorm_f32(x_rows, g_ref[...])
    tm, d_model = hn.shape
    n_slabs = d_model // LANES
    if any(d > 1 for d in dils):
        slab_ref = refs[2 + 4 * n + n_extra]
        for j in range(n_slabs):
            slab_ref[j] = hn[:, j * LANES:(j + 1) * LANES]
    low_half = lax.broadcasted_iota(jnp.int32, (tm, LANES), 1) < HEAD_DIM
    for gi, (dil, w_ref) in enumerate(zip(dils, w_refs)):
        q_ref, kx_ref, vx_ref = out_refs[3 * gi:3 * gi + 3]
        n_rows = tm // dil
        if dil == 1:
            h = hn
        else:
            h = jnp.concatenate(
                [jnp.concatenate([slab_ref[j, pl.ds(r, n_rows, stride=dil), :] for j in range(n_slabs)], axis=1)
                 for r in range(dil)], axis=0)
        hb = h.astype(jnp.bfloat16)
        y_kv = jnp.dot(hb, w_ref[:, Q_COLS:], preferred_element_type=jnp.float32)
        y_q = jnp.dot(hb, w_ref[:, :Q_COLS], preferred_element_type=jnp.float32)
        pieces = []
        for o_ref, col0 in ((kx_ref, 0), (vx_ref, KV_COLS)):
            tiles = []
            for t in range(KV_COLS // LANES):
                pair = y_kv[:, col0 + t * LANES:col0 + (t + 1) * LANES]
                only_a = jnp.where(low_half, pair, 0.0)
                only_b = jnp.where(low_half, 0.0, pair)
                tiles += [only_a, pltpu.roll(only_a, HEAD_DIM, 1), pltpu.roll(only_b, HEAD_DIM, 1), only_b]
            pieces.append((o_ref, jnp.concatenate(tiles, axis=1).astype(jnp.bfloat16)))
        pieces.append((q_ref, (y_q * (HEAD_DIM ** -0.5 * LOG2_E)).astype(jnp.bfloat16)))
        for o_ref, val in pieces:
            words = pltpu.bitcast(val, o_ref.dtype)
            n_words = n_rows // ROWS_PER_WORD
            for r in range(dil):
                o_ref[r] = words[r * n_words:(r + 1) * n_words]


def _qkv_projection(x, g, w, dils):
    B, S, D = _batch_shape(x)
    n = len(dils)
    tm = PROJ_TM if n == 1 else PROJ_TM_MULTI
    in_specs, x_args, bp = _row_input(x, tm)
    in_specs.append(_const_spec((1, D)))
    in_specs += [pl.BlockSpec((D, QKV_COLS), lambda b, i, gi=gi: (0, gi), pipeline_mode=pl.Buffered(1))
                 for gi in range(n)]
    out_specs, out_shape = [], []
    for dil in dils:
        for cols in (Q_COLS, KVX_COLS, KVX_COLS):
            out_specs.append(pl.BlockSpec((None, dil, tm // dil // ROWS_PER_WORD, cols),
                                          lambda b, i: (b, 0, i, 0)))
            out_shape.append(jax.ShapeDtypeStruct((B, dil, S // dil // ROWS_PER_WORD, cols), jnp.uint32))
    if bp is not None:
        out_specs.append(pl.BlockSpec((None, tm, D), lambda b, i: (b, i, 0)))
        out_shape.append(jax.ShapeDtypeStruct((B, S, D), x[0].dtype))
    scratch = []
    if any(d > 1 for d in dils):
        scratch.append(pltpu.VMEM((D // LANES, tm, LANES), jnp.float32))
    outs = pl.pallas_call(
        functools.partial(_proj_kernel, dils=tuple(dils), bp=bp),
        grid=(B, S // tm),
        in_specs=in_specs,
        out_specs=out_specs,
        out_shape=out_shape,
        scratch_shapes=scratch,
        compiler_params=_params(2),
        name="qkv_proj_" + "_".join(f"d{d}" for d in dils),
    )(*x_args, g.reshape(1, D), *([w] * n))
    groups = [tuple(outs[3 * gi:3 * gi + 3]) for gi in range(n)]
    return groups, (outs[3 * n] if bp is not None else x)


def _rel_bucket(rel):
    half = NUM_BUCKETS // 2
    max_exact = half // 2
    n = np.abs(rel)
    large = max_exact + (np.log(np.maximum(n, 1) / max_exact) / np.log(REL_MAX_DISTANCE / max_exact)
                         * (half - max_exact)).astype(np.int32)
    large = np.minimum(large, half - 1)
    return (rel > 0).astype(np.int32) * half + np.where(n < max_exact, n, large)


def _window_geometry(tq, radius, seq_len):
    width = min(tq + 2 * radius, seq_len)
    if seq_len // tq == 1:
        return width, (0,)
    return width, (0, radius, width - tq)


def _dot_heads(g, v):
    return g * Q_PER_KV + v, g * Q_PER_KV + HEADS_PER_TILE + v


def _band_bias_table(rel_bias, tq, radius, dil, seq_len):
    width, offsets = _window_geometry(tq, radius, seq_len)
    span = max(offsets) + tq - 1
    n_diag = span + width
    diag_rel = np.arange(n_diag) - span
    per_diag = jnp.transpose(rel_bias[_rel_bucket(diag_rel * dil)], (1, 0)).astype(jnp.float32) * LOG2_E
    per_diag = jnp.where(jnp.asarray(np.abs(diag_rel) <= radius)[None], per_diag, MASK_VALUE)
    padded = jnp.pad(per_diag, ((0, 0), (0, 1)))
    tiled = jnp.broadcast_to(padded[:, None, :], (N_HEADS, tq, n_diag + 1)).reshape(N_HEADS, -1)
    skewed = tiled[:, :tq * n_diag].reshape(N_HEADS, tq, n_diag)
    order = [h for g in range(N_KV_HEADS) for v in range(HEADS_PER_TILE) for h in _dot_heads(g, v)]
    skewed = skewed[np.asarray(order)]
    tables = []
    for off in offsets:
        c0 = span - off
        tables.append(skewed[:, :, c0:c0 + width].reshape(N_HEADS * tq, width))
    return jnp.stack(tables, axis=0)


def _attn_kernel(*refs, tq, radius, seq_len, dil, has_sink, want_lse):
    refs = list(refs)
    q_ref, kx_ref, vx_ref, bias_ref = refs[:4]
    pos = 4
    sink_ref = None
    if has_sink:
        sink_ref = refs[pos]
        pos += 1
    o_ref = refs[pos]
    pos += 1
    lse_ref = refs[pos] if want_lse else None

    n_res = q_ref.shape[0]
    blocks = q_ref.shape[1] * ROWS_PER_WORD // tq
    width, offsets = _window_geometry(tq, radius, seq_len)
    n_blk = seq_len // tq
    i = pl.program_id(2)
    bf16 = jnp.bfloat16

    lane = lax.broadcasted_iota(jnp.int32, (tq, LANES), 1)
    low_half2 = lax.broadcasted_iota(jnp.int32, (2 * tq, LANES), 1) < HEAD_DIM
    top_rows = lax.broadcasted_iota(jnp.int32, (2 * tq, 1), 0) < tq
    ones_lo = jnp.where(lax.broadcasted_iota(jnp.int32, (width, LANES), 1) < HEAD_DIM, 1.0, 0.0).astype(bf16)
    ones_hi = jnp.where(lax.broadcasted_iota(jnp.int32, (width, LANES), 1) < HEAD_DIM, 0.0, 1.0).astype(bf16)

    def window_start(j):
        blk = i * blocks + j
        if n_blk == 1:
            return 0
        return jnp.clip(blk * tq - radius, 0, seq_len - width)

    def load_rows(ref, res, start, n, lanes):
        word_start = start // ROWS_PER_WORD
        if not isinstance(word_start, int):
            word_start = pl.multiple_of(word_start, HEAD_DIM // ROWS_PER_WORD)
        words = ref[res, pl.ds(word_start, n // ROWS_PER_WORD), lanes]
        return pltpu.bitcast(words, bf16)

    def scores(res, j, g):
        base = g * Q_PER_KV * HEAD_DIM
        q2 = jnp.concatenate([load_rows(q_ref, res, j * tq, tq, slice(base, base + LANES)),
                              load_rows(q_ref, res, j * tq, tq, slice(base + LANES, base + 2 * LANES))],
                             axis=0)
        k_win = jnp.concatenate(
            [load_rows(kx_ref, res, window_start(j), width, slice(t * LANES, (t + 1) * LANES))
             for t in range(g * HEADS_PER_TILE, (g + 1) * HEADS_PER_TILE)], axis=0)
        return lax.dot_general(q2, k_win, (((1,), (1,)), ((), ())), preferred_element_type=jnp.float32)

    work = [(res, j, g) for res in range(n_res) for j in range(blocks) for g in range(N_KV_HEADS)]
    s_next = scores(*work[0])
    for n, (res, j, g) in enumerate(work):
        base = g * Q_PER_KV * HEAD_DIM
        rows = slice(j * tq, (j + 1) * tq)
        blk = i * blocks + j
        if n_blk == 1:
            variant = 0
        else:
            variant = jnp.where(blk == 0, 0, jnp.where(blk == n_blk - 1, 2, 1))
        if dil > 1:
            token_rows = pl.ds(blk * (tq * dil) + pl.program_id(1) * n_res + res, tq, stride=dil)
        s_all = s_next
        if n + 1 < len(work):
            s_next = scores(*work[n + 1])
        probs = []
        maxes = []
        sink_terms = []
        for v in range(HEADS_PER_TILE):
            t = g * HEADS_PER_TILE + v
            s = s_all[:, v * width:(v + 1) * width] + bias_ref[variant, t * 2 * tq:(t + 1) * 2 * tq, :]
            m = jnp.max(s, axis=-1, keepdims=True)
            if has_sink:
                h_top, h_bot = _dot_heads(g, v)
                sk = jnp.where(top_rows, sink_ref[h_top], sink_ref[h_bot])
                m = jnp.maximum(m, sk)
                sink_terms.append(jnp.exp2(sk - m))
            maxes.append(m)
            probs.append(jnp.exp2(s - m).astype(bf16))
        v_tiles = [load_rows(vx_ref, res, window_start(j), width, slice(t * LANES, (t + 1) * LANES))
                   for t in range(g * HEADS_PER_TILE, (g + 1) * HEADS_PER_TILE)]
        v_rhs = jnp.concatenate([jnp.concatenate([v_tiles[0], ones_lo], axis=1),
                                 jnp.concatenate([v_tiles[1], ones_hi], axis=1)], axis=0)
        acc = jnp.dot(jnp.concatenate(probs, axis=1), v_rhs, preferred_element_type=jnp.float32)
        denom = acc[:, LANES:]
        if has_sink:
            denom = denom + jnp.where(low_half2, sink_terms[0], sink_terms[1])
        o = acc[:, :LANES] / denom
        if dil == 1:
            o_ref[rows, base:base + LANES] = o[:tq].astype(o_ref.dtype)
            o_ref[rows, base + LANES:base + 2 * LANES] = o[tq:].astype(o_ref.dtype)
        else:
            o_ref[g * HEADS_PER_TILE, token_rows, :] = o[:tq]
            o_ref[g * HEADS_PER_TILE + 1, token_rows, :] = o[tq:]
        if want_lse:
            if g == 0:
                lse_acc = jnp.zeros((tq, LANES), jnp.float32)
            lse = jnp.where(low_half2, maxes[0] * LN_2, maxes[1] * LN_2) + jnp.log(denom)
            for half, half_rows in ((0, slice(0, tq)), (1, slice(tq, 2 * tq))):
                tile = g * HEADS_PER_TILE + half
                here = (lane == tile) | (lane == HEAD_DIM + tile)
                lse_acc = jnp.where(here, lse[half_rows], lse_acc)
            if g == N_KV_HEADS - 1:
                if dil == 1:
                    lse_ref[rows, :] = lse_acc
                else:
                    lse_ref[token_rows, :] = lse_acc


def _band_attention(qkv, bias, radius, *, sink=None, want_lse=False):
    q, kx, vx = qkv
    B, dil, L = q.shape[0], q.shape[1], q.shape[2] * ROWS_PER_WORD
    S = L * dil
    tq = ATTN_TQ
    ts = min(L, ATTN_STEP_ROWS)
    n_res = min(dil, ATTN_STEP_ROWS // ts)
    D = Q_COLS

    in_specs = [
        pl.BlockSpec((None, n_res, ts // ROWS_PER_WORD, Q_COLS), lambda b, r, i: (b, r, i, 0)),
        pl.BlockSpec((None, n_res, L // ROWS_PER_WORD, KVX_COLS), lambda b, r, i: (b, r, 0, 0)),
        pl.BlockSpec((None, n_res, L // ROWS_PER_WORD, KVX_COLS), lambda b, r, i: (b, r, 0, 0)),
        _const_spec(bias.shape),
    ]
    args = [q, kx, vx, bias]
    if sink is not None:
        in_specs.append(pl.BlockSpec(memory_space=pltpu.SMEM))
        args.append(sink.astype(jnp.float32) * LOG2_E)
    if dil == 1:
        out_specs = [pl.BlockSpec((None, ts, D), lambda b, r, i: (b, i, 0))]
        out_shape = [jax.ShapeDtypeStruct((B, S, D), jnp.bfloat16)]
        lse_spec = pl.BlockSpec((None, ts, LANES), lambda b, r, i: (b, i, 0))
    else:
        assert want_lse
        out_specs = [pl.BlockSpec((None, D // LANES, S, LANES), lambda b, r, i: (b, 0, 0, 0))]
        out_shape = [jax.ShapeDtypeStruct((B, D // LANES, S, LANES), jnp.float32)]
        lse_spec = pl.BlockSpec((None, S, LANES), lambda b, r, i: (b, 0, 0))
    if want_lse:
        out_specs.append(lse_spec)
        out_shape.append(jax.ShapeDtypeStruct((B, S, LANES), jnp.float32))

    kernel = functools.partial(_attn_kernel, tq=tq, radius=radius, seq_len=L, dil=dil,
                               has_sink=sink is not None, want_lse=want_lse)
    out = pl.pallas_call(
        kernel,
        grid=(B, dil // n_res, L // ts),
        in_specs=in_specs,
        out_specs=out_specs,
        out_shape=out_shape,
        compiler_params=_params(3),
        name=f"band_attn_r{radius}_d{dil}",
    )(*args)
    if want_lse:
        return out[0], out[1]
    return out[0]


def _head_expand_matrix():
    e = np.zeros((LANES, Q_COLS), np.float32)
    for h in range(N_HEADS):
        e[_lse_lane(h), h * HEAD_DIM:(h + 1) * HEAD_DIM] = 1.0
    return jnp.asarray(np.concatenate([e, e], axis=0), jnp.bfloat16)


def _read_rows_f32(o_ref):
    if len(o_ref.shape) == 2:
        return o_ref[...].astype(jnp.float32)
    return jnp.concatenate([o_ref[j] for j in range(o_ref.shape[0])], axis=1)


def _ffn_kernel(*refs, n_mix, final, bp_in, bp_out):
    refs = list(refs)
    pos = 1 if bp_in is None else 2
    x = _read_rows(refs[:pos], bp_in)
    if n_mix >= 1:
        o_refs = refs[pos:pos + n_mix]
        pos += n_mix
        if n_mix > 1:
            lse_refs = refs[pos:pos + n_mix]
            expand_ref = refs[pos + n_mix]
            pos += n_mix + 1
        wo_ref = refs[pos]
        pos += 1
        if n_mix == 1:
            o = o_refs[0][...]
        else:
            lses = [r[...] for r in lse_refs]
            top = functools.reduce(jnp.maximum, lses)
            es = [jnp.exp(l - top) for l in lses]
            inv_tot = 1.0 / functools.reduce(lambda a, b: a + b, es)
            o = None
            w_rest = None
            for gi, (e, o_ref) in enumerate(zip(es, o_refs)):
                if gi < n_mix - 1:
                    w = e * inv_tot
                    w_hi = w.astype(jnp.bfloat16)
                    w_lo = (w - w_hi.astype(jnp.float32)).astype(jnp.bfloat16)
                    w_full = jnp.dot(jnp.concatenate([w_hi, w_lo], axis=1), expand_ref[...],
                                     preferred_element_type=jnp.float32)
                    w_rest = 1.0 - w_full if w_rest is None else w_rest - w_full
                else:
                    w_full = w_rest
                term = w_full * _read_rows_f32(o_ref)
                o = term if o is None else o + term
            o = o.astype(jnp.bfloat16)
        x = x + jnp.dot(o, wo_ref[...], preferred_element_type=jnp.float32)
    g_ref, w1_ref, w2_ref = refs[pos:pos + 3]
    pos += 3
    if final:
        fg_ref = refs[pos]
        pos += 1
    out_refs = refs[pos:]

    h = _rmsnorm_f32(x, g_ref[...]).astype(jnp.bfloat16)
    acc = x
    d_ff = w1_ref.shape[1]
    for c in range(d_ff // FFN_CHUNK):
        cols = slice(c * FFN_CHUNK, (c + 1) * FFN_CHUNK)
        a = jnp.dot(h, w1_ref[:, cols], preferred_element_type=jnp.float32)
        a = jnp.square(jnp.maximum(a, 0.0)).astype(jnp.bfloat16)
        acc = acc + jnp.dot(a, w2_ref[cols, :], preferred_element_type=jnp.float32)
    if final:
        acc = _rmsnorm_f32(acc, fg_ref[...])
    if bp_out is None:
        out_refs[0][...] = acc
    else:
        @pl.when(pl.program_id(0) < bp_out)
        def _():
            out_refs[0][...] = acc

        @pl.when(pl.program_id(0) >= bp_out)
        def _():
            out_refs[1][...] = acc


def _ffn(x, g, w1, w2, *, mix=None, final_g=None, split_out=None):
    B, S, D = _batch_shape(x)
    d_ff = w1.shape[1]
    many_blocks = (mix is not None and len(mix[0]) > 1) or isinstance(x, tuple) or split_out is not None
    tm = FFN_TM_MERGE if many_blocks else FFN_TM
    row_spec = lambda c: pl.BlockSpec((None, tm, c), lambda b, i: (b, i, 0))
    slab_spec = lambda c: pl.BlockSpec((None, c // LANES, tm, LANES), lambda b, i: (b, 0, i, 0))
    in_specs, args, bp_in = _row_input(x, tm)
    n_mix = 0
    if mix is not None:
        o_list, lse_list, w_o = mix
        n_mix = len(o_list)
        in_specs += [row_spec(D) if o.ndim == 3 else slab_spec(D) for o in o_list]
        args += o_list
        if n_mix > 1:
            in_specs += [row_spec(LANES)] * n_mix + [_const_spec((2 * LANES, Q_COLS))]
            args += lse_list + [_head_expand_matrix()]
        in_specs.append(_const_spec((D, D)))
        args.append(w_o)
    in_specs += [_const_spec((1, D)), _const_spec((D, d_ff)), _const_spec((d_ff, D))]
    args += [g.reshape(1, D), w1, w2]
    if final_g is not None:
        in_specs.append(_const_spec((1, D)))
        args.append(final_g.reshape(1, D))
    if split_out is None:
        out_specs = row_spec(D)
        out_shape = jax.ShapeDtypeStruct((B, S, D), jnp.float32)
    else:
        bp, last = split_out, S // tm - 1
        out_specs = [
            pl.BlockSpec((None, tm, D), lambda b, i: (jnp.minimum(b, bp - 1), jnp.where(b < bp, i, last), 0)),
            pl.BlockSpec((None, tm, D), lambda b, i: (jnp.maximum(b - bp, 0), jnp.where(b < bp, 0, i), 0))]
        out_shape = [jax.ShapeDtypeStruct((bp, S, D), jnp.float32),
                     jax.ShapeDtypeStruct((B - bp, S, D), jnp.float32)]
    out = pl.pallas_call(
        functools.partial(_ffn_kernel, n_mix=n_mix, final=final_g is not None, bp_in=bp_in,
                          bp_out=split_out),
        grid=(B, S // tm),
        in_specs=in_specs,
        out_specs=out_specs,
        out_shape=out_shape,
        compiler_params=_params(2),
        name=f"ffn_mix{n_mix}" + ("_final" if final_g is not None else ""),
    )(*args)
    return out


def _gmlp_kernel(x_ref, g_ref, win_ref, lng_ref, lnb_ref, ws_ref, bs_ref, wout_ref, o_ref):
    tm = x_ref.shape[0]
    hidden = wout_ref.shape[0]
    gcols = hidden // B_GROUPS
    subs = [slice(r0, r0 + GMLP_SUB) for r0 in range(0, tm, GMLP_SUB)]
    zs = []
    for rows in subs:
        h = _rmsnorm_f32(x_ref[rows, :], g_ref[...]).astype(jnp.bfloat16)
        zs.append(jnp.dot(h, win_ref[...], preferred_element_type=jnp.float32))
    for rows, z in zip(subs, zs):
        z = 0.5 * z * (1.0 + lax.erf(z * (2.0 ** -0.5)))
        u = z[:, :hidden]
        v = z[:, hidden:]
        mu = jnp.mean(v, axis=-1, keepdims=True)
        vc = v - mu
        var = jnp.mean(vc * vc, axis=-1, keepdims=True)
        vn = (vc * lax.rsqrt(var + LN_EPS) * lng_ref[...] + lnb_ref[...]).astype(jnp.bfloat16)
        gated = []
        for r0 in range(0, GMLP_SUB, B_CHUNK):
            parts = []
            for grp in range(B_GROUPS):
                parts.append(jnp.dot(ws_ref[grp], vn[r0:r0 + B_CHUNK, grp * gcols:(grp + 1) * gcols],
                                     preferred_element_type=jnp.float32))
            mixed = jnp.concatenate(parts, axis=-1) + bs_ref[...]
            gated.append((u[r0:r0 + B_CHUNK, :] * mixed).astype(jnp.bfloat16))
        t = jnp.concatenate(gated, axis=0)
        o_ref[rows, :] = x_ref[rows, :] + jnp.dot(t, wout_ref[...], preferred_element_type=jnp.float32)


def _gmlp(x, g, w_in, ln_g, ln_b, w_s, b_s, w_out):
    B, S, D = x.shape
    hidden = w_out.shape[0]
    T = B * S
    tm = GMLP_TM
    bs_full = jnp.repeat(jnp.transpose(b_s).astype(jnp.float32), hidden // B_GROUPS, axis=1)
    row_spec = pl.BlockSpec((tm, D), lambda i: (i, 0))
    out = pl.pallas_call(
        _gmlp_kernel,
        grid=(T // tm,),
        in_specs=[row_spec, _const_spec((1, D)), _const_spec((D, 2 * hidden)),
                  _const_spec((1, hidden)), _const_spec((1, hidden)),
                  _const_spec((B_GROUPS, B_CHUNK, B_CHUNK)), _const_spec((B_CHUNK, hidden)),
                  _const_spec((hidden, D))],
        out_specs=row_spec,
        out_shape=jax.ShapeDtypeStruct((T, D), jnp.float32),
        compiler_params=_params(1),
        name="gmlp",
    )(x.reshape(T, D), g.reshape(1, D), w_in, ln_g.reshape(1, hidden).astype(jnp.float32),
      ln_b.reshape(1, hidden).astype(jnp.float32), w_s, bs_full, w_out)
    return out.reshape(B, S, D)


def _trunk(x, p):
    depth = p["norm_mix_g"].shape[0]
    n_prompt = x[0].shape[0]
    for i in range(depth):
        kind, j = i % N_MIXERS, i // N_MIXERS
        g_mix = p["norm_mix_g"][i]
        mix = None
        if kind == 0:
            (qkv,), x = _qkv_projection(x, g_mix, p["a_wqkv"][j], [1])
            o = _band_attention(qkv, p["a_bias"], A_RADIUS, sink=p["a_sink"][j])
            mix = ([o], None, p["a_wo"][j])
        elif kind == 1:
            assert not isinstance(x, tuple), "the gMLP kernel reads a single batch array"
            x = _gmlp(x, g_mix, p["b_win"][j], p["b_ln_g"][j], p["b_ln_b"][j], p["b_ws"][j],
                      p["b_bs"][j], p["b_wo"][j])
        else:
            o_list, lse_list = [], []
            dils = [dil for _, dil in C_GROUPS]
            qkvs, x = _qkv_projection(x, g_mix, p["c_wqkv"][j], dils)
            for gi, (window, dil) in enumerate(C_GROUPS):
                o, lse = _band_attention(qkvs[gi], p["c_bias"][gi], window // (2 * dil), want_lse=True)
                o_list.append(o)
                lse_list.append(lse)
            mix = (o_list, lse_list, p["c_wo"][j])
        last = i == depth - 1
        x = _ffn(x, p["norm_ffn_g"][i], p["ffn_w1"][i], p["ffn_w2"][i], mix=mix,
                 final_g=p["final_g"] if last else None, split_out=n_prompt if last else None)
    return tuple(x)


def kernel(x_prompt, x_sample, rel_bias, norm_mix_g, norm_ffn_g, final_g, ffn_w1, ffn_w2, a_wqkv,
           a_sink, a_wo, b_win, b_ln_g, b_ln_b, b_ws, b_bs, b_wo, c_wqkv, c_wo):
    bf16 = jnp.bfloat16
    f32 = jnp.float32
    S = x_prompt.shape[1]
    assert x_sample.shape[1] == S
    p = {
        "norm_mix_g": norm_mix_g.astype(f32), "norm_ffn_g": norm_ffn_g.astype(f32),
        "final_g": final_g.astype(f32),
        "ffn_w1": ffn_w1.astype(bf16), "ffn_w2": ffn_w2.astype(bf16),
        "a_wqkv": a_wqkv.astype(bf16), "a_sink": a_sink, "a_wo": a_wo.astype(bf16),
        "b_win": b_win.astype(bf16), "b_ln_g": b_ln_g, "b_ln_b": b_ln_b,
        "b_ws": b_ws.astype(bf16), "b_bs": b_bs, "b_wo": b_wo.astype(bf16),
        "c_wqkv": c_wqkv.astype(bf16), "c_wo": c_wo.astype(bf16),
        "a_bias": _band_bias_table(rel_bias, ATTN_TQ, A_RADIUS, 1, S),
        "c_bias": [_band_bias_table(rel_bias, ATTN_TQ, window // (2 * dil), dil, S // dil)
                   for window, dil in C_GROUPS],
    }
    return _trunk((x_prompt, x_sample), p)
```
